```python
import math
import jax, jax.numpy as jnp
from jax import lax
import numpy as np

D_MODEL = 1024
BATCH = 8
SEQ = 2048
DEPTH = 4

ATTN_WIDTH = D_MODEL // 2
HEAD_DIM = 64
ATTN_HEADS = ATTN_WIDTH // HEAD_DIM
DILATED_PAIRS = ((128, 1), (512, 4), (2048, 16))
SSM_WIDTH = D_MODEL - ATTN_WIDTH
SSM_GROUP_CH = 16
SSM_GROUPS = SSM_WIDTH // SSM_GROUP_CH
SSM_STATE = 64
IN_WIDTH = 3 * ATTN_WIDTH + SSM_WIDTH
MOE_GROUPS = 4
EXPERTS_PER_GROUP = 4
N_EXPERTS = MOE_GROUPS * EXPERTS_PER_GROUP
EXPERT_FF = D_MODEL // 4
TOP_K_INNER = 2
RMS_EPS = 1e-6
DT_MIN = 1e-3
DT_MAX = 1e-1

kernel_name = "hymba_s5_dilated_hmoe_trunk"


def rmsnorm(x, g):
    xf = x.astype(jnp.float32)
    y = xf * lax.rsqrt(jnp.mean(xf * xf, axis=-1, keepdims=True) + RMS_EPS)
    return (y * g.astype(jnp.float32)).astype(x.dtype)


def dilated_branch(q, k, v, dil, span):
    B, S, H, Dh = q.shape
    M = S // dil
    nb = -(-M // span)
    Mp = nb * span

    def blocks(a):
        a = a.reshape(B, M, dil, H, Dh)
        a = jnp.pad(a, ((0, 0), (0, Mp - M), (0, 0), (0, 0), (0, 0)))
        return a.reshape(B, nb, span, dil, H, Dh)

    def with_prev(a):
        prev = jnp.pad(a, ((0, 0), (1, 0), (0, 0), (0, 0), (0, 0), (0, 0)))[:, :-1]
        return jnp.concatenate([prev, a], axis=2)

    qb = blocks(q)
    kk = with_prev(blocks(k))
    vv = with_prev(blocks(v))
    s = jnp.einsum('bnqrhc,bnkrhc->bnrhqk', qb.astype(jnp.float32), kk.astype(jnp.float32))
    s = s * (1.0 / math.sqrt(Dh))
    qpos = jnp.arange(span)[:, None]
    kpos = jnp.arange(2 * span)[None, :]
    dist = span + qpos - kpos
    band = (dist >= 0) & (dist <= span)
    not_first = (jnp.arange(nb) > 0)[:, None, None] | (kpos >= span)[None]
    valid = band[None] & not_first
    s = jnp.where(valid[None, :, None, None], s, -jnp.inf)
    mx = jnp.max(s, axis=-1, keepdims=True)
    p = jnp.exp(s - mx)
    den = jnp.sum(p, axis=-1)
    o = jnp.einsum('bnrhqk,bnkrhc->bnqrhc', p, vv.astype(jnp.float32))
    o = o / jnp.transpose(den, (0, 1, 4, 2, 3))[..., None]
    lse = jnp.transpose(mx[..., 0] + jnp.log(den), (0, 1, 4, 2, 3))
    o = o.reshape(B, Mp, dil, H, Dh)[:, :M].reshape(B, S, H, Dh)
    lse = lse.reshape(B, Mp, dil, H)[:, :M].reshape(B, S, H)
    return o, lse


def dilated_attention(q, k, v):
    B, S, H, Dh = q.shape
    outs, lses = [], []
    for window, dil in DILATED_PAIRS:
        o, lse = dilated_branch(q, k, v, dil, window // dil)
        outs.append(o)
        lses.append(lse)
    w = jax.nn.softmax(jnp.stack(lses, axis=0), axis=0)
    o = jnp.sum(w[..., None] * jnp.stack(outs, axis=0), axis=0)
    return o.reshape(B, S, H * Dh).astype(q.dtype)


def s5_mixer(u, lam_re, lam_im, log_dt, b_re, b_im, c_re, c_im, d_skip, w_glu):
    B, S, _ = u.shape
    f32 = jnp.float32
    uf = u.astype(f32).reshape(B, S, SSM_GROUPS, SSM_GROUP_CH)
    lam = lax.complex(lam_re.astype(f32), lam_im.astype(f32))
    dt = jnp.exp(log_dt.astype(f32))[:, None]
    lam_bar = jnp.exp(lam * dt)
    b_t = lax.complex(b_re.astype(f32), b_im.astype(f32))
    b_bar = ((lam_bar - 1.0) / lam)[..., None] * b_t
    c_t = lax.complex(c_re.astype(f32), c_im.astype(f32))
    bu = jnp.einsum('gph,bsgh->sbgp', b_bar, uf.astype(jnp.complex64))
    a = jnp.broadcast_to(lam_bar[None, None], (S, 1) + lam_bar.shape)

    def combine(left, right):
        a_l, b_l = left
        a_r, b_r = right
        return a_r * a_l, a_r * b_l + b_r

    _, states = lax.associative_scan(combine, (a, bu), axis=0)
    y = jnp.einsum('ghp,sbgp->bsgh', c_t, states).real + d_skip.astype(f32) * uf
    y = jax.nn.gelu(y.reshape(B, S, SSM_WIDTH))
    y = y * jax.nn.sigmoid(y @ w_glu.astype(f32))
    return y.astype(u.dtype)


def hier_moe(h, w_rg, b_rg, w_re, b_re_, w_gate, w_up, w_down):
    B, S, D = h.shape
    t = h.reshape(-1, D)
    grp_prob = jax.nn.softmax((t @ w_rg + b_rg).astype(jnp.float32), axis=-1)
    grp = jnp.argmax(grp_prob, axis=-1)
    g1 = jnp.take_along_axis(grp_prob, grp[:, None], axis=-1)
    exp_logits = (jnp.einsum('td,gde->tge', t, w_re) + b_re_).astype(jnp.float32)
    sel = jnp.take_along_axis(exp_logits, grp[:, None, None], axis=1)[:, 0]
    top_v, top_i = lax.top_k(sel, TOP_K_INNER)
    w2 = jax.nn.softmax(top_v, axis=-1) * g1
    expert_idx = grp[:, None] * EXPERTS_PER_GROUP + top_i
    gates = jnp.sum(jax.nn.one_hot(expert_idx, N_EXPERTS, dtype=jnp.float32) * w2[..., None], axis=1)
    hg = jnp.einsum('td,edf->tef', t, w_gate)
    hu = jnp.einsum('td,edf->tef', t, w_up)
    act = jax.nn.silu(hg) * hu * gates.astype(t.dtype)[..., None]
    out = jnp.einsum('tef,efd->td', act, w_down)
    return out.reshape(B, S, D)


def setup_inputs(seed: int = 0) -> dict:
    key = jax.random.key(seed)
    ks = jax.random.split(key, 24)
    nrm = jax.random.normal
    L, D = DEPTH, D_MODEL
    G, P, H = SSM_GROUPS, SSM_STATE, SSM_GROUP_CH
    return {
        "x": nrm(ks[0], (BATCH, SEQ, D), jnp.float32),
        "ln1_g": 1.0 + 0.02 * nrm(ks[1], (L, D), jnp.float32),
        "w_in": nrm(ks[2], (L, D, IN_WIDTH), jnp.float32) * D ** -0.5,
        "lam_re": -0.5 + 0.01 * nrm(ks[3], (L, G, P), jnp.float32),
        "lam_im": jnp.pi * jnp.arange(P, dtype=jnp.float32) + 0.01 * nrm(ks[4], (L, G, P), jnp.float32),
        "log_dt": jax.random.uniform(ks[5], (L, G), jnp.float32, math.log(DT_MIN), math.log(DT_MAX)),
        "b_re": nrm(ks[6], (L, G, P, H), jnp.float32) * (2 * H) ** -0.5,
        "b_im": nrm(ks[7], (L, G, P, H), jnp.float32) * (2 * H) ** -0.5,
        "c_re": nrm(ks[8], (L, G, H, P), jnp.float32) * (2 * P) ** -0.5,
        "c_im": nrm(ks[9], (L, G, H, P), jnp.float32) * (2 * P) ** -0.5,
        "d_skip": nrm(ks[10], (L, G, H), jnp.float32),
        "w_glu": nrm(ks[11], (L, SSM_WIDTH, SSM_WIDTH), jnp.float32) * SSM_WIDTH ** -0.5,
        "gn_attn": 1.0 + 0.02 * nrm(ks[12], (L, ATTN_WIDTH), jnp.float32),
        "gn_ssm": 1.0 + 0.02 * nrm(ks[13], (L, SSM_WIDTH), jnp.float32),
        "w_out": nrm(ks[14], (L, ATTN_WIDTH + SSM_WIDTH, D), jnp.float32) * (ATTN_WIDTH + SSM_WIDTH) ** -0.5,
        "ln2_g": 1.0 + 0.02 * nrm(ks[15], (L, D), jnp.float32),
        "w_router_grp": nrm(ks[16], (L, D, MOE_GROUPS), jnp.float32) * D ** -0.5,
        "b_router_grp": 0.01 * nrm(ks[17], (L, MOE_GROUPS), jnp.float32),
        "w_router_exp": nrm(ks[18], (L, MOE_GROUPS, D, EXPERTS_PER_GROUP), jnp.float32) * D ** -0.5,
        "b_router_exp": 0.01 * nrm(ks[19], (L, MOE_GROUPS, EXPERTS_PER_GROUP), jnp.float32),
        "w_gate": nrm(ks[20], (L, N_EXPERTS, D, EXPERT_FF), jnp.float32) * D ** -0.5,
        "w_up": nrm(ks[21], (L, N_EXPERTS, D, EXPERT_FF), jnp.float32) * D ** -0.5,
        "w_down": nrm(ks[22], (L, N_EXPERTS, EXPERT_FF, D), jnp.float32) * EXPERT_FF ** -0.5,
        "final_g": 1.0 + 0.02 * nrm(ks[23], (D,), jnp.float32),
    }


def reference(x, ln1_g, w_in, lam_re, lam_im, log_dt, b_re, b_im, c_re, c_im, d_skip, w_glu,
              gn_attn, gn_ssm, w_out, ln2_g, w_router_grp, b_router_grp, w_router_exp,
              b_router_exp, w_gate, w_up, w_down, final_g):
    B, S, D = x.shape
    for l in range(DEPTH):
        h = rmsnorm(x, ln1_g[l])
        proj = h @ w_in[l]
        q = proj[..., :ATTN_WIDTH].reshape(B, S, ATTN_HEADS, HEAD_DIM)
        k = proj[..., ATTN_WIDTH:2 * ATTN_WIDTH].reshape(B, S, ATTN_HEADS, HEAD_DIM)
        v = proj[..., 2 * ATTN_WIDTH:3 * ATTN_WIDTH].reshape(B, S, ATTN_HEADS, HEAD_DIM)
        u = proj[..., 3 * ATTN_WIDTH:]
        attn_out = dilated_attention(q, k, v)
        ssm_out = s5_mixer(u, lam_re[l], lam_im[l], log_dt[l], b_re[l], b_im[l],
                           c_re[l], c_im[l], d_skip[l], w_glu[l])
        mixed = jnp.concatenate([rmsnorm(attn_out, gn_attn[l]), rmsnorm(ssm_out, gn_ssm[l])], axis=-1)
        x = x + mixed @ w_out[l]
        h = rmsnorm(x, ln2_g[l])
        x = x + hier_moe(h, w_router_grp[l], b_router_grp[l], w_router_exp[l], b_router_exp[l],
                         w_gate[l], w_up[l], w_down[l])
    return rmsnorm(x, final_g)
```

```python
import functools
import math

import jax
import jax.numpy as jnp
from jax import lax
from jax.experimental import pallas as pl
from jax.experimental.pallas import tpu as pltpu

D_MODEL = 1024
BATCH = 8
SEQ = 2048
DEPTH = 4
ATTN_WIDTH = 512
HEAD_DIM = 64
ATTN_HEADS = 8
DILATED_PAIRS = ((128, 1), (512, 4), (2048, 16))
SPAN = 128
SSM_WIDTH = 512
SSM_GROUP_CH = 16
SSM_GROUPS = 32
SSM_STATE = 64
MOE_GROUPS = 4
EXPERTS_PER_GROUP = 4
N_EXPERTS = 16
EXPERT_FF = 256
RMS_EPS = 1e-6

LANES = 128
SUBLANES = 8
VMEM_LIMIT_BYTES = 56 * 1024 * 1024

TOK_TILE = 512
S5_CHUNK = 128
S5_GROUPS_PER_CHUNK = 8
S5_NCHUNK = SSM_GROUPS // S5_GROUPS_PER_CHUNK
S5_STATE_LANES = S5_GROUPS_PER_CHUNK * SSM_STATE

F32 = jnp.float32
BF16 = jnp.bfloat16


def _params(*sem):
    return pltpu.CompilerParams(dimension_semantics=sem, vmem_limit_bytes=VMEM_LIMIT_BYTES)


def _rms(x, g):
    return x * lax.rsqrt(jnp.mean(x * x, axis=-1, keepdims=True) + RMS_EPS) * g


def _split_bf16(a):
    hi = a.astype(BF16)
    lo = (a - hi.astype(F32)).astype(BF16)
    return hi, lo


def _inproj_kernel(x_ref, g_ref, w_ref, q_ref, k_ref, v_ref, u_ref):
    h = _rms(x_ref[...], g_ref[...]).astype(BF16)
    p = jnp.dot(h, w_ref[...], preferred_element_type=F32)
    aw = ATTN_WIDTH
    q_ref[...] = (p[:, :aw] * (1.0 / math.sqrt(HEAD_DIM))).astype(BF16)
    k_ref[...] = p[:, aw:2 * aw].astype(BF16)
    v_ref[...] = p[:, 2 * aw:3 * aw].astype(BF16)
    u_ref[...] = p[:, 3 * aw:]


def _tok_spec(width):
    return pl.BlockSpec((TOK_TILE, width), lambda b, si: (si, b))


def _const_spec(shape):
    return pl.BlockSpec(shape, lambda b, si: (0,) * len(shape))


def _inproj(x, x_spec, g, w):
    aw = ATTN_WIDTH
    out_bf = jax.ShapeDtypeStruct((SEQ, BATCH * aw), BF16)
    return pl.pallas_call(
        _inproj_kernel,
        grid=(BATCH, SEQ // TOK_TILE),
        in_specs=[x_spec, _const_spec((1, D_MODEL)), _const_spec((D_MODEL, 3 * aw + SSM_WIDTH))],
        out_specs=[_tok_spec(aw), _tok_spec(aw), _tok_spec(aw), _tok_spec(SSM_WIDTH)],
        out_shape=[out_bf, out_bf, out_bf, jax.ShapeDtypeStruct((SEQ, BATCH * SSM_WIDTH), F32)],
        compiler_params=_params("parallel", "parallel"),
        name="inproj",
    )(x, g, w)


def _attn_kernel(q_ref, k_ref, v_ref, o_ref, lse_ref, *, nb):
    lane = lax.broadcasted_iota(jnp.int32, (SPAN, LANES), 1)
    low_half = lane < HEAD_DIM

    def one_block(q, kk, vv, valid):
        lse_tile = jnp.zeros((SPAN, LANES), F32)
        outs = []
        for pair in range(ATTN_HEADS // 2):
            sl = slice(pair * LANES, (pair + 1) * LANES)
            qp, kp, vp = q[:, sl], kk[:, sl], vv[:, sl]
            halves = []
            for half in range(2):
                keep = low_half if half == 0 else jnp.logical_not(low_half)
                qm = jnp.where(keep, qp, jnp.zeros_like(qp))
                s = lax.dot_general(qm, kp, (((1,), (1,)), ((), ())), preferred_element_type=F32)
                s = jnp.where(valid, s, -jnp.inf)
                m = jnp.max(s, axis=-1, keepdims=True)
                p = jnp.exp(s - m)
                den = jnp.sum(p, axis=-1, keepdims=True)
                pv = jnp.dot(p.astype(BF16), vp, preferred_element_type=F32)
                halves.append(pv / den)
                lse_tile = jnp.where(lane == 2 * pair + half, m + jnp.log(den), lse_tile)
            outs.append(jnp.where(low_half, halves[0], halves[1]))
        return jnp.concatenate(outs, axis=-1), lse_tile

    row0 = lax.broadcasted_iota(jnp.int32, (SPAN, SPAN), 0)
    col0 = lax.broadcasted_iota(jnp.int32, (SPAN, SPAN), 1)
    o, lse = one_block(q_ref[0:SPAN, :], k_ref[0:SPAN, :], v_ref[0:SPAN, :], col0 <= row0)
    o_ref[0:SPAN, :] = o
    lse_ref[0:SPAN, :] = lse

    if nb > 1:
        row = lax.broadcasted_iota(jnp.int32, (SPAN, 2 * SPAN), 0)
        col = lax.broadcasted_iota(jnp.int32, (SPAN, 2 * SPAN), 1)
        band = (col >= row) & (col <= row + SPAN)

        def body(n, carry):
            qs = pl.ds(pl.multiple_of(n * SPAN, SPAN), SPAN)
            ks = pl.ds(pl.multiple_of((n - 1) * SPAN, SPAN), 2 * SPAN)
            o, lse = one_block(q_ref[qs, :], k_ref[ks, :], v_ref[ks, :], band)
            o_ref[qs, :] = o
            lse_ref[qs, :] = lse
            return carry

        lax.fori_loop(1, nb, body, 0)


def _attn_branch(q, k, v, dil):
    m_rows = SEQ // dil
    ncol = dil * BATCH
    aw = ATTN_WIDTH

    def view(a, width):
        return a.reshape(m_rows, ncol * width)

    spec = pl.BlockSpec((m_rows, aw), lambda c: (0, c))
    o, lse = pl.pallas_call(
        functools.partial(_attn_kernel, nb=m_rows // SPAN),
        grid=(ncol,),
        in_specs=[spec, spec, spec],
        out_specs=[spec, pl.BlockSpec((m_rows, LANES), lambda c: (0, c))],
        out_shape=[jax.ShapeDtypeStruct((m_rows, ncol * aw), F32),
                   jax.ShapeDtypeStruct((m_rows, ncol * LANES), F32)],
        compiler_params=_params("parallel"),
        name=f"attn_d{dil}",
    )(view(q, aw), view(k, aw), view(v, aw))
    return o.reshape(SEQ, BATCH * aw), lse.reshape(SEQ, BATCH * LANES)


def _s5_kernel(u_ref, bre_ref, bim_ref, lre_ref, lim_ref, cre_ref, cim_ref, d_ref, wglu_ref,
               o_ref, st_re, st_im, xr_buf, xi_buf, y_buf):
    rows = S5_CHUNK * BATCH

    @pl.when(pl.program_id(0) == 0)
    def _():
        st_re[...] = jnp.zeros_like(st_re)
        st_im[...] = jnp.zeros_like(st_im)

    u = u_ref[...].reshape(rows, SSM_WIDTH)
    ub = u.astype(BF16)
    for c in range(S5_NCHUNK):
        ch = slice(c * LANES, (c + 1) * LANES)
        stl = slice(c * S5_STATE_LANES, (c + 1) * S5_STATE_LANES)
        xr_buf[...] = jnp.dot(ub[:, ch], bre_ref[c], preferred_element_type=F32)
        xi_buf[...] = jnp.dot(ub[:, ch], bim_ref[c], preferred_element_type=F32)
        lr = jnp.broadcast_to(lre_ref[:, stl], (BATCH, S5_STATE_LANES))
        li = jnp.broadcast_to(lim_ref[:, stl], (BATCH, S5_STATE_LANES))

        def step(s, carry):
            xr, xi = carry
            sl = pl.ds(pl.multiple_of(s * BATCH, BATCH), BATCH)
            nr = lr * xr - li * xi + xr_buf[sl, :]
            ni = lr * xi + li * xr + xi_buf[sl, :]
            xr_buf[sl, :] = nr
            xi_buf[sl, :] = ni
            return nr, ni

        xr, xi = lax.fori_loop(0, S5_CHUNK, step, (st_re[:, stl], st_im[:, stl]), unroll=4)
        st_re[:, stl] = xr
        st_im[:, stl] = xi
        yc = (jnp.dot(xr_buf[...].astype(BF16), cre_ref[c], preferred_element_type=F32)
              - jnp.dot(xi_buf[...].astype(BF16), cim_ref[c], preferred_element_type=F32))
        y_buf[:, ch] = yc + d_ref[:, ch] * u[:, ch]

    y = jax.nn.gelu(y_buf[...])
    z = jnp.dot(y.astype(BF16), wglu_ref[...], preferred_element_type=F32)
    o_ref[...] = (y * jax.nn.sigmoid(z)).reshape(S5_CHUNK, BATCH, SSM_WIDTH)


def _s5_discretize(lam_re, lam_im, log_dt, b_re, b_im, c_re, c_im):
    dt = jnp.exp(log_dt)[:, None]
    mag = jnp.exp(lam_re * dt)
    lb_re, lb_im = mag * jnp.cos(lam_im * dt), mag * jnp.sin(lam_im * dt)
    den = lam_re * lam_re + lam_im * lam_im
    nr, ni = lb_re - 1.0, lb_im
    f_re = (nr * lam_re + ni * lam_im) / den
    f_im = (ni * lam_re - nr * lam_im) / den
    bb_re = f_re[..., None] * b_re - f_im[..., None] * b_im
    bb_im = f_re[..., None] * b_im + f_im[..., None] * b_re
    eye = jnp.eye(S5_GROUPS_PER_CHUNK, dtype=F32)
    gpc, nch = S5_GROUPS_PER_CHUNK, S5_NCHUNK

    def b_blockdiag(b):
        b = b.reshape(nch, gpc, SSM_STATE, SSM_GROUP_CH)
        m = jnp.einsum('cgph,gk->cghkp', b, eye)
        return m.reshape(nch, gpc * SSM_GROUP_CH, gpc * SSM_STATE).astype(BF16)

    def c_blockdiag(c):
        c = c.reshape(nch, gpc, SSM_GROUP_CH, SSM_STATE)
        m = jnp.einsum('cghp,gk->cgpkh', c, eye)
        return m.reshape(nch, gpc * SSM_STATE, gpc * SSM_GROUP_CH).astype(BF16)

    return (b_blockdiag(bb_re), b_blockdiag(bb_im),
            lb_re.reshape(1, SSM_GROUPS * SSM_STATE), lb_im.reshape(1, SSM_GROUPS * SSM_STATE),
            c_blockdiag(c_re), c_blockdiag(c_im))


def _s5(u, disc, d_skip, w_glu):
    bre, bim, lre, lim, cre, cim = disc
    rows = S5_CHUNK * BATCH
    blk = pl.BlockSpec((S5_CHUNK, BATCH, SSM_WIDTH), lambda i: (i, 0, 0))

    def full(a):
        return pl.BlockSpec(a.shape, lambda i: (0,) * a.ndim)

    args = (bre, bim, lre, lim, cre, cim, d_skip, w_glu)
    return pl.pallas_call(
        _s5_kernel,
        grid=(SEQ // S5_CHUNK,),
        in_specs=[blk] + [full(a) for a in args],
        out_specs=blk,
        out_shape=jax.ShapeDtypeStruct((SEQ, BATCH, SSM_WIDTH), F32),
        scratch_shapes=[pltpu.VMEM((BATCH, SSM_GROUPS * SSM_STATE), F32),
                        pltpu.VMEM((BATCH, SSM_GROUPS * SSM_STATE), F32),
                        pltpu.VMEM((rows, S5_STATE_LANES), F32),
                        pltpu.VMEM((rows, S5_STATE_LANES), F32),
                        pltpu.VMEM((rows, SSM_WIDTH), F32)],
        compiler_params=_params("arbitrary"),
        name="s5",
    )(u, *args)


def _outproj_kernel(o1_ref, o4_ref, o16_ref, l1_ref, l4_ref, l16_ref, ssm_ref, x_ref,
                    ga_ref, gs_ref, w_ref, e_ref, xo_ref):
    l1, l4, l16 = l1_ref[...], l4_ref[...], l16_ref[...]
    m = jnp.maximum(jnp.maximum(l1, l4), l16)
    e1, e4, e16 = jnp.exp(l1 - m), jnp.exp(l4 - m), jnp.exp(l16 - m)
    den = e1 + e4 + e16

    def expand(w):
        hi, lo = _split_bf16(w)
        return (jnp.dot(hi, e_ref[...], preferred_element_type=F32)
                + jnp.dot(lo, e_ref[...], preferred_element_type=F32))

    attn = (expand(e1 / den) * o1_ref[...] + expand(e4 / den) * o4_ref[...]
            + expand(e16 / den) * o16_ref[...])
    a_n = _rms(attn, ga_ref[...]).astype(BF16)
    s_n = _rms(ssm_ref[...], gs_ref[...]).astype(BF16)
    y = (jnp.dot(a_n, w_ref[0:ATTN_WIDTH, :], preferred_element_type=F32)
         + jnp.dot(s_n, w_ref[ATTN_WIDTH:, :], preferred_element_type=F32))
    xo_ref[...] = x_ref[...] + y


def _outproj(o1, o4, o16, l1, l4, l16, ssm, x, x_spec, ga, gs, w, expand_mat):
    aw = ATTN_WIDTH
    return pl.pallas_call(
        _outproj_kernel,
        grid=(BATCH, SEQ // TOK_TILE),
        in_specs=[_tok_spec(aw)] * 3 + [_tok_spec(LANES)] * 3 + [_tok_spec(SSM_WIDTH), x_spec,
                  _const_spec((1, aw)), _const_spec((1, SSM_WIDTH)),
                  _const_spec((aw + SSM_WIDTH, D_MODEL)), _const_spec((LANES, aw))],
        out_specs=_tok_spec(D_MODEL),
        out_shape=jax.ShapeDtypeStruct((SEQ, BATCH * D_MODEL), F32),
        compiler_params=_params("parallel", "parallel"),
        name="outproj",
    )(o1, o4, o16, l1, l4, l16, ssm, x, ga, gs, w, expand_mat)


ROUTER_ROWS = 32
EXPERT_ROW0 = 8


def _router_kernel(x_ref, g_ref, whi_ref, wlo_ref, b_ref, gates_ref):
    h = _rms(x_ref[...], g_ref[...])
    h_hi, h_lo = _split_bf16(h)
    nt = (((1,), (1,)), ((), ()))
    logits = (lax.dot_general(whi_ref[...], h_hi, nt, preferred_element_type=F32)
              + lax.dot_general(wlo_ref[...], h_hi, nt, preferred_element_type=F32)
              + lax.dot_general(whi_ref[...], h_lo, nt, preferred_element_type=F32)
              + b_ref[...])
    ng, ne = MOE_GROUPS, EXPERTS_PER_GROUP
    gl = [logits[g:g + 1, :] for g in range(ng)]
    best, grp = gl[0], jnp.zeros_like(gl[0], dtype=jnp.int32)
    for g in range(1, ng):
        better = gl[g] > best
        grp = jnp.where(better, g, grp)
        best = jnp.where(better, gl[g], best)
    g1 = 1.0 / sum(jnp.exp(x - best) for x in gl)
    sel = []
    for e in range(ne):
        acc = jnp.zeros_like(best)
        for g in range(ng):
            r = EXPERT_ROW0 + g * ne + e
            acc = jnp.where(grp == g, logits[r:r + 1, :], acc)
        sel.append(acc)

    def first_argmax(vals):
        bv, bi = vals[0], jnp.zeros_like(grp)
        for e in range(1, ne):
            better = vals[e] > bv
            bi = jnp.where(better, e, bi)
            bv = jnp.where(better, vals[e], bv)
        return bv, bi

    v1, i1 = first_argmax(sel)
    v2, i2 = first_argmax([jnp.where(i1 == e, -jnp.inf, sel[e]) for e in range(ne)])
    e2 = jnp.exp(v2 - v1)
    w1 = g1 / (1.0 + e2)
    w2 = g1 * e2 / (1.0 + e2)
    tokens = logits.shape[1]
    rowid = lax.broadcasted_iota(jnp.int32, (LANES, tokens), 0)
    table = jnp.zeros((LANES, tokens), F32)
    for g in range(ng):
        for e in range(ne):
            val = jnp.where(grp == g, jnp.where(i1 == e, w1, jnp.where(i2 == e, w2, 0.0)), 0.0)
            table = jnp.where(rowid == g * ne + e, val, table)
    gates_ref[...] = table.T


def _router(x, g, whi, wlo, bias):
    return pl.pallas_call(
        _router_kernel,
        grid=(BATCH, SEQ // TOK_TILE),
        in_specs=[_tok_spec(D_MODEL), _const_spec((1, D_MODEL)),
                  _const_spec((ROUTER_ROWS, D_MODEL)), _const_spec((ROUTER_ROWS, D_MODEL)),
                  _const_spec((ROUTER_ROWS, 1))],
        out_specs=_tok_spec(LANES),
        out_shape=jax.ShapeDtypeStruct((SEQ, BATCH * LANES), F32),
        compiler_params=_params("parallel", "parallel"),
        name="router",
    )(x, g, whi, wlo, bias)


def _moe_kernel(x_ref, g_ref, gates_ref, e_ref, wg_ref, wu_ref, wd_ref, xo_ref, h_buf):
    grp = pl.program_id(2)

    @pl.when(grp == 0)
    def _():
        x = x_ref[...]
        h_buf[...] = _rms(x, g_ref[...]).astype(BF16)
        xo_ref[...] = x

    h = h_buf[...]
    g_hi, g_lo = _split_bf16(gates_ref[...])
    gate_wide = (jnp.dot(g_hi, e_ref[...], preferred_element_type=F32)
                 + jnp.dot(g_lo, e_ref[...], preferred_element_type=F32))
    acc = jnp.zeros(xo_ref.shape, F32)
    for e in range(EXPERTS_PER_GROUP):
        hg = jnp.dot(h, wg_ref[e], preferred_element_type=F32)
        hu = jnp.dot(h, wu_ref[e], preferred_element_type=F32)
        act = jax.nn.silu(hg) * hu * gate_wide[:, e * EXPERT_FF:(e + 1) * EXPERT_FF]
        acc = acc + jnp.dot(act.astype(BF16), wd_ref[e], preferred_element_type=F32)
    xo_ref[...] += acc


def _moe(x, g, gates, gate_expand, wg, wu, wd):
    npg = EXPERTS_PER_GROUP
    tok = lambda width: pl.BlockSpec((TOK_TILE, width), lambda b, si, gi: (si, b))
    const = lambda shape: pl.BlockSpec(shape, lambda b, si, gi: (0,) * len(shape))
    per_group = lambda *shape: pl.BlockSpec((None,) + shape, lambda b, si, gi: (gi,) + (0,) * len(shape))
    return pl.pallas_call(
        _moe_kernel,
        grid=(BATCH, SEQ // TOK_TILE, MOE_GROUPS),
        in_specs=[tok(D_MODEL), const((1, D_MODEL)), tok(LANES),
                  per_group(LANES, npg * EXPERT_FF),
                  per_group(npg, D_MODEL, EXPERT_FF), per_group(npg, D_MODEL, EXPERT_FF),
                  per_group(npg, EXPERT_FF, D_MODEL)],
        out_specs=tok(D_MODEL),
        out_shape=jax.ShapeDtypeStruct((SEQ, BATCH * D_MODEL), F32),
        scratch_shapes=[pltpu.VMEM((TOK_TILE, D_MODEL), BF16)],
        compiler_params=_params("parallel", "parallel", "arbitrary"),
        name="moe",
    )(x, g, gates, gate_expand, wg, wu, wd)


def _final_kernel(x_ref, g_ref, o_ref):
    o_ref[...] = _rms(x_ref[...], g_ref[...])


def _final(x, g):
    return pl.pallas_call(
        _final_kernel,
        grid=(BATCH, SEQ // TOK_TILE),
        in_specs=[_tok_spec(D_MODEL), _const_spec((1, D_MODEL))],
        out_specs=pl.BlockSpec((None, TOK_TILE, D_MODEL), lambda b, si: (b, si, 0)),
        out_shape=jax.ShapeDtypeStruct((BATCH, SEQ, D_MODEL), F32),
        compiler_params=_params("parallel", "parallel"),
        name="final_norm",
    )(x, g)


def _head_expand_matrix():
    r = jnp.arange(LANES)[:, None]
    c = jnp.arange(ATTN_WIDTH)[None, :] // HEAD_DIM
    return (r == c).astype(BF16)


def _gate_expand_matrices():
    g = jnp.arange(MOE_GROUPS)[:, None, None]
    r = jnp.arange(LANES)[None, :, None]
    c = jnp.arange(EXPERTS_PER_GROUP * EXPERT_FF)[None, None, :] // EXPERT_FF
    return (r == g * EXPERTS_PER_GROUP + c).astype(BF16)


def kernel(x, ln1_g, w_in, lam_re, lam_im, log_dt, b_re, b_im, c_re, c_im, d_skip, w_glu,
           gn_attn, gn_ssm, w_out, ln2_g, w_router_grp, b_router_grp, w_router_exp,
           b_router_exp, w_gate, w_up, w_down, final_g):
    assert x.shape == (BATCH, SEQ, D_MODEL)
    head_expand = _head_expand_matrix()
    gate_expand = _gate_expand_matrices()
    natural_spec = pl.BlockSpec((None, TOK_TILE, D_MODEL), lambda b, si: (b, si, 0))
    ng, npg = MOE_GROUPS, EXPERTS_PER_GROUP

    for l in range(DEPTH):
        x_spec = natural_spec if l == 0 else _tok_spec(D_MODEL)
        q, k, v, u = _inproj(x, x_spec, ln1_g[l][None, :], w_in[l].astype(BF16))
        branches = [_attn_branch(q, k, v, dil) for _, dil in DILATED_PAIRS]
        disc = _s5_discretize(lam_re[l], lam_im[l], log_dt[l], b_re[l], b_im[l], c_re[l], c_im[l])
        ssm = _s5(u.reshape(SEQ, BATCH, SSM_WIDTH), disc, d_skip[l].reshape(1, SSM_WIDTH),
                  w_glu[l].astype(BF16)).reshape(SEQ, BATCH * SSM_WIDTH)
        (o1, l1), (o4, l4), (o16, l16) = branches
        x = _outproj(o1, o4, o16, l1, l4, l16, ssm, x, x_spec, gn_attn[l][None, :],
                     gn_ssm[l][None, :], w_out[l].astype(BF16), head_expand)

        w_r = jnp.zeros((ROUTER_ROWS, D_MODEL), F32)
        w_r = w_r.at[0:ng].set(w_router_grp[l].T)
        w_r = w_r.at[EXPERT_ROW0:EXPERT_ROW0 + N_EXPERTS].set(
            jnp.transpose(w_router_exp[l], (0, 2, 1)).reshape(N_EXPERTS, D_MODEL))
        b_r = jnp.zeros((ROUTER_ROWS, 1), F32)
        b_r = b_r.at[0:ng, 0].set(b_router_grp[l])
        b_r = b_r.at[EXPERT_ROW0:EXPERT_ROW0 + N_EXPERTS, 0].set(b_router_exp[l].reshape(N_EXPERTS))
        w_r_hi, w_r_lo = _split_bf16(w_r)
        gates = _router(x, ln2_g[l][None, :], w_r_hi, w_r_lo, b_r)
        x = _moe(x, ln2_g[l][None, :], gates, gate_expand,
                 w_gate[l].astype(BF16).reshape(ng, npg, D_MODEL, EXPERT_FF),
                 w_up[l].astype(BF16).reshape(ng, npg, D_MODEL, EXPERT_FF),
                 w_down[l].astype(BF16).reshape(ng, npg, EXPERT_FF, D_MODEL))
    return _final(x, final_g[None, :])
```

```python
import functools
import math

import jax
import jax.numpy as jnp
from jax import lax
from jax.experimental import pallas as pl
from jax.experimental.pallas import tpu as pltpu

D_MODEL = 1024
BATCH = 8
SEQ = 2048
DEPTH = 4
ATTN_WIDTH = 512
HEAD_DIM = 64
ATTN_HEADS = 8
DILATIONS = (1, 4, 16)
SPAN = 128
SSM_WIDTH = 512
SSM_GROUP_CH = 16
SSM_GROUPS = 32
SSM_STATE = 64
MOE_GROUPS = 4
EXPERTS_PER_GROUP = 4
N_EXPERTS = 16
EXPERT_FF = 256
RMS_EPS = 1e-6

LANES = 128
BF16_ROWS = 16
VMEM_LIMIT_BYTES = 56 * 1024 * 1024

TOK_TILE = 512
PERM_TILE = 256
S5_CHUNK = 128
S5_GROUPS_PER_CHUNK = 8
S5_NCHUNK = SSM_GROUPS // S5_GROUPS_PER_CHUNK
S5_STATE_LANES = S5_GROUPS_PER_CHUNK * SSM_STATE
S5_SUB = BF16_ROWS

F32 = jnp.float32
BF16 = jnp.bfloat16


def _params(*sem):
    return pltpu.CompilerParams(dimension_semantics=sem, vmem_limit_bytes=VMEM_LIMIT_BYTES)


def _rms(x, g):
    return x * lax.rsqrt(jnp.mean(x * x, axis=-1, keepdims=True) + RMS_EPS) * g


def _split_bf16(a):
    hi = a.astype(BF16)
    lo = (a - hi.astype(F32)).astype(BF16)
    return hi, lo


def _dot(a, b):
    return jnp.dot(a, b, preferred_element_type=F32)


def _dot_hilo(p, a):
    hi, lo = _split_bf16(a)
    return _dot(p, hi) + _dot(p, lo)


def _tok_spec(width):
    return pl.BlockSpec((None, TOK_TILE, width), lambda b, si: (b, si, 0))


def _const_spec(shape):
    return pl.BlockSpec(shape, lambda *_: (0,) * len(shape))


def _inproj_kernel(x_ref, g_ref, w_ref, q_ref, k_ref, v_ref, u_ref):
    h = _rms(x_ref[...], g_ref[...]).astype(BF16)
    p = _dot(h, w_ref[...])
    aw = ATTN_WIDTH
    q_ref[...] = (p[:, :aw] * (1.0 / math.sqrt(HEAD_DIM))).astype(BF16)
    k_ref[...] = p[:, aw:2 * aw].astype(BF16)
    v_ref[...] = p[:, 2 * aw:3 * aw].astype(BF16)
    u_ref[...] = p[:, 3 * aw:].astype(BF16)


def _inproj(x, g, w):
    aw = ATTN_WIDTH
    out = jax.ShapeDtypeStruct((BATCH, SEQ, aw), BF16)
    return pl.pallas_call(
        _inproj_kernel,
        grid=(BATCH, SEQ // TOK_TILE),
        in_specs=[_tok_spec(D_MODEL), _const_spec((1, D_MODEL)),
                  _const_spec((D_MODEL, 3 * aw + SSM_WIDTH))],
        out_specs=[_tok_spec(aw)] * 4,
        out_shape=[out] * 4,
        compiler_params=_params("parallel", "parallel"),
        name="inproj",
    )(x, g, w)


def _attn_block(q, kk, vv, bias, low_half, lane):
    heads = range(ATTN_HEADS)
    nt = (((1,), (1,)), ((), ()))
    scores = []
    for h in heads:
        sl = slice((h // 2) * LANES, (h // 2 + 1) * LANES)
        keep = low_half if h % 2 == 0 else jnp.logical_not(low_half)
        qm = jnp.where(keep, q[:, sl], jnp.zeros_like(q[:, sl]))
        scores.append(lax.dot_general(qm, kk[:, sl], nt, preferred_element_type=F32) + bias)
    probs, dens, lses = [], [], []
    for h in heads:
        m = jnp.max(scores[h], axis=-1, keepdims=True)
        p = jnp.exp(scores[h] - m)
        den = jnp.sum(p, axis=-1, keepdims=True)
        probs.append(p.astype(BF16))
        dens.append(den)
        lses.append(m + jnp.log(den))
    pv = [_dot(probs[h], vv[:, (h // 2) * LANES:(h // 2 + 1) * LANES]) / dens[h] for h in heads]
    lse_tile = jnp.zeros((SPAN, LANES), F32)
    for h in heads:
        lse_tile = jnp.where(lane == h, lses[h], lse_tile)
    outs = [jnp.where(low_half, pv[2 * j], pv[2 * j + 1]) for j in range(ATTN_HEADS // 2)]
    return jnp.concatenate(outs, axis=-1), lse_tile


def _attn_kernel(q_ref, k_ref, v_ref, p4_ref, p4t_ref, p16_ref, p16t_ref, e_ref, o_ref,
                 qkv4, qkv16, lse1, o4p, lse4p, o16p, lse16p):
    lane = lax.broadcasted_iota(jnp.int32, (SPAN, LANES), 1)
    low_half = lane < HEAD_DIM
    row0 = lax.broadcasted_iota(jnp.int32, (SPAN, SPAN), 0)
    col0 = lax.broadcasted_iota(jnp.int32, (SPAN, SPAN), 1)
    bias_first = jnp.where(col0 <= row0, 0.0, -jnp.inf).astype(F32)
    row = lax.broadcasted_iota(jnp.int32, (SPAN, 2 * SPAN), 0)
    col = lax.broadcasted_iota(jnp.int32, (SPAN, 2 * SPAN), 1)
    bias_band = jnp.where((col >= row) & (col <= row + SPAN), 0.0, -jnp.inf).astype(F32)
    block = functools.partial(_attn_block, low_half=low_half, lane=lane)
    srcs = (q_ref, k_ref, v_ref)

    def permute(i, carry):
        rows = pl.ds(pl.multiple_of(i * PERM_TILE, PERM_TILE), PERM_TILE)
        for a, src in enumerate(srcs):
            x = src[rows, :]
            y4 = _dot(p4_ref[...], x).astype(BF16)
            y16 = _dot(p16_ref[...], x).astype(BF16)
            n4, n16 = PERM_TILE // 4, PERM_TILE // 16
            for r in range(4):
                qkv4[a, r, pl.ds(pl.multiple_of(i * n4, n4), n4), :] = y4[r * n4:(r + 1) * n4]
            for r in range(16):
                qkv16[a, r, pl.ds(pl.multiple_of(i * n16, n16), n16), :] = y16[r * n16:(r + 1) * n16]
        return carry

    lax.fori_loop(0, SEQ // PERM_TILE, permute, 0)

    def branch(get, put, nblocks):
        head = pl.ds(0, SPAN)
        o, lse = block(get(0, head), get(1, head), get(2, head), bias_first)
        put(head, o, lse)
        if nblocks > 1:
            def body(n, carry):
                qs = pl.ds(pl.multiple_of(n * SPAN, SPAN), SPAN)
                ks = pl.ds(pl.multiple_of((n - 1) * SPAN, SPAN), 2 * SPAN)
                o, lse = block(get(0, qs), get(1, ks), get(2, ks), bias_band)
                put(qs, o, lse)
                return carry
            lax.fori_loop(1, nblocks, body, 0)

    def put1(rows, o, lse):
        o_ref[rows, :] = o
        lse1[rows, :] = lse

    branch(lambda a, rows: srcs[a][rows, :], put1, SEQ // SPAN)

    def class4(r, carry):
        def put(rows, o, lse):
            o4p[r, rows, :] = o.astype(BF16)
            lse4p[r, rows, :] = lse
        branch(lambda a, rows: qkv4[a, r, rows, :], put, SEQ // 4 // SPAN)
        return carry

    lax.fori_loop(0, 4, class4, 0)

    def class16(r, carry):
        def put(rows, o, lse):
            o16p[r, rows, :] = o.astype(BF16)
            lse16p[r, rows, :] = lse
        branch(lambda a, rows: qkv16[a, r, rows, :], put, SEQ // 16 // SPAN)
        return carry

    lax.fori_loop(0, 16, class16, 0)

    def expand(w):
        hi, lo = _split_bf16(w)
        return _dot(hi, e_ref[...]) + _dot(lo, e_ref[...])

    def combine(i, carry):
        rows = pl.ds(pl.multiple_of(i * PERM_TILE, PERM_TILE), PERM_TILE)
        n4, n16 = PERM_TILE // 4, PERM_TILE // 16
        r4 = pl.ds(pl.multiple_of(i * n4, n4), n4)
        r16 = pl.ds(pl.multiple_of(i * n16, n16), n16)
        o4 = _dot(p4t_ref[...], jnp.concatenate([o4p[r, r4, :] for r in range(4)], axis=0))
        l4 = _dot_hilo(p4t_ref[...], jnp.concatenate([lse4p[r, r4, :] for r in range(4)], axis=0))
        o16 = _dot(p16t_ref[...], jnp.concatenate([o16p[r, r16, :] for r in range(16)], axis=0))
        l16 = _dot_hilo(p16t_ref[...], jnp.concatenate([lse16p[r, r16, :] for r in range(16)], axis=0))
        l1 = lse1[rows, :]
        m = jnp.maximum(jnp.maximum(l1, l4), l16)
        e1, e4, e16 = jnp.exp(l1 - m), jnp.exp(l4 - m), jnp.exp(l16 - m)
        den = e1 + e4 + e16
        o_ref[rows, :] = (expand(e1 / den) * o_ref[rows, :] + expand(e4 / den) * o4
                          + expand(e16 / den) * o16)
        return carry

    lax.fori_loop(0, SEQ // PERM_TILE, combine, 0)


def _perm_matrix(dil):
    n = PERM_TILE // dil
    out_row = jnp.arange(PERM_TILE)
    src = dil * (out_row % n) + out_row // n
    return (src[:, None] == jnp.arange(PERM_TILE)[None, :]).astype(BF16)


def _head_expand_matrix():
    r = jnp.arange(LANES)[:, None]
    c = jnp.arange(ATTN_WIDTH)[None, :] // HEAD_DIM
    return (r == c).astype(BF16)


def _attention(q, k, v):
    aw = ATTN_WIDTH
    seq_spec = pl.BlockSpec((None, SEQ, aw), lambda b: (b, 0, 0))
    p4, p16 = _perm_matrix(4), _perm_matrix(16)
    consts = (p4, p4.T, p16, p16.T, _head_expand_matrix())
    return pl.pallas_call(
        _attn_kernel,
        grid=(BATCH,),
        in_specs=[seq_spec] * 3 + [_const_spec(c.shape) for c in consts],
        out_specs=seq_spec,
        out_shape=jax.ShapeDtypeStruct((BATCH, SEQ, aw), F32),
        scratch_shapes=[pltpu.VMEM((3, 4, SEQ // 4, aw), BF16),
                        pltpu.VMEM((3, 16, SEQ // 16, aw), BF16),
                        pltpu.VMEM((SEQ, LANES), F32),
                        pltpu.VMEM((4, SEQ // 4, aw), BF16),
                        pltpu.VMEM((4, SEQ // 4, LANES), F32),
                        pltpu.VMEM((16, SEQ // 16, aw), BF16),
                        pltpu.VMEM((16, SEQ // 16, LANES), F32)],
        compiler_params=_params("parallel"),
        name="attention",
    )(q, k, v, *consts)


def _s5_kernel(u_ref, pf_ref, pb_ref, bre_ref, bim_ref, lre_ref, lim_ref, cre_ref, cim_ref, d_ref,
               wglu_ref, o_ref, st_re, st_im, u_buf, xr_buf, xi_buf, y_buf):
    rows = S5_CHUNK * BATCH
    sub_rows = S5_SUB * BATCH

    @pl.when(pl.program_id(0) == 0)
    def _():
        st_re[...] = jnp.zeros_like(st_re)
        st_im[...] = jnp.zeros_like(st_im)

    for tb in range(S5_CHUNK // S5_SUB):
        t = slice(tb * S5_SUB, (tb + 1) * S5_SUB)
        piece = jnp.concatenate([u_ref[b, t, :] for b in range(BATCH)], axis=0)
        u_buf[tb * sub_rows:(tb + 1) * sub_rows, :] = _dot(pf_ref[...], piece)

    for c in range(S5_NCHUNK):
        ch = slice(c * LANES, (c + 1) * LANES)
        stl = slice(c * S5_STATE_LANES, (c + 1) * S5_STATE_LANES)
        ub = u_buf[:, ch].astype(BF16)
        xr_buf[...] = _dot(ub, bre_ref[c])
        xi_buf[...] = _dot(ub, bim_ref[c])
        lr = jnp.broadcast_to(lre_ref[:, stl], (BATCH, S5_STATE_LANES))
        li = jnp.broadcast_to(lim_ref[:, stl], (BATCH, S5_STATE_LANES))

        def step(s, carry):
            xr, xi = carry
            sl = pl.ds(pl.multiple_of(s * BATCH, BATCH), BATCH)
            nr = lr * xr - li * xi + xr_buf[sl, :]
            ni = lr * xi + li * xr + xi_buf[sl, :]
            xr_buf[sl, :] = nr
            xi_buf[sl, :] = ni
            return nr, ni

        xr, xi = lax.fori_loop(0, S5_CHUNK, step, (st_re[:, stl], st_im[:, stl]), unroll=4)
        st_re[:, stl] = xr
        st_im[:, stl] = xi
        yc = (_dot(xr_buf[...].astype(BF16), cre_ref[c]) - _dot(xi_buf[...].astype(BF16), cim_ref[c]))
        y_buf[:, ch] = yc + d_ref[:, ch] * u_buf[:, ch]

    y = jax.nn.gelu(y_buf[...])
    z = _dot(y.astype(BF16), wglu_ref[...])
    y_buf[...] = y * jax.nn.sigmoid(z)
    for tb in range(S5_CHUNK // S5_SUB):
        back = _dot(pb_ref[...], y_buf[tb * sub_rows:(tb + 1) * sub_rows, :].astype(BF16)).astype(BF16)
        for b in range(BATCH):
            o_ref[b, tb * S5_SUB:(tb + 1) * S5_SUB, :] = back[b * S5_SUB:(b + 1) * S5_SUB]


def _s5_discretize(lam_re, lam_im, log_dt, b_re, b_im, c_re, c_im):
    dt = jnp.exp(log_dt)[:, None]
    mag = jnp.exp(lam_re * dt)
    lb_re, lb_im = mag * jnp.cos(lam_im * dt), mag * jnp.sin(lam_im * dt)
    den = lam_re * lam_re + lam_im * lam_im
    nr, ni = lb_re - 1.0, lb_im
    f_re = (nr * lam_re + ni * lam_im) / den
    f_im = (ni * lam_re - nr * lam_im) / den
    bb_re = f_re[..., None] * b_re - f_im[..., None] * b_im
    bb_im = f_re[..., None] * b_im + f_im[..., None] * b_re
    eye = jnp.eye(S5_GROUPS_PER_CHUNK, dtype=F32)
    gpc, nch = S5_GROUPS_PER_CHUNK, S5_NCHUNK

    def b_blockdiag(b):
        b = b.reshape(nch, gpc, SSM_STATE, SSM_GROUP_CH)
        m = jnp.einsum('cgph,gk->cghkp', b, eye)
        return m.reshape(nch, gpc * SSM_GROUP_CH, gpc * SSM_STATE).astype(BF16)

    def c_blockdiag(c):
        c = c.reshape(nch, gpc, SSM_GROUP_CH, SSM_STATE)
        m = jnp.einsum('cghp,gk->cgpkh', c, eye)
        return m.reshape(nch, gpc * SSM_STATE, gpc * SSM_GROUP_CH).astype(BF16)

    return (b_blockdiag(bb_re), b_blockdiag(bb_im),
            lb_re.reshape(1, SSM_GROUPS * SSM_STATE), lb_im.reshape(1, SSM_GROUPS * SSM_STATE),
            c_blockdiag(c_re), c_blockdiag(c_im))


def _s5_reorder_matrices():
    n = S5_SUB * BATCH
    out_row = jnp.arange(n)
    src = (out_row % BATCH) * S5_SUB + out_row // BATCH
    fwd = (src[:, None] == jnp.arange(n)[None, :]).astype(BF16)
    return fwd, fwd.T


def _s5(u, disc, d_skip, w_glu):
    rows = S5_CHUNK * BATCH
    blk = pl.BlockSpec((BATCH, S5_CHUNK, SSM_WIDTH), lambda i: (0, i, 0))
    args = _s5_reorder_matrices() + tuple(disc) + (d_skip, w_glu)
    return pl.pallas_call(
        _s5_kernel,
        grid=(SEQ // S5_CHUNK,),
        in_specs=[blk] + [_const_spec(a.shape) for a in args],
        out_specs=blk,
        out_shape=jax.ShapeDtypeStruct((BATCH, SEQ, SSM_WIDTH), BF16),
        scratch_shapes=[pltpu.VMEM((BATCH, SSM_GROUPS * SSM_STATE), F32),
                        pltpu.VMEM((BATCH, SSM_GROUPS * SSM_STATE), F32),
                        pltpu.VMEM((rows, SSM_WIDTH), F32),
                        pltpu.VMEM((rows, S5_STATE_LANES), F32),
                        pltpu.VMEM((rows, S5_STATE_LANES), F32),
                        pltpu.VMEM((rows, SSM_WIDTH), F32)],
        compiler_params=_params("arbitrary"),
        name="s5",
    )(u, *args)


def _outproj_kernel(attn_ref, ssm_ref, x_ref, ga_ref, gs_ref, w_ref, xo_ref):
    a_n = _rms(attn_ref[...], ga_ref[...]).astype(BF16)
    s_n = _rms(ssm_ref[...].astype(F32), gs_ref[...]).astype(BF16)
    y = _dot(a_n, w_ref[0:ATTN_WIDTH, :]) + _dot(s_n, w_ref[ATTN_WIDTH:, :])
    xo_ref[...] = x_ref[...] + y


def _outproj(attn, ssm, x, ga, gs, w):
    aw = ATTN_WIDTH
    return pl.pallas_call(
        _outproj_kernel,
        grid=(BATCH, SEQ // TOK_TILE),
        in_specs=[_tok_spec(aw), _tok_spec(SSM_WIDTH), _tok_spec(D_MODEL),
                  _const_spec((1, aw)), _const_spec((1, SSM_WIDTH)),
                  _const_spec((aw + SSM_WIDTH, D_MODEL))],
        out_specs=_tok_spec(D_MODEL),
        out_shape=jax.ShapeDtypeStruct((BATCH, SEQ, D_MODEL), F32),
        compiler_params=_params("parallel", "parallel"),
        name="outproj",
    )(attn, ssm, x, ga, gs, w)


ROUTER_ROWS = 32
EXPERT_ROW0 = 8


def _router_kernel(x_ref, g_ref, whi_ref, wlo_ref, b_ref, gates_ref):
    h = _rms(x_ref[...], g_ref[...])
    h_hi, h_lo = _split_bf16(h)
    nt = (((1,), (1,)), ((), ()))
    logits = (lax.dot_general(whi_ref[...], h_hi, nt, preferred_element_type=F32)
              + lax.dot_general(wlo_ref[...], h_hi, nt, preferred_element_type=F32)
              + lax.dot_general(whi_ref[...], h_lo, nt, preferred_element_type=F32)
              + b_ref[...])
    ng, ne = MOE_GROUPS, EXPERTS_PER_GROUP
    gl = [logits[g:g + 1, :] for g in range(ng)]
    best, grp = gl[0], jnp.zeros_like(gl[0], dtype=jnp.int32)
    for g in range(1, ng):
        better = gl[g] > best
        grp = jnp.where(better, g, grp)
        best = jnp.where(better, gl[g], best)
    g1 = 1.0 / sum(jnp.exp(x - best) for x in gl)
    sel = []
    for e in range(ne):
        acc = jnp.zeros_like(best)
        for g in range(ng):
            r = EXPERT_ROW0 + g * ne + e
            acc = jnp.where(grp == g, logits[r:r + 1, :], acc)
        sel.append(acc)

    def first_argmax(vals):
        bv, bi = vals[0], jnp.zeros_like(grp)
        for e in range(1, ne):
            better = vals[e] > bv
            bi = jnp.where(better, e, bi)
            bv = jnp.where(better, vals[e], bv)
        return bv, bi

    v1, i1 = first_argmax(sel)
    v2, i2 = first_argmax([jnp.where(i1 == e, -jnp.inf, sel[e]) for e in range(ne)])
    e2 = jnp.exp(v2 - v1)
    w1 = g1 / (1.0 + e2)
    w2 = g1 * e2 / (1.0 + e2)
    tokens = logits.shape[1]
    rowid = lax.broadcasted_iota(jnp.int32, (LANES, tokens), 0)
    table = jnp.zeros((LANES, tokens), F32)
    for g in range(ng):
        for e in range(ne):
            val = jnp.where(grp == g, jnp.where(i1 == e, w1, jnp.where(i2 == e, w2, 0.0)), 0.0)
            table = jnp.where(rowid == g * ne + e, val, table)
    gates_ref[...] = table.T


def _router(x, g, whi, wlo, bias):
    return pl.pallas_call(
        _router_kernel,
        grid=(BATCH, SEQ // TOK_TILE),
        in_specs=[_tok_spec(D_MODEL), _const_spec((1, D_MODEL)),
                  _const_spec((ROUTER_ROWS, D_MODEL)), _const_spec((ROUTER_ROWS, D_MODEL)),
                  _const_spec((ROUTER_ROWS, 1))],
        out_specs=_tok_spec(LANES),
        out_shape=jax.ShapeDtypeStruct((BATCH, SEQ, LANES), F32),
        compiler_params=_params("parallel", "parallel"),
        name="router",
    )(x, g, whi, wlo, bias)


def _moe_kernel(x_ref, g_ref, gates_ref, e_ref, wg_ref, wu_ref, wd_ref, xo_ref, h_buf):
    grp = pl.program_id(2)

    @pl.when(grp == 0)
    def _():
        x = x_ref[...]
        h_buf[...] = _rms(x, g_ref[...]).astype(BF16)
        xo_ref[...] = x

    h = h_buf[...]
    g_hi, g_lo = _split_bf16(gates_ref[...])
    gate_wide = _dot(g_hi, e_ref[...]) + _dot(g_lo, e_ref[...])
    acc = jnp.zeros(xo_ref.shape, F32)
    for e in range(EXPERTS_PER_GROUP):
        hg = _dot(h, wg_ref[e])
        hu = _dot(h, wu_ref[e])
        act = jax.nn.silu(hg) * hu * gate_wide[:, e * EXPERT_FF:(e + 1) * EXPERT_FF]
        acc = acc + _dot(act.astype(BF16), wd_ref[e])
    xo_ref[...] += acc


def _moe(x, g, gates, gate_expand, wg, wu, wd):
    npg = EXPERTS_PER_GROUP
    tok = lambda width: pl.BlockSpec((None, TOK_TILE, width), lambda b, si, gi: (b, si, 0))
    per_group = lambda *shape: pl.BlockSpec((None,) + shape, lambda b, si, gi: (gi,) + (0,) * len(shape))
    return pl.pallas_call(
        _moe_kernel,
        grid=(BATCH, SEQ // TOK_TILE, MOE_GROUPS),
        in_specs=[tok(D_MODEL), _const_spec((1, D_MODEL)), tok(LANES),
                  per_group(LANES, npg * EXPERT_FF),
                  per_group(npg, D_MODEL, EXPERT_FF), per_group(npg, D_MODEL, EXPERT_FF),
                  per_group(npg, EXPERT_FF, D_MODEL)],
        out_specs=tok(D_MODEL),
        out_shape=jax.ShapeDtypeStruct((BATCH, SEQ, D_MODEL), F32),
        scratch_shapes=[pltpu.VMEM((TOK_TILE, D_MODEL), BF16)],
        compiler_params=_params("parallel", "parallel", "arbitrary"),
        name="moe",
    )(x, g, gates, gate_expand, wg, wu, wd)


def _final_kernel(x_ref, g_ref, o_ref):
    o_ref[...] = _rms(x_ref[...], g_ref[...])


def _final(x, g):
    return pl.pallas_call(
        _final_kernel,
        grid=(BATCH, SEQ // TOK_TILE),
        in_specs=[_tok_spec(D_MODEL), _const_spec((1, D_MODEL))],
        out_specs=_tok_spec(D_MODEL),
        out_shape=jax.ShapeDtypeStruct((BATCH, SEQ, D_MODEL), F32),
        compiler_params=_params("parallel", "parallel"),
        name="final_norm",
    )(x, g)


def _gate_expand_matrices():
    g = jnp.arange(MOE_GROUPS)[:, None, None]
    r = jnp.arange(LANES)[None, :, None]
    c = jnp.arange(EXPERTS_PER_GROUP * EXPERT_FF)[None, None, :] // EXPERT_FF
    return (r == g * EXPERTS_PER_GROUP + c).astype(BF16)


def kernel(x, ln1_g, w_in, lam_re, lam_im, log_dt, b_re, b_im, c_re, c_im, d_skip, w_glu,
           gn_attn, gn_ssm, w_out, ln2_g, w_router_grp, b_router_grp, w_router_exp,
           b_router_exp, w_gate, w_up, w_down, final_g):
    assert x.shape == (BATCH, SEQ, D_MODEL)
    gate_expand = _gate_expand_matrices()
    ng, npg = MOE_GROUPS, EXPERTS_PER_GROUP

    for l in range(DEPTH):
        q, k, v, u = _inproj(x, ln1_g[l][None, :], w_in[l].astype(BF16))
        attn = _attention(q, k, v)
        disc = _s5_discretize(lam_re[l], lam_im[l], log_dt[l], b_re[l], b_im[l], c_re[l], c_im[l])
        ssm = _s5(u, disc, d_skip[l].reshape(1, SSM_WIDTH), w_glu[l].astype(BF16))
        x = _outproj(attn, ssm, x, gn_attn[l][None, :], gn_ssm[l][None, :], w_out[l].astype(BF16))

        w_r = jnp.zeros((ROUTER_ROWS, D_MODEL), F32)
        w_r = w_r.at[0:ng].set(w_router_grp[l].T)
        w_r = w_r.at[EXPERT_ROW0:EXPERT_ROW0 + N_EXPERTS].set(
            jnp.transpose(w_router_exp[l], (0, 2, 1)).reshape(N_EXPERTS, D_MODEL))
        b_r = jnp.zeros((ROUTER_ROWS, 1), F32)
        b_r = b_r.at[0:ng, 0].set(b_router_grp[l])
        b_r = b_r.at[EXPERT_ROW0:EXPERT_ROW0 + N_EXPERTS, 0].set(b_router_exp[l].reshape(N_EXPERTS))
        w_r_hi, w_r_lo = _split_bf16(w_r)
        gates = _router(x, ln2_g[l][None, :], w_r_hi, w_r_lo, b_r)
        x = _moe(x, ln2_g[l][None, :], gates, gate_expand,
                 w_gate[l].astype(BF16).reshape(ng, npg, D_MODEL, EXPERT_FF),
                 w_up[l].astype(BF16).reshape(ng, npg, D_MODEL, EXPERT_FF),
                 w_down[l].astype(BF16).reshape(ng, npg, EXPERT_FF, D_MODEL))
    return _final(x, final_g[None, :])
```

```python
import functools
import math

import jax
import jax.numpy as jnp
from jax import lax
from jax.experimental import pallas as pl
from jax.experimental.pallas import tpu as pltpu

D_MODEL = 1024
BATCH = 8
SEQ = 2048
DEPTH = 4
ATTN_WIDTH = 512
HEAD_DIM = 64
ATTN_HEADS = 8
DILATIONS = (1, 4, 16)
SPAN = 128
SSM_WIDTH = 512
SSM_GROUP_CH = 16
SSM_GROUPS = 32
SSM_STATE = 64
MOE_GROUPS = 4
EXPERTS_PER_GROUP = 4
N_EXPERTS = 16
EXPERT_FF = 256
RMS_EPS = 1e-6

LANES = 128
BF16_ROWS = 16
VMEM_LIMIT_BYTES = 56 * 1024 * 1024

TOK_TILE = 512
PERM_TILE = 256
S5_CHUNK = 128
S5_GROUPS_PER_CHUNK = 8
S5_NCHUNK = SSM_GROUPS // S5_GROUPS_PER_CHUNK
S5_STATE_LANES = S5_GROUPS_PER_CHUNK * SSM_STATE
S5_SUB = BF16_ROWS

F32 = jnp.float32
BF16 = jnp.bfloat16


def _params(*sem):
    return pltpu.CompilerParams(dimension_semantics=sem, vmem_limit_bytes=VMEM_LIMIT_BYTES)


def _rms(x, g):
    return x * lax.rsqrt(jnp.mean(x * x, axis=-1, keepdims=True) + RMS_EPS) * g


def _split_bf16(a):
    hi = a.astype(BF16)
    lo = (a - hi.astype(F32)).astype(BF16)
    return hi, lo


def _dot(a, b):
    return jnp.dot(a, b, preferred_element_type=F32)


def _dot_hilo(p, a):
    hi, lo = _split_bf16(a)
    return _dot(p, hi) + _dot(p, lo)


def _tok_spec(width):
    return pl.BlockSpec((None, TOK_TILE, width), lambda b, si: (b, si, 0))


def _const_spec(shape):
    return pl.BlockSpec(shape, lambda *_: (0,) * len(shape))


def _inproj_kernel(x_ref, g_ref, w_ref, q_ref, k_ref, v_ref, u_ref):
    h = _rms(x_ref[...], g_ref[...]).astype(BF16)
    p = _dot(h, w_ref[...])
    aw = ATTN_WIDTH
    q_ref[...] = (p[:, :aw] * (1.0 / math.sqrt(HEAD_DIM))).astype(BF16)
    k_ref[...] = p[:, aw:2 * aw].astype(BF16)
    v_ref[...] = p[:, 2 * aw:3 * aw].astype(BF16)
    u_ref[...] = p[:, 3 * aw:].astype(BF16)


def _inproj(x, g, w):
    aw = ATTN_WIDTH
    out = jax.ShapeDtypeStruct((BATCH, SEQ, aw), BF16)
    return pl.pallas_call(
        _inproj_kernel,
        grid=(BATCH, SEQ // TOK_TILE),
        in_specs=[_tok_spec(D_MODEL), _const_spec((1, D_MODEL)),
                  _const_spec((D_MODEL, 3 * aw + SSM_WIDTH))],
        out_specs=[_tok_spec(aw)] * 4,
        out_shape=[out] * 4,
        compiler_params=_params("parallel", "parallel"),
        name="inproj",
    )(x, g, w)


def _attn_block(q, kk, vv, bias, low_half, lane):
    heads = range(ATTN_HEADS)
    nt = (((1,), (1,)), ((), ()))
    scores = []
    for h in heads:
        sl = slice((h // 2) * LANES, (h // 2 + 1) * LANES)
        keep = low_half if h % 2 == 0 else jnp.logical_not(low_half)
        qm = jnp.where(keep, q[:, sl], jnp.zeros_like(q[:, sl]))
        scores.append(lax.dot_general(qm, kk[:, sl], nt, preferred_element_type=F32) + bias)
    probs, dens, lses = [], [], []
    for h in heads:
        m = jnp.max(scores[h], axis=-1, keepdims=True)
        p = jnp.exp(scores[h] - m)
        den = jnp.sum(p, axis=-1, keepdims=True)
        probs.append(p.astype(BF16))
        dens.append(den)
        lses.append(m + jnp.log(den))
    pv = [_dot(probs[h], vv[:, (h // 2) * LANES:(h // 2 + 1) * LANES]) / dens[h] for h in heads]
    lse_tile = jnp.zeros((SPAN, LANES), F32)
    for h in heads:
        lse_tile = jnp.where(lane == h, lses[h], lse_tile)
    outs = [jnp.where(low_half, pv[2 * j], pv[2 * j + 1]) for j in range(ATTN_HEADS // 2)]
    return jnp.concatenate(outs, axis=-1), lse_tile


def _attn_kernel(q_ref, k_ref, v_ref, p4_ref, p4t_ref, p16_ref, p16t_ref, e_ref, o_ref,
                 qkv4, qkv16, lse1, o4p, lse4p, o16p, lse16p):
    lane = lax.broadcasted_iota(jnp.int32, (SPAN, LANES), 1)
    low_half = lane < HEAD_DIM
    row0 = lax.broadcasted_iota(jnp.int32, (SPAN, SPAN), 0)
    col0 = lax.broadcasted_iota(jnp.int32, (SPAN, SPAN), 1)
    bias_first = jnp.where(col0 <= row0, 0.0, -jnp.inf).astype(F32)
    row = lax.broadcasted_iota(jnp.int32, (SPAN, 2 * SPAN), 0)
    col = lax.broadcasted_iota(jnp.int32, (SPAN, 2 * SPAN), 1)
    bias_band = jnp.where((col >= row) & (col <= row + SPAN), 0.0, -jnp.inf).astype(F32)
    block = functools.partial(_attn_block, low_half=low_half, lane=lane)
    srcs = (q_ref, k_ref, v_ref)

    def permute(i, carry):
        rows = pl.ds(pl.multiple_of(i * PERM_TILE, PERM_TILE), PERM_TILE)
        for a, src in enumerate(srcs):
            x = src[rows, :]
            y4 = _dot(p4_ref[...], x).astype(BF16)
            y16 = _dot(p16_ref[...], x).astype(BF16)
            n4, n16 = PERM_TILE // 4, PERM_TILE // 16
            for r in range(4):
                qkv4[a, r, pl.ds(pl.multiple_of(i * n4, n4), n4), :] = y4[r * n4:(r + 1) * n4]
            for r in range(16):
                qkv16[a, r, pl.ds(pl.multiple_of(i * n16, n16), n16), :] = y16[r * n16:(r + 1) * n16]
        return carry

    lax.fori_loop(0, SEQ // PERM_TILE, permute, 0)

    def branch(get, put, nblocks):
        head = pl.ds(0, SPAN)
        o, lse = block(get(0, head), get(1, head), get(2, head), bias_first)
        put(head, o, lse)
        if nblocks > 1:
            def body(n, carry):
                qs = pl.ds(pl.multiple_of(n * SPAN, SPAN), SPAN)
                ks = pl.ds(pl.multiple_of((n - 1) * SPAN, SPAN), 2 * SPAN)
                o, lse = block(get(0, qs), get(1, ks), get(2, ks), bias_band)
                put(qs, o, lse)
                return carry
            lax.fori_loop(1, nblocks, body, 0)

    def put1(rows, o, lse):
        o_ref[rows, :] = o
        lse1[rows, :] = lse

    branch(lambda a, rows: srcs[a][rows, :], put1, SEQ // SPAN)

    def class4(r, carry):
        def put(rows, o, lse):
            o4p[r, rows, :] = o.astype(BF16)
            lse4p[r, rows, :] = lse
        branch(lambda a, rows: qkv4[a, r, rows, :], put, SEQ // 4 // SPAN)
        return carry

    lax.fori_loop(0, 4, class4, 0)

    def class16(r, carry):
        def put(rows, o, lse):
            o16p[r, rows, :] = o.astype(BF16)
            lse16p[r, rows, :] = lse
        branch(lambda a, rows: qkv16[a, r, rows, :], put, SEQ // 16 // SPAN)
        return carry

    lax.fori_loop(0, 16, class16, 0)

    def expand(w):
        hi, lo = _split_bf16(w)
        return _dot(hi, e_ref[...]) + _dot(lo, e_ref[...])

    def combine(i, carry):
        rows = pl.ds(pl.multiple_of(i * PERM_TILE, PERM_TILE), PERM_TILE)
        n4, n16 = PERM_TILE // 4, PERM_TILE // 16
        r4 = pl.ds(pl.multiple_of(i * n4, n4), n4)
        r16 = pl.ds(pl.multiple_of(i * n16, n16), n16)
        o4 = _dot(p4t_ref[...], jnp.concatenate([o4p[r, r4, :] for r in range(4)], axis=0))
        l4 = _dot_hilo(p4t_ref[...], jnp.concatenate([lse4p[r, r4, :] for r in range(4)], axis=0))
        o16 = _dot(p16t_ref[...], jnp.concatenate([o16p[r, r16, :] for r in range(16)], axis=0))
        l16 = _dot_hilo(p16t_ref[...], jnp.concatenate([lse16p[r, r16, :] for r in range(16)], axis=0))
        l1 = lse1[rows, :]
        m = jnp.maximum(jnp.maximum(l1, l4), l16)
        e1, e4, e16 = jnp.exp(l1 - m), jnp.exp(l4 - m), jnp.exp(l16 - m)
        den = e1 + e4 + e16
        o_ref[rows, :] = (expand(e1 / den) * o_ref[rows, :] + expand(e4 / den) * o4
                          + expand(e16 / den) * o16)
        return carry

    lax.fori_loop(0, SEQ // PERM_TILE, combine, 0)


def _perm_matrix(dil):
    n = PERM_TILE // dil
    out_row = jnp.arange(PERM_TILE)
    src = dil * (out_row % n) + out_row // n
    return (src[:, None] == jnp.arange(PERM_TILE)[None, :]).astype(BF16)


def _head_expand_matrix():
    r = jnp.arange(LANES)[:, None]
    c = jnp.arange(ATTN_WIDTH)[None, :] // HEAD_DIM
    return (r == c).astype(BF16)


def _attention(q, k, v):
    aw = ATTN_WIDTH
    seq_spec = pl.BlockSpec((None, SEQ, aw), lambda b: (b, 0, 0))
    p4, p16 = _perm_matrix(4), _perm_matrix(16)
    consts = (p4, p4.T, p16, p16.T, _head_expand_matrix())
    return pl.pallas_call(
        _attn_kernel,
        grid=(BATCH,),
        in_specs=[seq_spec] * 3 + [_const_spec(c.shape) for c in consts],
        out_specs=seq_spec,
        out_shape=jax.ShapeDtypeStruct((BATCH, SEQ, aw), F32),
        scratch_shapes=[pltpu.VMEM((3, 4, SEQ // 4, aw), BF16),
                        pltpu.VMEM((3, 16, SEQ // 16, aw), BF16),
                        pltpu.VMEM((SEQ, LANES), F32),
                        pltpu.VMEM((4, SEQ // 4, aw), BF16),
                        pltpu.VMEM((4, SEQ // 4, LANES), F32),
                        pltpu.VMEM((16, SEQ // 16, aw), BF16),
                        pltpu.VMEM((16, SEQ // 16, LANES), F32)],
        compiler_params=_params("parallel"),
        name="attention",
    )(q, k, v, *consts)


def _s5_kernel(u_ref, pf_ref, pb_ref, bre_ref, bim_ref, lre_ref, lim_ref, cre_ref, cim_ref, d_ref,
               wglu_ref, o_ref, st_re, st_im, u_buf, xr_buf, xi_buf, y_buf):
    rows = S5_CHUNK * BATCH
    sub_rows = S5_SUB * BATCH

    @pl.when(pl.program_id(0) == 0)
    def _():
        st_re[...] = jnp.zeros_like(st_re)
        st_im[...] = jnp.zeros_like(st_im)

    for tb in range(S5_CHUNK // S5_SUB):
        t = slice(tb * S5_SUB, (tb + 1) * S5_SUB)
        piece = jnp.concatenate([u_ref[b, t, :] for b in range(BATCH)], axis=0)
        u_buf[tb * sub_rows:(tb + 1) * sub_rows, :] = _dot(pf_ref[...], piece)

    for c in range(S5_NCHUNK):
        ch = slice(c * LANES, (c + 1) * LANES)
        stl = slice(c * S5_STATE_LANES, (c + 1) * S5_STATE_LANES)
        ub = u_buf[:, ch].astype(BF16)
        xr_buf[...] = _dot(ub, bre_ref[c])
        xi_buf[...] = _dot(ub, bim_ref[c])
        lr = jnp.broadcast_to(lre_ref[:, stl], (BATCH, S5_STATE_LANES))
        li = jnp.broadcast_to(lim_ref[:, stl], (BATCH, S5_STATE_LANES))

        def step(s, carry):
            xr, xi = carry
            sl = pl.ds(pl.multiple_of(s * BATCH, BATCH), BATCH)
            nr = lr * xr - li * xi + xr_buf[sl, :]
            ni = lr * xi + li * xr + xi_buf[sl, :]
            xr_buf[sl, :] = nr
            xi_buf[sl, :] = ni
            return nr, ni

        xr, xi = lax.fori_loop(0, S5_CHUNK, step, (st_re[:, stl], st_im[:, stl]), unroll=4)
        st_re[:, stl] = xr
        st_im[:, stl] = xi
        yc = (_dot(xr_buf[...].astype(BF16), cre_ref[c]) - _dot(xi_buf[...].astype(BF16), cim_ref[c]))
        y_buf[:, ch] = yc + d_ref[:, ch] * u_buf[:, ch]

    y = jax.nn.gelu(y_buf[...])
    z = _dot(y.astype(BF16), wglu_ref[...])
    y_buf[...] = y * jax.nn.sigmoid(z)
    for tb in range(S5_CHUNK // S5_SUB):
        back = _dot(pb_ref[...], y_buf[tb * sub_rows:(tb + 1) * sub_rows, :].astype(BF16)).astype(BF16)
        for b in range(BATCH):
            o_ref[b, tb * S5_SUB:(tb + 1) * S5_SUB, :] = back[b * S5_SUB:(b + 1) * S5_SUB]


def _s5_discretize(lam_re, lam_im, log_dt, b_re, b_im, c_re, c_im):
    dt = jnp.exp(log_dt)[:, None]
    mag = jnp.exp(lam_re * dt)
    lb_re, lb_im = mag * jnp.cos(lam_im * dt), mag * jnp.sin(lam_im * dt)
    den = lam_re * lam_re + lam_im * lam_im
    nr, ni = lb_re - 1.0, lb_im
    f_re = (nr * lam_re + ni * lam_im) / den
    f_im = (ni * lam_re - nr * lam_im) / den
    bb_re = f_re[..., None] * b_re - f_im[..., None] * b_im
    bb_im = f_re[..., None] * b_im + f_im[..., None] * b_re
    eye = jnp.eye(S5_GROUPS_PER_CHUNK, dtype=F32)
    gpc, nch = S5_GROUPS_PER_CHUNK, S5_NCHUNK

    def b_blockdiag(b):
        b = b.reshape(nch, gpc, SSM_STATE, SSM_GROUP_CH)
        m = jnp.einsum('cgph,gk->cghkp', b, eye)
        return m.reshape(nch, gpc * SSM_GROUP_CH, gpc * SSM_STATE).astype(BF16)

    def c_blockdiag(c):
        c = c.reshape(nch, gpc, SSM_GROUP_CH, SSM_STATE)
        m = jnp.einsum('cghp,gk->cgpkh', c, eye)
        return m.reshape(nch, gpc * SSM_STATE, gpc * SSM_GROUP_CH).astype(BF16)

    return (b_blockdiag(bb_re), b_blockdiag(bb_im),
            lb_re.reshape(1, SSM_GROUPS * SSM_STATE), lb_im.reshape(1, SSM_GROUPS * SSM_STATE),
            c_blockdiag(c_re), c_blockdiag(c_im))


def _s5_reorder_matrices():
    n = S5_SUB * BATCH
    out_row = jnp.arange(n)
    src = (out_row % BATCH) * S5_SUB + out_row // BATCH
    fwd = (src[:, None] == jnp.arange(n)[None, :]).astype(BF16)
    return fwd, fwd.T


def _s5(u, disc, d_skip, w_glu):
    rows = S5_CHUNK * BATCH
    blk = pl.BlockSpec((BATCH, S5_CHUNK, SSM_WIDTH), lambda i: (0, i, 0))
    args = _s5_reorder_matrices() + tuple(disc) + (d_skip, w_glu)
    return pl.pallas_call(
        _s5_kernel,
        grid=(SEQ // S5_CHUNK,),
        in_specs=[blk] + [_const_spec(a.shape) for a in args],
        out_specs=blk,
        out_shape=jax.ShapeDtypeStruct((BATCH, SEQ, SSM_WIDTH), BF16),
        scratch_shapes=[pltpu.VMEM((BATCH, SSM_GROUPS * SSM_STATE), F32),
                        pltpu.VMEM((BATCH, SSM_GROUPS * SSM_STATE), F32),
                        pltpu.VMEM((rows, SSM_WIDTH), F32),
                        pltpu.VMEM((rows, S5_STATE_LANES), F32),
                        pltpu.VMEM((rows, S5_STATE_LANES), F32),
                        pltpu.VMEM((rows, SSM_WIDTH), F32)],
        compiler_params=_params("arbitrary"),
        name="s5",
    )(u, *args)


def _outproj_kernel(attn_ref, ssm_ref, x_ref, ga_ref, gs_ref, w_ref, xo_ref):
    a_n = _rms(attn_ref[...], ga_ref[...]).astype(BF16)
    s_n = _rms(ssm_ref[...].astype(F32), gs_ref[...]).astype(BF16)
    y = _dot(a_n, w_ref[0:ATTN_WIDTH, :]) + _dot(s_n, w_ref[ATTN_WIDTH:, :])
    xo_ref[...] = x_ref[...] + y


def _outproj(attn, ssm, x, ga, gs, w):
    aw = ATTN_WIDTH
    return pl.pallas_call(
        _outproj_kernel,
        grid=(BATCH, SEQ // TOK_TILE),
        in_specs=[_tok_spec(aw), _tok_spec(SSM_WIDTH), _tok_spec(D_MODEL),
                  _const_spec((1, aw)), _const_spec((1, SSM_WIDTH)),
                  _const_spec((aw + SSM_WIDTH, D_MODEL))],
        out_specs=_tok_spec(D_MODEL),
        out_shape=jax.ShapeDtypeStruct((BATCH, SEQ, D_MODEL), F32),
        compiler_params=_params("parallel", "parallel"),
        name="outproj",
    )(attn, ssm, x, ga, gs, w)


ROUTER_ROWS = 32
EXPERT_ROW0 = 8
PAIRS_PER_GROUP = EXPERTS_PER_GROUP * (EXPERTS_PER_GROUP - 1) // 2
N_BUCKETS = MOE_GROUPS * PAIRS_PER_GROUP
REC_BUCKET, REC_W_LO, REC_W_HI = 0, 1, 2
TOKENS = BATCH * SEQ
PLAN_SIDE = 128
MOE_TILE = 256
MOE_NTILES = TOKENS // MOE_TILE + N_BUCKETS
MOE_SLOTS = MOE_NTILES * MOE_TILE
MOE_ROW = D_MODEL + LANES
DISPATCH_TILE = 256
assert PLAN_SIDE * PLAN_SIDE == TOKENS and MOE_NTILES <= PLAN_SIDE


def _router_kernel(x_ref, g_ref, whi_ref, wlo_ref, b_ref, rec_ref, bucket_ref):
    h = _rms(x_ref[...], g_ref[...])
    h_hi, h_lo = _split_bf16(h)
    nt = (((1,), (1,)), ((), ()))
    logits = (lax.dot_general(whi_ref[...], h_hi, nt, preferred_element_type=F32)
              + lax.dot_general(wlo_ref[...], h_hi, nt, preferred_element_type=F32)
              + lax.dot_general(whi_ref[...], h_lo, nt, preferred_element_type=F32)
              + b_ref[...])
    ng, ne = MOE_GROUPS, EXPERTS_PER_GROUP
    gl = [logits[g:g + 1, :] for g in range(ng)]
    best, grp = gl[0], jnp.zeros_like(gl[0], dtype=jnp.int32)
    for g in range(1, ng):
        better = gl[g] > best
        grp = jnp.where(better, g, grp)
        best = jnp.where(better, gl[g], best)
    g1 = 1.0 / sum(jnp.exp(x - best) for x in gl)
    sel = []
    for e in range(ne):
        acc = jnp.zeros_like(best)
        for g in range(ng):
            r = EXPERT_ROW0 + g * ne + e
            acc = jnp.where(grp == g, logits[r:r + 1, :], acc)
        sel.append(acc)

    def first_argmax(vals):
        bv, bi = vals[0], jnp.zeros_like(grp)
        for e in range(1, ne):
            better = vals[e] > bv
            bi = jnp.where(better, e, bi)
            bv = jnp.where(better, vals[e], bv)
        return bv, bi

    v1, i1 = first_argmax(sel)
    v2, i2 = first_argmax([jnp.where(i1 == e, -jnp.inf, sel[e]) for e in range(ne)])
    e2 = jnp.exp(v2 - v1)
    w1 = g1 / (1.0 + e2)
    w2 = g1 * e2 / (1.0 + e2)
    first_is_low = i1 < i2
    lo = jnp.where(first_is_low, i1, i2)
    hi = jnp.where(first_is_low, i2, i1)
    pair = jnp.where(lo == 0, 0, jnp.where(lo == 1, 3, 5)) + hi - lo - 1
    bucket = (grp * PAIRS_PER_GROUP + pair).astype(F32)
    w_lo = jnp.where(first_is_low, w1, w2)
    w_hi = jnp.where(first_is_low, w2, w1)
    tokens = logits.shape[1]
    rowid = lax.broadcasted_iota(jnp.int32, (LANES, tokens), 0)
    table = jnp.where(rowid == REC_BUCKET, bucket,
                      jnp.where(rowid == REC_W_LO, w_lo, jnp.where(rowid == REC_W_HI, w_hi, 0.0)))
    rec_ref[...] = table.T
    bucket_ref[...] = bucket


def _router(x, g, whi, wlo, bias):
    n_s = SEQ // TOK_TILE
    return pl.pallas_call(
        _router_kernel,
        grid=(BATCH, n_s),
        in_specs=[_tok_spec(D_MODEL), _const_spec((1, D_MODEL)),
                  _const_spec((ROUTER_ROWS, D_MODEL)), _const_spec((ROUTER_ROWS, D_MODEL)),
                  _const_spec((ROUTER_ROWS, 1))],
        out_specs=[pl.BlockSpec((TOK_TILE, LANES), lambda b, si: (b * n_s + si, 0)),
                   pl.BlockSpec((1, TOK_TILE), lambda b, si: (0, b * n_s + si))],
        out_shape=[jax.ShapeDtypeStruct((TOKENS, LANES), F32),
                   jax.ShapeDtypeStruct((1, TOKENS), F32)],
        compiler_params=_params("parallel", "parallel"),
        name="router",
    )(x, g, whi, wlo, bias)


def _plan_kernel(bucket_ref, pos_ref, tile_bucket_ref):
    n = PLAN_SIDE
    bucket = bucket_ref[...]
    r = lax.broadcasted_iota(jnp.int32, (n, n), 0)
    c = lax.broadcasted_iota(jnp.int32, (n, n), 1)
    before_in_row = (r < c).astype(BF16)
    rows_before = (c < r).astype(BF16)
    ones = jnp.ones((n, n), BF16)
    tile_start = (c * MOE_TILE).astype(F32)
    pos = jnp.zeros((n, n), F32)
    base = jnp.zeros((n, n), F32)
    ended = jnp.zeros((n, n), F32)
    for k in range(N_BUCKETS):
        member = bucket == float(k)
        mb = member.astype(BF16)
        in_row = _dot(mb, before_in_row)
        row_count = _dot(mb, ones).astype(BF16)
        rank = _dot(rows_before, row_count) + in_row
        total = _dot(ones, row_count)
        pos = jnp.where(member, base + rank, pos)
        base = base + jnp.ceil(total * (1.0 / MOE_TILE)) * MOE_TILE
        ended = ended + (tile_start >= base).astype(F32)
    pos_ref[...] = pos.astype(jnp.int32)
    tile_bucket_ref[...] = ended.astype(jnp.int32)


def _plan(bucket):
    n = PLAN_SIDE
    pos, tile_bucket = pl.pallas_call(
        _plan_kernel,
        out_shape=[jax.ShapeDtypeStruct((n, n), jnp.int32)] * 2,
        name="moe_plan",
    )(bucket.reshape(n, n))
    return pos.reshape(TOKENS), tile_bucket[0, :MOE_NTILES]


def _row_copy(src, src_row, dst, dst_row, sem):
    return pltpu.make_async_copy(src.at[pl.ds(src_row, 1), :], dst.at[pl.ds(dst_row, 1), :], sem)


def _dispatch_kernel(pos_ref, x_ref, g_ref, rec_ref, zeros_ref, xs_ref, buf, sems):
    del zeros_ref
    i = pl.program_id(0)
    last = pl.num_programs(0) - 1
    slot = i % 2

    def wait_rows(s):
        pltpu.make_async_copy(buf.at[s], xs_ref.at[pl.ds(0, DISPATCH_TILE), :], sems.at[s]).wait()

    @pl.when(i >= 2)
    def _():
        wait_rows(slot)

    buf[slot, :, 0:D_MODEL] = _rms(x_ref[...], g_ref[...])
    buf[slot, :, D_MODEL:] = rec_ref[...]

    def send(r, carry):
        _row_copy(buf.at[slot], r, xs_ref, pos_ref[i * DISPATCH_TILE + r], sems.at[slot]).start()
        return carry

    lax.fori_loop(0, DISPATCH_TILE, send, 0, unroll=8)

    @pl.when(i == last)
    def _():
        wait_rows(1 - slot)
        wait_rows(slot)


def _dispatch(pos, x, g, rec):
    grid_spec = pltpu.PrefetchScalarGridSpec(
        num_scalar_prefetch=1,
        grid=(TOKENS // DISPATCH_TILE,),
        in_specs=[pl.BlockSpec((DISPATCH_TILE, D_MODEL), lambda i, pos: (i, 0)),
                  pl.BlockSpec((1, D_MODEL), lambda i, pos: (0, 0)),
                  pl.BlockSpec((DISPATCH_TILE, LANES), lambda i, pos: (i, 0)),
                  pl.BlockSpec(memory_space=pl.ANY)],
        out_specs=pl.BlockSpec(memory_space=pl.ANY),
        scratch_shapes=[pltpu.VMEM((2, DISPATCH_TILE, MOE_ROW), F32),
                        pltpu.SemaphoreType.DMA((2,))])
    return pl.pallas_call(
        _dispatch_kernel,
        grid_spec=grid_spec,
        out_shape=jax.ShapeDtypeStruct((MOE_SLOTS, MOE_ROW), F32),
        input_output_aliases={4: 0},
        compiler_params=_params("arbitrary"),
        name="moe_dispatch",
    )(pos, x, g, rec, jnp.zeros((MOE_SLOTS, MOE_ROW), F32))


def _experts_kernel(e_lo_ref, e_hi_ref, used_ref, xs_ref, wg_lo, wu_lo, wd_lo, wg_hi, wu_hi, wd_hi, ys_ref):
    i = pl.program_id(0)

    @pl.when(used_ref[i] != 0)
    def _():
        h = xs_ref[:, 0:D_MODEL].astype(BF16)
        rec = xs_ref[:, D_MODEL:]

        def expert(wg, wu, wd, lane):
            hg = _dot(h, wg[...].astype(BF16))
            hu = _dot(h, wu[...].astype(BF16))
            act = jax.nn.silu(hg) * hu * rec[:, lane:lane + 1]
            return _dot(act.astype(BF16), wd[...].astype(BF16))

        ys_ref[...] = expert(wg_lo, wu_lo, wd_lo, REC_W_LO) + expert(wg_hi, wu_hi, wd_hi, REC_W_HI)

    @pl.when(used_ref[i] == 0)
    def _():
        ys_ref[...] = jnp.zeros_like(ys_ref)


def _experts(e_lo, e_hi, used, xs, w_gate, w_up, w_down):
    up = lambda pick: pl.BlockSpec((None, D_MODEL, EXPERT_FF), lambda i, lo, hi, u: (pick(lo, hi)[i], 0, 0))
    down = lambda pick: pl.BlockSpec((None, EXPERT_FF, D_MODEL), lambda i, lo, hi, u: (pick(lo, hi)[i], 0, 0))
    first, second = (lambda lo, hi: lo), (lambda lo, hi: hi)
    grid_spec = pltpu.PrefetchScalarGridSpec(
        num_scalar_prefetch=3,
        grid=(MOE_NTILES,),
        in_specs=[pl.BlockSpec((MOE_TILE, MOE_ROW), lambda i, lo, hi, u: (i, 0)),
                  up(first), up(first), down(first), up(second), up(second), down(second)],
        out_specs=pl.BlockSpec((MOE_TILE, D_MODEL), lambda i, lo, hi, u: (i, 0)))
    return pl.pallas_call(
        _experts_kernel,
        grid_spec=grid_spec,
        out_shape=jax.ShapeDtypeStruct((MOE_SLOTS, D_MODEL), F32),
        compiler_params=_params("arbitrary"),
        name="moe_experts",
    )(e_lo, e_hi, used, xs, w_gate, w_up, w_down, w_gate, w_up, w_down)


def _combine_kernel(pos_ref, x_ref, ys_ref, xo_ref, buf, sems):
    i = pl.program_id(0)
    n = pl.num_programs(0)
    slot = i % 2

    def fetch(step, s):
        def body(r, carry):
            _row_copy(ys_ref, pos_ref[step * DISPATCH_TILE + r], buf.at[s], r, sems.at[s]).start()
            return carry
        lax.fori_loop(0, DISPATCH_TILE, body, 0, unroll=8)

    @pl.when(i == 0)
    def _():
        fetch(0, 0)

    @pl.when(i + 1 < n)
    def _():
        fetch(i + 1, 1 - slot)

    pltpu.make_async_copy(ys_ref.at[pl.ds(0, DISPATCH_TILE), :], buf.at[slot], sems.at[slot]).wait()
    xo_ref[...] = x_ref[...] + buf[slot]


def _combine(pos, x, ys):
    grid_spec = pltpu.PrefetchScalarGridSpec(
        num_scalar_prefetch=1,
        grid=(TOKENS // DISPATCH_TILE,),
        in_specs=[pl.BlockSpec((DISPATCH_TILE, D_MODEL), lambda i, pos: (i, 0)),
                  pl.BlockSpec(memory_space=pl.ANY)],
        out_specs=pl.BlockSpec((DISPATCH_TILE, D_MODEL), lambda i, pos: (i, 0)),
        scratch_shapes=[pltpu.VMEM((2, DISPATCH_TILE, D_MODEL), F32),
                        pltpu.SemaphoreType.DMA((2,))])
    return pl.pallas_call(
        _combine_kernel,
        grid_spec=grid_spec,
        out_shape=jax.ShapeDtypeStruct((TOKENS, D_MODEL), F32),
        compiler_params=_params("arbitrary"),
        name="moe_combine",
    )(pos, x, ys)


def _bucket_experts():
    pairs = [(a, b) for a in range(EXPERTS_PER_GROUP) for b in range(a + 1, EXPERTS_PER_GROUP)]
    lo = [g * EXPERTS_PER_GROUP + a for g in range(MOE_GROUPS) for a, _ in pairs]
    hi = [g * EXPERTS_PER_GROUP + b for g in range(MOE_GROUPS) for _, b in pairs]
    return jnp.array(lo, jnp.int32), jnp.array(hi, jnp.int32)


def _routed_moe(x, g, whi, wlo, bias, w_gate, w_up, w_down):
    rec, bucket = _router(x, g, whi, wlo, bias)
    pos, tile_bucket = _plan(bucket)
    used = (tile_bucket < N_BUCKETS).astype(jnp.int32)
    tile_bucket = jnp.minimum(tile_bucket, N_BUCKETS - 1)
    lo, hi = _bucket_experts()
    x_rows = x.reshape(TOKENS, D_MODEL)
    xs = _dispatch(pos, x_rows, g, rec)
    ys = _experts(lo[tile_bucket], hi[tile_bucket], used, xs, w_gate, w_up, w_down)
    return _combine(pos, x_rows, ys).reshape(BATCH, SEQ, D_MODEL)


def _final_kernel(x_ref, g_ref, o_ref):
    o_ref[...] = _rms(x_ref[...], g_ref[...])


def _final(x, g):
    return pl.pallas_call(
        _final_kernel,
        grid=(BATCH, SEQ // TOK_TILE),
        in_specs=[_tok_spec(D_MODEL), _const_spec((1, D_MODEL))],
        out_specs=_tok_spec(D_MODEL),
        out_shape=jax.ShapeDtypeStruct((BATCH, SEQ, D_MODEL), F32),
        compiler_params=_params("parallel", "parallel"),
        name="final_norm",
    )(x, g)


def kernel(x, ln1_g, w_in, lam_re, lam_im, log_dt, b_re, b_im, c_re, c_im, d_skip, w_glu,
           gn_attn, gn_ssm, w_out, ln2_g, w_router_grp, b_router_grp, w_router_exp,
           b_router_exp, w_gate, w_up, w_down, final_g):
    assert x.shape == (BATCH, SEQ, D_MODEL)
    ng = MOE_GROUPS

    for l in range(DEPTH):
        q, k, v, u = _inproj(x, ln1_g[l][None, :], w_in[l].astype(BF16))
        attn = _attention(q, k, v)
        disc = _s5_discretize(lam_re[l], lam_im[l], log_dt[l], b_re[l], b_im[l], c_re[l], c_im[l])
        ssm = _s5(u, disc, d_skip[l].reshape(1, SSM_WIDTH), w_glu[l].astype(BF16))
        x = _outproj(attn, ssm, x, gn_attn[l][None, :], gn_ssm[l][None, :], w_out[l].astype(BF16))

        w_r = jnp.zeros((ROUTER_ROWS, D_MODEL), F32)
        w_r = w_r.at[0:ng].set(w_router_grp[l].T)
        w_r = w_r.at[EXPERT_ROW0:EXPERT_ROW0 + N_EXPERTS].set(
            jnp.transpose(w_router_exp[l], (0, 2, 1)).reshape(N_EXPERTS, D_MODEL))
        b_r = jnp.zeros((ROUTER_ROWS, 1), F32)
        b_r = b_r.at[0:ng, 0].set(b_router_grp[l])
        b_r = b_r.at[EXPERT_ROW0:EXPERT_ROW0 + N_EXPERTS, 0].set(b_router_exp[l].reshape(N_EXPERTS))
        w_r_hi, w_r_lo = _split_bf16(w_r)
        x = _routed_moe(x, ln2_g[l][None, :], w_r_hi, w_r_lo, b_r, w_gate[l], w_up[l], w_down[l])
    return _final(x, final_g[None, :])
```

```python
import functools
import math

import jax
import jax.numpy as jnp
from jax import lax
from jax.experimental import pallas as pl
from jax.experimental.pallas import tpu as pltpu

D_MODEL = 1024
BATCH = 8
SEQ = 2048
DEPTH = 4
ATTN_WIDTH = 512
HEAD_DIM = 64
ATTN_HEADS = 8
DILATIONS = (1, 4, 16)
SPAN = 128
SSM_WIDTH = 512
SSM_GROUP_CH = 16
SSM_GROUPS = 32
SSM_STATE = 64
MOE_GROUPS = 4
EXPERTS_PER_GROUP = 4
N_EXPERTS = 16
EXPERT_FF = 256
RMS_EPS = 1e-6

LANES = 128
BF16_ROWS = 16
VMEM_LIMIT_BYTES = 56 * 1024 * 1024

TOK_TILE = 512
PERM_TILE = 256
S5_CHUNK = 128
S5_GROUPS_PER_CHUNK = 8
S5_NCHUNK = SSM_GROUPS // S5_GROUPS_PER_CHUNK
S5_STATE_LANES = S5_GROUPS_PER_CHUNK * SSM_STATE
S5_SUB = BF16_ROWS

F32 = jnp.float32
BF16 = jnp.bfloat16


def _params(*sem):
    return pltpu.CompilerParams(dimension_semantics=sem, vmem_limit_bytes=VMEM_LIMIT_BYTES)


def _rms(x, g):
    return x * lax.rsqrt(jnp.mean(x * x, axis=-1, keepdims=True) + RMS_EPS) * g


def _split_bf16(a):
    hi = a.astype(BF16)
    lo = (a - hi.astype(F32)).astype(BF16)
    return hi, lo


def _dot(a, b):
    return jnp.dot(a, b, preferred_element_type=F32)


def _dot_hilo(p, a):
    hi, lo = _split_bf16(a)
    return _dot(p, hi) + _dot(p, lo)


def _tok_spec(width):
    return pl.BlockSpec((None, TOK_TILE, width), lambda b, si: (b, si, 0))


def _const_spec(shape):
    return pl.BlockSpec(shape, lambda *_: (0,) * len(shape))


def _inproj_kernel(x_ref, g_ref, w_ref, q_ref, k_ref, v_ref, u_ref):
    h = _rms(x_ref[...], g_ref[...]).astype(BF16)
    p = _dot(h, w_ref[...])
    aw = ATTN_WIDTH
    q_ref[...] = (p[:, :aw] * (1.0 / math.sqrt(HEAD_DIM))).astype(BF16)
    k_ref[...] = p[:, aw:2 * aw].astype(BF16)
    v_ref[...] = p[:, 2 * aw:3 * aw].astype(BF16)
    u_ref[...] = p[:, 3 * aw:].astype(BF16)


def _inproj(x, g, w):
    aw = ATTN_WIDTH
    out = jax.ShapeDtypeStruct((BATCH, SEQ, aw), BF16)
    return pl.pallas_call(
        _inproj_kernel,
        grid=(BATCH, SEQ // TOK_TILE),
        in_specs=[_tok_spec(D_MODEL), _const_spec((1, D_MODEL)),
                  _const_spec((D_MODEL, 3 * aw + SSM_WIDTH))],
        out_specs=[_tok_spec(aw)] * 4,
        out_shape=[out] * 4,
        compiler_params=_params("parallel", "parallel"),
        name="inproj",
    )(x, g, w)


def _attn_block(q, kk, vv, bias, low_half, lane):
    heads = range(ATTN_HEADS)
    nt = (((1,), (1,)), ((), ()))
    scores = []
    for h in heads:
        sl = slice((h // 2) * LANES, (h // 2 + 1) * LANES)
        keep = low_half if h % 2 == 0 else jnp.logical_not(low_half)
        qm = jnp.where(keep, q[:, sl], jnp.zeros_like(q[:, sl]))
        scores.append(lax.dot_general(qm, kk[:, sl], nt, preferred_element_type=F32) + bias)
    probs, dens, lses = [], [], []
    for h in heads:
        m = jnp.max(scores[h], axis=-1, keepdims=True)
        p = jnp.exp(scores[h] - m)
        den = jnp.sum(p, axis=-1, keepdims=True)
        probs.append(p.astype(BF16))
        dens.append(den)
        lses.append(m + jnp.log(den))
    pv = [_dot(probs[h], vv[:, (h // 2) * LANES:(h // 2 + 1) * LANES]) / dens[h] for h in heads]
    lse_tile = jnp.zeros((SPAN, LANES), F32)
    for h in heads:
        lse_tile = jnp.where(lane == h, lses[h], lse_tile)
    outs = [jnp.where(low_half, pv[2 * j], pv[2 * j + 1]) for j in range(ATTN_HEADS // 2)]
    return jnp.concatenate(outs, axis=-1), lse_tile


def _attn_kernel(q_ref, k_ref, v_ref, p4_ref, p4t_ref, p16_ref, p16t_ref, e_ref, o_ref,
                 qkv4, qkv16, lse1, o4p, lse4p, o16p, lse16p):
    lane = lax.broadcasted_iota(jnp.int32, (SPAN, LANES), 1)
    low_half = lane < HEAD_DIM
    row0 = lax.broadcasted_iota(jnp.int32, (SPAN, SPAN), 0)
    col0 = lax.broadcasted_iota(jnp.int32, (SPAN, SPAN), 1)
    bias_first = jnp.where(col0 <= row0, 0.0, -jnp.inf).astype(F32)
    row = lax.broadcasted_iota(jnp.int32, (SPAN, 2 * SPAN), 0)
    col = lax.broadcasted_iota(jnp.int32, (SPAN, 2 * SPAN), 1)
    bias_band = jnp.where((col >= row) & (col <= row + SPAN), 0.0, -jnp.inf).astype(F32)
    block = functools.partial(_attn_block, low_half=low_half, lane=lane)
    srcs = (q_ref, k_ref, v_ref)

    def permute(i, carry):
        rows = pl.ds(pl.multiple_of(i * PERM_TILE, PERM_TILE), PERM_TILE)
        for a, src in enumerate(srcs):
            x = src[rows, :]
            y4 = _dot(p4_ref[...], x).astype(BF16)
            y16 = _dot(p16_ref[...], x).astype(BF16)
            n4, n16 = PERM_TILE // 4, PERM_TILE // 16
            for r in range(4):
                qkv4[a, r, pl.ds(pl.multiple_of(i * n4, n4), n4), :] = y4[r * n4:(r + 1) * n4]
            for r in range(16):
                qkv16[a, r, pl.ds(pl.multiple_of(i * n16, n16), n16), :] = y16[r * n16:(r + 1) * n16]
        return carry

    lax.fori_loop(0, SEQ // PERM_TILE, permute, 0)

    def branch(get, put, nblocks):
        head = pl.ds(0, SPAN)
        o, lse = block(get(0, head), get(1, head), get(2, head), bias_first)
        put(head, o, lse)
        if nblocks > 1:
            def body(n, carry):
                qs = pl.ds(pl.multiple_of(n * SPAN, SPAN), SPAN)
                ks = pl.ds(pl.multiple_of((n - 1) * SPAN, SPAN), 2 * SPAN)
                o, lse = block(get(0, qs), get(1, ks), get(2, ks), bias_band)
                put(qs, o, lse)
                return carry
            lax.fori_loop(1, nblocks, body, 0)

    def put1(rows, o, lse):
        o_ref[rows, :] = o
        lse1[rows, :] = lse

    branch(lambda a, rows: srcs[a][rows, :], put1, SEQ // SPAN)

    def class4(r, carry):
        def put(rows, o, lse):
            o4p[r, rows, :] = o.astype(BF16)
            lse4p[r, rows, :] = lse
        branch(lambda a, rows: qkv4[a, r, rows, :], put, SEQ // 4 // SPAN)
        return carry

    lax.fori_loop(0, 4, class4, 0)

    def class16(r, carry):
        def put(rows, o, lse):
            o16p[r, rows, :] = o.astype(BF16)
            lse16p[r, rows, :] = lse
        branch(lambda a, rows: qkv16[a, r, rows, :], put, SEQ // 16 // SPAN)
        return carry

    lax.fori_loop(0, 16, class16, 0)

    def expand(w):
        hi, lo = _split_bf16(w)
        return _dot(hi, e_ref[...]) + _dot(lo, e_ref[...])

    def combine(i, carry):
        rows = pl.ds(pl.multiple_of(i * PERM_TILE, PERM_TILE), PERM_TILE)
        n4, n16 = PERM_TILE // 4, PERM_TILE // 16
        r4 = pl.ds(pl.multiple_of(i * n4, n4), n4)
        r16 = pl.ds(pl.multiple_of(i * n16, n16), n16)
        o4 = _dot(p4t_ref[...], jnp.concatenate([o4p[r, r4, :] for r in range(4)], axis=0))
        l4 = _dot_hilo(p4t_ref[...], jnp.concatenate([lse4p[r, r4, :] for r in range(4)], axis=0))
        o16 = _dot(p16t_ref[...], jnp.concatenate([o16p[r, r16, :] for r in range(16)], axis=0))
        l16 = _dot_hilo(p16t_ref[...], jnp.concatenate([lse16p[r, r16, :] for r in range(16)], axis=0))
        l1 = lse1[rows, :]
        m = jnp.maximum(jnp.maximum(l1, l4), l16)
        e1, e4, e16 = jnp.exp(l1 - m), jnp.exp(l4 - m), jnp.exp(l16 - m)
        den = e1 + e4 + e16
        o_ref[rows, :] = (expand(e1 / den) * o_ref[rows, :] + expand(e4 / den) * o4
                          + expand(e16 / den) * o16)
        return carry

    lax.fori_loop(0, SEQ // PERM_TILE, combine, 0)


def _perm_matrix(dil):
    n = PERM_TILE // dil
    out_row = jnp.arange(PERM_TILE)
    src = dil * (out_row % n) + out_row // n
    return (src[:, None] == jnp.arange(PERM_TILE)[None, :]).astype(BF16)


def _head_expand_matrix():
    r = jnp.arange(LANES)[:, None]
    c = jnp.arange(ATTN_WIDTH)[None, :] // HEAD_DIM
    return (r == c).astype(BF16)


def _attention(q, k, v):
    aw = ATTN_WIDTH
    seq_spec = pl.BlockSpec((None, SEQ, aw), lambda b: (b, 0, 0))
    p4, p16 = _perm_matrix(4), _perm_matrix(16)
    consts = (p4, p4.T, p16, p16.T, _head_expand_matrix())
    return pl.pallas_call(
        _attn_kernel,
        grid=(BATCH,),
        in_specs=[seq_spec] * 3 + [_const_spec(c.shape) for c in consts],
        out_specs=seq_spec,
        out_shape=jax.ShapeDtypeStruct((BATCH, SEQ, aw), F32),
        scratch_shapes=[pltpu.VMEM((3, 4, SEQ // 4, aw), BF16),
                        pltpu.VMEM((3, 16, SEQ // 16, aw), BF16),
                        pltpu.VMEM((SEQ, LANES), F32),
                        pltpu.VMEM((4, SEQ // 4, aw), BF16),
                        pltpu.VMEM((4, SEQ // 4, LANES), F32),
                        pltpu.VMEM((16, SEQ // 16, aw), BF16),
                        pltpu.VMEM((16, SEQ // 16, LANES), F32)],
        compiler_params=_params("parallel"),
        name="attention",
    )(q, k, v, *consts)


def _s5_kernel(u_ref, pf_ref, pb_ref, bre_ref, bim_ref, lre_ref, lim_ref, cre_ref, cim_ref, d_ref,
               wglu_ref, o_ref, st_re, st_im, u_buf, xr_buf, xi_buf, y_buf):
    rows = S5_CHUNK * BATCH
    sub_rows = S5_SUB * BATCH

    @pl.when(pl.program_id(0) == 0)
    def _():
        st_re[...] = jnp.zeros_like(st_re)
        st_im[...] = jnp.zeros_like(st_im)

    for tb in range(S5_CHUNK // S5_SUB):
        t = slice(tb * S5_SUB, (tb + 1) * S5_SUB)
        piece = jnp.concatenate([u_ref[b, t, :] for b in range(BATCH)], axis=0)
        u_buf[tb * sub_rows:(tb + 1) * sub_rows, :] = _dot(pf_ref[...], piece)

    for c in range(S5_NCHUNK):
        ch = slice(c * LANES, (c + 1) * LANES)
        stl = slice(c * S5_STATE_LANES, (c + 1) * S5_STATE_LANES)
        ub = u_buf[:, ch].astype(BF16)
        xr_buf[...] = _dot(ub, bre_ref[c])
        xi_buf[...] = _dot(ub, bim_ref[c])
        lr = jnp.broadcast_to(lre_ref[:, stl], (BATCH, S5_STATE_LANES))
        li = jnp.broadcast_to(lim_ref[:, stl], (BATCH, S5_STATE_LANES))

        def step(s, carry):
            xr, xi = carry
            sl = pl.ds(pl.multiple_of(s * BATCH, BATCH), BATCH)
            nr = lr * xr - li * xi + xr_buf[sl, :]
            ni = lr * xi + li * xr + xi_buf[sl, :]
            xr_buf[sl, :] = nr
            xi_buf[sl, :] = ni
            return nr, ni

        xr, xi = lax.fori_loop(0, S5_CHUNK, step, (st_re[:, stl], st_im[:, stl]), unroll=4)
        st_re[:, stl] = xr
        st_im[:, stl] = xi
        yc = (_dot(xr_buf[...].astype(BF16), cre_ref[c]) - _dot(xi_buf[...].astype(BF16), cim_ref[c]))
        y_buf[:, ch] = yc + d_ref[:, ch] * u_buf[:, ch]

    y = jax.nn.gelu(y_buf[...])
    z = _dot(y.astype(BF16), wglu_ref[...])
    y_buf[...] = y * jax.nn.sigmoid(z)
    for tb in range(S5_CHUNK // S5_SUB):
        back = _dot(pb_ref[...], y_buf[tb * sub_rows:(tb + 1) * sub_rows, :].astype(BF16)).astype(BF16)
        for b in range(BATCH):
            o_ref[b, tb * S5_SUB:(tb + 1) * S5_SUB, :] = back[b * S5_SUB:(b + 1) * S5_SUB]


def _s5_discretize(lam_re, lam_im, log_dt, b_re, b_im, c_re, c_im):
    dt = jnp.exp(log_dt)[:, None]
    mag = jnp.exp(lam_re * dt)
    lb_re, lb_im = mag * jnp.cos(lam_im * dt), mag * jnp.sin(lam_im * dt)
    den = lam_re * lam_re + lam_im * lam_im
    nr, ni = lb_re - 1.0, lb_im
    f_re = (nr * lam_re + ni * lam_im) / den
    f_im = (ni * lam_re - nr * lam_im) / den
    bb_re = f_re[..., None] * b_re - f_im[..., None] * b_im
    bb_im = f_re[..., None] * b_im + f_im[..., None] * b_re
    eye = jnp.eye(S5_GROUPS_PER_CHUNK, dtype=F32)
    gpc, nch = S5_GROUPS_PER_CHUNK, S5_NCHUNK

    def b_blockdiag(b):
        b = b.reshape(nch, gpc, SSM_STATE, SSM_GROUP_CH)
        m = jnp.einsum('cgph,gk->cghkp', b, eye)
        return m.reshape(nch, gpc * SSM_GROUP_CH, gpc * SSM_STATE).astype(BF16)

    def c_blockdiag(c):
        c = c.reshape(nch, gpc, SSM_GROUP_CH, SSM_STATE)
        m = jnp.einsum('cghp,gk->cgpkh', c, eye)
        return m.reshape(nch, gpc * SSM_STATE, gpc * SSM_GROUP_CH).astype(BF16)

    return (b_blockdiag(bb_re), b_blockdiag(bb_im),
            lb_re.reshape(1, SSM_GROUPS * SSM_STATE), lb_im.reshape(1, SSM_GROUPS * SSM_STATE),
            c_blockdiag(c_re), c_blockdiag(c_im))


def _s5_reorder_matrices():
    n = S5_SUB * BATCH
    out_row = jnp.arange(n)
    src = (out_row % BATCH) * S5_SUB + out_row // BATCH
    fwd = (src[:, None] == jnp.arange(n)[None, :]).astype(BF16)
    return fwd, fwd.T


def _s5(u, disc, d_skip, w_glu):
    rows = S5_CHUNK * BATCH
    blk = pl.BlockSpec((BATCH, S5_CHUNK, SSM_WIDTH), lambda i: (0, i, 0))
    args = _s5_reorder_matrices() + tuple(disc) + (d_skip, w_glu)
    return pl.pallas_call(
        _s5_kernel,
        grid=(SEQ // S5_CHUNK,),
        in_specs=[blk] + [_const_spec(a.shape) for a in args],
        out_specs=blk,
        out_shape=jax.ShapeDtypeStruct((BATCH, SEQ, SSM_WIDTH), BF16),
        scratch_shapes=[pltpu.VMEM((BATCH, SSM_GROUPS * SSM_STATE), F32),
                        pltpu.VMEM((BATCH, SSM_GROUPS * SSM_STATE), F32),
                        pltpu.VMEM((rows, SSM_WIDTH), F32),
                        pltpu.VMEM((rows, S5_STATE_LANES), F32),
                        pltpu.VMEM((rows, S5_STATE_LANES), F32),
                        pltpu.VMEM((rows, SSM_WIDTH), F32)],
        compiler_params=_params("arbitrary"),
        name="s5",
    )(u, *args)


def _outproj_kernel(attn_ref, ssm_ref, x_ref, ga_ref, gs_ref, w_ref, g2_ref, whi_ref, wlo_ref, b_ref,
                    xo_ref, rec_ref, bucket_ref):
    a_n = _rms(attn_ref[...], ga_ref[...]).astype(BF16)
    s_n = _rms(ssm_ref[...].astype(F32), gs_ref[...]).astype(BF16)
    y = _dot(a_n, w_ref[0:ATTN_WIDTH, :]) + _dot(s_n, w_ref[ATTN_WIDTH:, :])
    x = x_ref[...] + y
    xo_ref[...] = x
    rec, bucket = _route(_rms(x, g2_ref[...]), whi_ref[...], wlo_ref[...], b_ref[...])
    rec_ref[...] = rec
    bucket_ref[...] = bucket


def _outproj(attn, ssm, x, ga, gs, w, g2, whi, wlo, bias):
    aw = ATTN_WIDTH
    n_s = SEQ // TOK_TILE
    return pl.pallas_call(
        _outproj_kernel,
        grid=(BATCH, n_s),
        in_specs=[_tok_spec(aw), _tok_spec(SSM_WIDTH), _tok_spec(D_MODEL),
                  _const_spec((1, aw)), _const_spec((1, SSM_WIDTH)),
                  _const_spec((aw + SSM_WIDTH, D_MODEL)), _const_spec((1, D_MODEL)),
                  _const_spec((ROUTER_ROWS, D_MODEL)), _const_spec((ROUTER_ROWS, D_MODEL)),
                  _const_spec((ROUTER_ROWS, 1))],
        out_specs=[_tok_spec(D_MODEL),
                   pl.BlockSpec((TOK_TILE, LANES), lambda b, si: (b * n_s + si, 0)),
                   pl.BlockSpec((1, TOK_TILE), lambda b, si: (0, b * n_s + si))],
        out_shape=[jax.ShapeDtypeStruct((BATCH, SEQ, D_MODEL), F32),
                   jax.ShapeDtypeStruct((TOKENS, LANES), F32),
                   jax.ShapeDtypeStruct((1, TOKENS), F32)],
        compiler_params=_params("parallel", "parallel"),
        name="outproj_router",
    )(attn, ssm, x, ga, gs, w, g2, whi, wlo, bias)


ROUTER_ROWS = 32
EXPERT_ROW0 = 8
PAIRS_PER_GROUP = EXPERTS_PER_GROUP * (EXPERTS_PER_GROUP - 1) // 2
N_BUCKETS = MOE_GROUPS * PAIRS_PER_GROUP
REC_BUCKET, REC_W_LO, REC_W_HI = 0, 1, 2
TOKENS = BATCH * SEQ
PLAN_SIDE = 128
MOE_TILE = 256
MOE_NTILES = TOKENS // MOE_TILE + N_BUCKETS
MOE_SLOTS = MOE_NTILES * MOE_TILE
MOE_ROW = D_MODEL + LANES
DISPATCH_TILE = 256
assert PLAN_SIDE * PLAN_SIDE == TOKENS and MOE_NTILES <= PLAN_SIDE


def _route(h, whi, wlo, bias):
    h_hi, h_lo = _split_bf16(h)
    nt = (((1,), (1,)), ((), ()))
    logits = (lax.dot_general(whi, h_hi, nt, preferred_element_type=F32)
              + lax.dot_general(wlo, h_hi, nt, preferred_element_type=F32)
              + lax.dot_general(whi, h_lo, nt, preferred_element_type=F32)
              + bias)
    ng, ne = MOE_GROUPS, EXPERTS_PER_GROUP
    gl = [logits[g:g + 1, :] for g in range(ng)]
    best, grp = gl[0], jnp.zeros_like(gl[0], dtype=jnp.int32)
    for g in range(1, ng):
        better = gl[g] > best
        grp = jnp.where(better, g, grp)
        best = jnp.where(better, gl[g], best)
    g1 = 1.0 / sum(jnp.exp(x - best) for x in gl)
    sel = []
    for e in range(ne):
        acc = jnp.zeros_like(best)
        for g in range(ng):
            r = EXPERT_ROW0 + g * ne + e
            acc = jnp.where(grp == g, logits[r:r + 1, :], acc)
        sel.append(acc)

    def first_argmax(vals):
        bv, bi = vals[0], jnp.zeros_like(grp)
        for e in range(1, ne):
            better = vals[e] > bv
            bi = jnp.where(better, e, bi)
            bv = jnp.where(better, vals[e], bv)
        return bv, bi

    v1, i1 = first_argmax(sel)
    v2, i2 = first_argmax([jnp.where(i1 == e, -jnp.inf, sel[e]) for e in range(ne)])
    e2 = jnp.exp(v2 - v1)
    w1 = g1 / (1.0 + e2)
    w2 = g1 * e2 / (1.0 + e2)
    first_is_low = i1 < i2
    lo = jnp.where(first_is_low, i1, i2)
    hi = jnp.where(first_is_low, i2, i1)
    pair = jnp.where(lo == 0, 0, jnp.where(lo == 1, 3, 5)) + hi - lo - 1
    bucket = (grp * PAIRS_PER_GROUP + pair).astype(F32)
    w_lo = jnp.where(first_is_low, w1, w2)
    w_hi = jnp.where(first_is_low, w2, w1)
    tokens = logits.shape[1]
    rowid = lax.broadcasted_iota(jnp.int32, (LANES, tokens), 0)
    table = jnp.where(rowid == REC_BUCKET, bucket,
                      jnp.where(rowid == REC_W_LO, w_lo, jnp.where(rowid == REC_W_HI, w_hi, 0.0)))
    return table.T, bucket


def _plan_kernel(bucket_ref, pos_ref, tile_bucket_ref):
    n = PLAN_SIDE
    bucket = bucket_ref[...]
    r = lax.broadcasted_iota(jnp.int32, (n, n), 0)
    c = lax.broadcasted_iota(jnp.int32, (n, n), 1)
    before_in_row = (r < c).astype(BF16)
    rows_before = (c < r).astype(BF16)
    ones = jnp.ones((n, n), BF16)
    tile_start = (c * MOE_TILE).astype(F32)
    pos = jnp.zeros((n, n), F32)
    base = jnp.zeros((n, n), F32)
    ended = jnp.zeros((n, n), F32)
    for k in range(N_BUCKETS):
        member = bucket == float(k)
        mb = member.astype(BF16)
        in_row = _dot(mb, before_in_row)
        row_count = _dot(mb, ones).astype(BF16)
        rank = _dot(rows_before, row_count) + in_row
        total = _dot(ones, row_count)
        pos = jnp.where(member, base + rank, pos)
        base = base + jnp.ceil(total * (1.0 / MOE_TILE)) * MOE_TILE
        ended = ended + (tile_start >= base).astype(F32)
    pos_ref[...] = pos.astype(jnp.int32)
    tile_bucket_ref[...] = ended.astype(jnp.int32)


def _plan(bucket):
    n = PLAN_SIDE
    pos, tile_bucket = pl.pallas_call(
        _plan_kernel,
        out_shape=[jax.ShapeDtypeStruct((n, n), jnp.int32)] * 2,
        name="moe_plan",
    )(bucket.reshape(n, n))
    return pos.reshape(TOKENS), tile_bucket[0, :MOE_NTILES]


def _row_copy(src, src_row, dst, dst_row, sem):
    return pltpu.make_async_copy(src.at[pl.ds(src_row, 1), :], dst.at[pl.ds(dst_row, 1), :], sem)


def _dispatch_kernel(pos_ref, fill_ref, x_ref, g_ref, rec_ref, xs_ref, buf, sems, fill_sem):
    i = pl.program_id(0)
    last = pl.num_programs(0) - 1
    slot = i % 2

    def wait_rows(s):
        pltpu.make_async_copy(buf.at[s], xs_ref.at[pl.ds(0, DISPATCH_TILE), :], sems.at[s]).wait()

    @pl.when(i == 0)
    def _():
        buf[1] = jnp.zeros((DISPATCH_TILE, MOE_ROW), F32)

        def tile_copy(t):
            rows = pl.ds(pl.multiple_of(t * MOE_TILE, MOE_TILE), MOE_TILE)
            return pltpu.make_async_copy(buf.at[1], xs_ref.at[rows, :], fill_sem)

        def start(t, carry):
            @pl.when(fill_ref[t] != 0)
            def _():
                tile_copy(t).start()
            return carry

        def wait(t, carry):
            @pl.when(fill_ref[t] != 0)
            def _():
                tile_copy(t).wait()
            return carry

        lax.fori_loop(0, MOE_NTILES, start, 0)
        lax.fori_loop(0, MOE_NTILES, wait, 0)

    @pl.when(i >= 2)
    def _():
        wait_rows(slot)

    buf[slot, :, 0:D_MODEL] = _rms(x_ref[...], g_ref[...])
    buf[slot, :, D_MODEL:] = rec_ref[...]

    def send(r, carry):
        _row_copy(buf.at[slot], r, xs_ref, pos_ref[i * DISPATCH_TILE + r], sems.at[slot]).start()
        return carry

    lax.fori_loop(0, DISPATCH_TILE, send, 0, unroll=8)

    @pl.when(i == last)
    def _():
        wait_rows(1 - slot)
        wait_rows(slot)


def _dispatch(pos, fill, x, g, rec):
    assert DISPATCH_TILE == MOE_TILE
    grid_spec = pltpu.PrefetchScalarGridSpec(
        num_scalar_prefetch=2,
        grid=(TOKENS // DISPATCH_TILE,),
        in_specs=[pl.BlockSpec((DISPATCH_TILE, D_MODEL), lambda i, pos, fill: (i, 0)),
                  pl.BlockSpec((1, D_MODEL), lambda i, pos, fill: (0, 0)),
                  pl.BlockSpec((DISPATCH_TILE, LANES), lambda i, pos, fill: (i, 0))],
        out_specs=pl.BlockSpec(memory_space=pl.ANY),
        scratch_shapes=[pltpu.VMEM((2, DISPATCH_TILE, MOE_ROW), F32),
                        pltpu.SemaphoreType.DMA((2,)),
                        pltpu.SemaphoreType.DMA(())])
    return pl.pallas_call(
        _dispatch_kernel,
        grid_spec=grid_spec,
        out_shape=jax.ShapeDtypeStruct((MOE_SLOTS, MOE_ROW), F32),
        compiler_params=_params("arbitrary"),
        name="moe_dispatch",
    )(pos, fill, x, g, rec)


def _experts_kernel(e_lo_ref, e_hi_ref, used_ref, xs_ref, wg_lo, wu_lo, wd_lo, wg_hi, wu_hi, wd_hi, ys_ref):
    i = pl.program_id(0)

    @pl.when(used_ref[i] != 0)
    def _():
        h = xs_ref[:, 0:D_MODEL].astype(BF16)
        rec = xs_ref[:, D_MODEL:]

        def expert(wg, wu, wd, lane):
            hg = _dot(h, wg[...].astype(BF16))
            hu = _dot(h, wu[...].astype(BF16))
            act = jax.nn.silu(hg) * hu * rec[:, lane:lane + 1]
            return _dot(act.astype(BF16), wd[...].astype(BF16))

        ys_ref[...] = expert(wg_lo, wu_lo, wd_lo, REC_W_LO) + expert(wg_hi, wu_hi, wd_hi, REC_W_HI)

    @pl.when(used_ref[i] == 0)
    def _():
        ys_ref[...] = jnp.zeros_like(ys_ref)


def _experts(layer, e_lo, e_hi, used, xs, w_gate, w_up, w_down):
    up = lambda pick: pl.BlockSpec((None, None, D_MODEL, EXPERT_FF),
                                   lambda i, lo, hi, u: (layer, pick(lo, hi)[i], 0, 0))
    down = lambda pick: pl.BlockSpec((None, None, EXPERT_FF, D_MODEL),
                                     lambda i, lo, hi, u: (layer, pick(lo, hi)[i], 0, 0))
    first, second = (lambda lo, hi: lo), (lambda lo, hi: hi)
    grid_spec = pltpu.PrefetchScalarGridSpec(
        num_scalar_prefetch=3,
        grid=(MOE_NTILES,),
        in_specs=[pl.BlockSpec((MOE_TILE, MOE_ROW), lambda i, lo, hi, u: (i, 0)),
                  up(first), up(first), down(first), up(second), up(second), down(second)],
        out_specs=pl.BlockSpec((MOE_TILE, D_MODEL), lambda i, lo, hi, u: (i, 0)))
    return pl.pallas_call(
        _experts_kernel,
        grid_spec=grid_spec,
        out_shape=jax.ShapeDtypeStruct((MOE_SLOTS, D_MODEL), F32),
        compiler_params=_params("arbitrary"),
        name="moe_experts",
    )(e_lo, e_hi, used, xs, w_gate, w_up, w_down, w_gate, w_up, w_down)


def _combine_kernel(pos_ref, x_ref, ys_ref, fg_ref, xo_ref, buf, sems, *, final_norm):
    i = pl.program_id(0)
    n = pl.num_programs(0)
    slot = i % 2

    def fetch(step, s):
        def body(r, carry):
            _row_copy(ys_ref, pos_ref[step * DISPATCH_TILE + r], buf.at[s], r, sems.at[s]).start()
            return carry
        lax.fori_loop(0, DISPATCH_TILE, body, 0, unroll=8)

    @pl.when(i == 0)
    def _():
        fetch(0, 0)

    @pl.when(i + 1 < n)
    def _():
        fetch(i + 1, 1 - slot)

    pltpu.make_async_copy(ys_ref.at[pl.ds(0, DISPATCH_TILE), :], buf.at[slot], sems.at[slot]).wait()
    x = x_ref[...] + buf[slot]
    xo_ref[...] = _rms(x, fg_ref[...]) if final_norm else x


def _combine(pos, x, ys, final_g=None):
    final_norm = final_g is not None
    if not final_norm:
        final_g = jnp.ones((1, D_MODEL), F32)
    grid_spec = pltpu.PrefetchScalarGridSpec(
        num_scalar_prefetch=1,
        grid=(TOKENS // DISPATCH_TILE,),
        in_specs=[pl.BlockSpec((DISPATCH_TILE, D_MODEL), lambda i, pos: (i, 0)),
                  pl.BlockSpec(memory_space=pl.ANY),
                  pl.BlockSpec((1, D_MODEL), lambda i, pos: (0, 0))],
        out_specs=pl.BlockSpec((DISPATCH_TILE, D_MODEL), lambda i, pos: (i, 0)),
        scratch_shapes=[pltpu.VMEM((2, DISPATCH_TILE, D_MODEL), F32),
                        pltpu.SemaphoreType.DMA((2,))])
    return pl.pallas_call(
        functools.partial(_combine_kernel, final_norm=final_norm),
        grid_spec=grid_spec,
        out_shape=jax.ShapeDtypeStruct((TOKENS, D_MODEL), F32),
        compiler_params=_params("arbitrary"),
        name="moe_combine_final" if final_norm else "moe_combine",
    )(pos, x, ys, final_g)


def _bucket_experts():
    pairs = [(a, b) for a in range(EXPERTS_PER_GROUP) for b in range(a + 1, EXPERTS_PER_GROUP)]
    lo = [g * EXPERTS_PER_GROUP + a for g in range(MOE_GROUPS) for a, _ in pairs]
    hi = [g * EXPERTS_PER_GROUP + b for g in range(MOE_GROUPS) for _, b in pairs]
    return jnp.array(lo, jnp.int32), jnp.array(hi, jnp.int32)


def _routed_moe(layer, x, rec, bucket, g, w_gate, w_up, w_down, final_g=None):
    pos, tile_bucket = _plan(bucket)
    unused = tile_bucket >= N_BUCKETS
    next_bucket = jnp.concatenate([tile_bucket[1:], jnp.full((1,), N_BUCKETS, jnp.int32)])
    fill = ((tile_bucket != next_bucket) | unused).astype(jnp.int32)
    tile_bucket = jnp.minimum(tile_bucket, N_BUCKETS - 1)
    lo, hi = _bucket_experts()
    x_rows = x.reshape(TOKENS, D_MODEL)
    xs = _dispatch(pos, fill, x_rows, g, rec)
    ys = _experts(layer, lo[tile_bucket], hi[tile_bucket], jnp.logical_not(unused).astype(jnp.int32),
                  xs, w_gate, w_up, w_down)
    return _combine(pos, x_rows, ys, final_g).reshape(BATCH, SEQ, D_MODEL)


def kernel(x, ln1_g, w_in, lam_re, lam_im, log_dt, b_re, b_im, c_re, c_im, d_skip, w_glu,
           gn_attn, gn_ssm, w_out, ln2_g, w_router_grp, b_router_grp, w_router_exp,
           b_router_exp, w_gate, w_up, w_down, final_g):
    assert x.shape == (BATCH, SEQ, D_MODEL)
    ng = MOE_GROUPS

    for l in range(DEPTH):
        q, k, v, u = _inproj(x, ln1_g[l][None, :], w_in[l].astype(BF16))
        attn = _attention(q, k, v)
        disc = _s5_discretize(lam_re[l], lam_im[l], log_dt[l], b_re[l], b_im[l], c_re[l], c_im[l])
        ssm = _s5(u, disc, d_skip[l].reshape(1, SSM_WIDTH), w_glu[l].astype(BF16))

        w_r = jnp.zeros((ROUTER_ROWS, D_MODEL), F32)
        w_r = w_r.at[0:ng].set(w_router_grp[l].T)
        w_r = w_r.at[EXPERT_ROW0:EXPERT_ROW0 + N_EXPERTS].set(
            jnp.transpose(w_router_exp[l], (0, 2, 1)).reshape(N_EXPERTS, D_MODEL))
        b_r = jnp.zeros((ROUTER_ROWS, 1), F32)
        b_r = b_r.at[0:ng, 0].set(b_router_grp[l])
        b_r = b_r.at[EXPERT_ROW0:EXPERT_ROW0 + N_EXPERTS, 0].set(b_router_exp[l].reshape(N_EXPERTS))
        w_r_hi, w_r_lo = _split_bf16(w_r)
        g2 = ln2_g[l][None, :]
        x, rec, bucket = _outproj(attn, ssm, x, gn_attn[l][None, :], gn_ssm[l][None, :],
                                  w_out[l].astype(BF16), g2, w_r_hi, w_r_lo, b_r)
        x = _routed_moe(l, x, rec, bucket, g2, w_gate, w_up, w_down,
                        final_g[None, :] if l == DEPTH - 1 else None)
    return x
```

```python
import functools
import math

import jax
import jax.numpy as jnp
from jax import lax
from jax.experimental import pallas as pl
from jax.experimental.pallas import tpu as pltpu

D_MODEL = 1024
BATCH = 8
SEQ = 2048
DEPTH = 4
ATTN_WIDTH = 512
HEAD_DIM = 64
ATTN_HEADS = 8
DILATIONS = (1, 4, 16)
SPAN = 128
SSM_WIDTH = 512
SSM_GROUP_CH = 16
SSM_GROUPS = 32
SSM_STATE = 64
MOE_GROUPS = 4
EXPERTS_PER_GROUP = 4
N_EXPERTS = 16
EXPERT_FF = 256
RMS_EPS = 1e-6

LANES = 128
BF16_ROWS = 16
VMEM_LIMIT_BYTES = 56 * 1024 * 1024

TOK_TILE = 512
PERM_TILE = 256
S5_CHUNK = 128
S5_GROUPS_PER_CHUNK = 8
S5_NCHUNK = SSM_GROUPS // S5_GROUPS_PER_CHUNK
S5_STATE_LANES = S5_GROUPS_PER_CHUNK * SSM_STATE
S5_SUB = BF16_ROWS

F32 = jnp.float32
BF16 = jnp.bfloat16


def _params(*sem):
    return pltpu.CompilerParams(dimension_semantics=sem, vmem_limit_bytes=VMEM_LIMIT_BYTES)


def _rms(x, g):
    return x * lax.rsqrt(jnp.mean(x * x, axis=-1, keepdims=True) + RMS_EPS) * g


def _split_bf16(a):
    hi = a.astype(BF16)
    lo = (a - hi.astype(F32)).astype(BF16)
    return hi, lo


def _dot(a, b):
    return jnp.dot(a, b, preferred_element_type=F32)


def _dot_hilo(p, a):
    hi, lo = _split_bf16(a)
    return _dot(p, hi) + _dot(p, lo)


def _tok_spec(width):
    return pl.BlockSpec((TOK_TILE, width), lambda i: (i, 0))


def _const_spec(shape):
    return pl.BlockSpec(shape, lambda *_: (0,) * len(shape))


def _inproj_kernel(x_ref, g_ref, w_ref, q_ref, k_ref, v_ref, u_ref):
    h = _rms(x_ref[...], g_ref[...]).astype(BF16)
    p = _dot(h, w_ref[...])
    aw = ATTN_WIDTH
    q_ref[...] = (p[:, :aw] * (1.0 / math.sqrt(HEAD_DIM))).astype(BF16)
    k_ref[...] = p[:, aw:2 * aw].astype(BF16)
    v_ref[...] = p[:, 2 * aw:3 * aw].astype(BF16)
    u_ref[...] = p[:, 3 * aw:].astype(BF16)


def _inproj(x, g, w):
    aw = ATTN_WIDTH
    out = jax.ShapeDtypeStruct((TOKENS, aw), BF16)
    outs = pl.pallas_call(
        _inproj_kernel,
        grid=(TOKENS // TOK_TILE,),
        in_specs=[_tok_spec(D_MODEL), _const_spec((1, D_MODEL)),
                  _const_spec((D_MODEL, 3 * aw + SSM_WIDTH))],
        out_specs=[_tok_spec(aw)] * 4,
        out_shape=[out] * 4,
        compiler_params=_params("parallel"),
        name="inproj",
    )(x, g, w)
    return [o.reshape(BATCH, SEQ, aw) for o in outs]


def _attn_block(q, kk, vv, bias, low_half, lane):
    heads = range(ATTN_HEADS)
    nt = (((1,), (1,)), ((), ()))
    scores = []
    for h in heads:
        sl = slice((h // 2) * LANES, (h // 2 + 1) * LANES)
        keep = low_half if h % 2 == 0 else jnp.logical_not(low_half)
        qm = jnp.where(keep, q[:, sl], jnp.zeros_like(q[:, sl]))
        scores.append(lax.dot_general(qm, kk[:, sl], nt, preferred_element_type=F32) + bias)
    probs, dens, lses = [], [], []
    for h in heads:
        m = jnp.max(scores[h], axis=-1, keepdims=True)
        p = jnp.exp(scores[h] - m)
        den = jnp.sum(p, axis=-1, keepdims=True)
        probs.append(p.astype(BF16))
        dens.append(den)
        lses.append(m + jnp.log(den))
    pv = [_dot(probs[h], vv[:, (h // 2) * LANES:(h // 2 + 1) * LANES]) / dens[h] for h in heads]
    lse_tile = jnp.zeros((SPAN, LANES), F32)
    for h in heads:
        lse_tile = jnp.where(lane == h, lses[h], lse_tile)
    outs = [jnp.where(low_half, pv[2 * j], pv[2 * j + 1]) for j in range(ATTN_HEADS // 2)]
    return jnp.concatenate(outs, axis=-1), lse_tile


def _attn_kernel(q_ref, k_ref, v_ref, p4_ref, p4t_ref, p16_ref, p16t_ref, e_ref, o_ref,
                 qkv4, qkv16, lse1, o4p, lse4p, o16p, lse16p):
    lane = lax.broadcasted_iota(jnp.int32, (SPAN, LANES), 1)
    low_half = lane < HEAD_DIM
    row0 = lax.broadcasted_iota(jnp.int32, (SPAN, SPAN), 0)
    col0 = lax.broadcasted_iota(jnp.int32, (SPAN, SPAN), 1)
    bias_first = jnp.where(col0 <= row0, 0.0, -jnp.inf).astype(F32)
    row = lax.broadcasted_iota(jnp.int32, (SPAN, 2 * SPAN), 0)
    col = lax.broadcasted_iota(jnp.int32, (SPAN, 2 * SPAN), 1)
    bias_band = jnp.where((col >= row) & (col <= row + SPAN), 0.0, -jnp.inf).astype(F32)
    block = functools.partial(_attn_block, low_half=low_half, lane=lane)
    srcs = (q_ref, k_ref, v_ref)

    def permute(i, carry):
        rows = pl.ds(pl.multiple_of(i * PERM_TILE, PERM_TILE), PERM_TILE)
        for a, src in enumerate(srcs):
            x = src[rows, :]
            y4 = _dot(p4_ref[...], x).astype(BF16)
            y16 = _dot(p16_ref[...], x).astype(BF16)
            n4, n16 = PERM_TILE // 4, PERM_TILE // 16
            for r in range(4):
                qkv4[a, r, pl.ds(pl.multiple_of(i * n4, n4), n4), :] = y4[r * n4:(r + 1) * n4]
            for r in range(16):
                qkv16[a, r, pl.ds(pl.multiple_of(i * n16, n16), n16), :] = y16[r * n16:(r + 1) * n16]
        return carry

    lax.fori_loop(0, SEQ // PERM_TILE, permute, 0)

    def branch(get, put, nblocks):
        head = pl.ds(0, SPAN)
        o, lse = block(get(0, head), get(1, head), get(2, head), bias_first)
        put(head, o, lse)
        if nblocks > 1:
            def body(n, carry):
                qs = pl.ds(pl.multiple_of(n * SPAN, SPAN), SPAN)
                ks = pl.ds(pl.multiple_of((n - 1) * SPAN, SPAN), 2 * SPAN)
                o, lse = block(get(0, qs), get(1, ks), get(2, ks), bias_band)
                put(qs, o, lse)
                return carry
            lax.fori_loop(1, nblocks, body, 0)

    def put1(rows, o, lse):
        o_ref[rows, :] = o
        lse1[rows, :] = lse

    branch(lambda a, rows: srcs[a][rows, :], put1, SEQ // SPAN)

    def class4(r, carry):
        def put(rows, o, lse):
            o4p[r, rows, :] = o.astype(BF16)
            lse4p[r, rows, :] = lse
        branch(lambda a, rows: qkv4[a, r, rows, :], put, SEQ // 4 // SPAN)
        return carry

    lax.fori_loop(0, 4, class4, 0)

    def class16(r, carry):
        def put(rows, o, lse):
            o16p[r, rows, :] = o.astype(BF16)
            lse16p[r, rows, :] = lse
        branch(lambda a, rows: qkv16[a, r, rows, :], put, SEQ // 16 // SPAN)
        return carry

    lax.fori_loop(0, 16, class16, 0)

    def expand(w):
        hi, lo = _split_bf16(w)
        return _dot(hi, e_ref[...]) + _dot(lo, e_ref[...])

    def combine(i, carry):
        rows = pl.ds(pl.multiple_of(i * PERM_TILE, PERM_TILE), PERM_TILE)
        n4, n16 = PERM_TILE // 4, PERM_TILE // 16
        r4 = pl.ds(pl.multiple_of(i * n4, n4), n4)
        r16 = pl.ds(pl.multiple_of(i * n16, n16), n16)
        o4 = _dot(p4t_ref[...], jnp.concatenate([o4p[r, r4, :] for r in range(4)], axis=0))
        l4 = _dot_hilo(p4t_ref[...], jnp.concatenate([lse4p[r, r4, :] for r in range(4)], axis=0))
        o16 = _dot(p16t_ref[...], jnp.concatenate([o16p[r, r16, :] for r in range(16)], axis=0))
        l16 = _dot_hilo(p16t_ref[...], jnp.concatenate([lse16p[r, r16, :] for r in range(16)], axis=0))
        l1 = lse1[rows, :]
        m = jnp.maximum(jnp.maximum(l1, l4), l16)
        e1, e4, e16 = jnp.exp(l1 - m), jnp.exp(l4 - m), jnp.exp(l16 - m)
        den = e1 + e4 + e16
        o_ref[rows, :] = (expand(e1 / den) * o_ref[rows, :] + expand(e4 / den) * o4
                          + expand(e16 / den) * o16)
        return carry

    lax.fori_loop(0, SEQ // PERM_TILE, combine, 0)


def _perm_matrix(dil):
    n = PERM_TILE // dil
    out_row = jnp.arange(PERM_TILE)
    src = dil * (out_row % n) + out_row // n
    return (src[:, None] == jnp.arange(PERM_TILE)[None, :]).astype(BF16)


def _head_expand_matrix():
    r = jnp.arange(LANES)[:, None]
    c = jnp.arange(ATTN_WIDTH)[None, :] // HEAD_DIM
    return (r == c).astype(BF16)


def _attention(q, k, v):
    aw = ATTN_WIDTH
    seq_spec = pl.BlockSpec((None, SEQ, aw), lambda b: (b, 0, 0))
    p4, p16 = _perm_matrix(4), _perm_matrix(16)
    consts = (p4, p4.T, p16, p16.T, _head_expand_matrix())
    return pl.pallas_call(
        _attn_kernel,
        grid=(BATCH,),
        in_specs=[seq_spec] * 3 + [_const_spec(c.shape) for c in consts],
        out_specs=seq_spec,
        out_shape=jax.ShapeDtypeStruct((BATCH, SEQ, aw), F32),
        scratch_shapes=[pltpu.VMEM((3, 4, SEQ // 4, aw), BF16),
                        pltpu.VMEM((3, 16, SEQ // 16, aw), BF16),
                        pltpu.VMEM((SEQ, LANES), F32),
                        pltpu.VMEM((4, SEQ // 4, aw), BF16),
                        pltpu.VMEM((4, SEQ // 4, LANES), F32),
                        pltpu.VMEM((16, SEQ // 16, aw), BF16),
                        pltpu.VMEM((16, SEQ // 16, LANES), F32)],
        compiler_params=_params("parallel"),
        name="attention",
    )(q, k, v, *consts)


def _s5_kernel(u_ref, pf_ref, pb_ref, bre_ref, bim_ref, lre_ref, lim_ref, cre_ref, cim_ref, d_ref,
               wglu_ref, o_ref, st_re, st_im, u_buf, xr_buf, xi_buf, y_buf):
    rows = S5_CHUNK * BATCH
    sub_rows = S5_SUB * BATCH

    @pl.when(pl.program_id(0) == 0)
    def _():
        st_re[...] = jnp.zeros_like(st_re)
        st_im[...] = jnp.zeros_like(st_im)

    for tb in range(S5_CHUNK // S5_SUB):
        t = slice(tb * S5_SUB, (tb + 1) * S5_SUB)
        piece = jnp.concatenate([u_ref[b, t, :] for b in range(BATCH)], axis=0)
        u_buf[tb * sub_rows:(tb + 1) * sub_rows, :] = _dot(pf_ref[...], piece)

    for c in range(S5_NCHUNK):
        ch = slice(c * LANES, (c + 1) * LANES)
        stl = slice(c * S5_STATE_LANES, (c + 1) * S5_STATE_LANES)
        ub = u_buf[:, ch].astype(BF16)
        xr_buf[...] = _dot(ub, bre_ref[c])
        xi_buf[...] = _dot(ub, bim_ref[c])
        lr = jnp.broadcast_to(lre_ref[:, stl], (BATCH, S5_STATE_LANES))
        li = jnp.broadcast_to(lim_ref[:, stl], (BATCH, S5_STATE_LANES))

        def step(s, carry):
            xr, xi = carry
            sl = pl.ds(pl.multiple_of(s * BATCH, BATCH), BATCH)
            nr = lr * xr - li * xi + xr_buf[sl, :]
            ni = lr * xi + li * xr + xi_buf[sl, :]
            xr_buf[sl, :] = nr
            xi_buf[sl, :] = ni
            return nr, ni

        xr, xi = lax.fori_loop(0, S5_CHUNK, step, (st_re[:, stl], st_im[:, stl]), unroll=4)
        st_re[:, stl] = xr
        st_im[:, stl] = xi
        yc = (_dot(xr_buf[...].astype(BF16), cre_ref[c]) - _dot(xi_buf[...].astype(BF16), cim_ref[c]))
        y_buf[:, ch] = yc + d_ref[:, ch] * u_buf[:, ch]

    y = jax.nn.gelu(y_buf[...])
    z = _dot(y.astype(BF16), wglu_ref[...])
    y_buf[...] = y * jax.nn.sigmoid(z)
    for tb in range(S5_CHUNK // S5_SUB):
        back = _dot(pb_ref[...], y_buf[tb * sub_rows:(tb + 1) * sub_rows, :].astype(BF16)).astype(BF16)
        for b in range(BATCH):
            o_ref[b, tb * S5_SUB:(tb + 1) * S5_SUB, :] = back[b * S5_SUB:(b + 1) * S5_SUB]


def _s5_discretize(lam_re, lam_im, log_dt, b_re, b_im, c_re, c_im):
    dt = jnp.exp(log_dt)[:, None]
    mag = jnp.exp(lam_re * dt)
    lb_re, lb_im = mag * jnp.cos(lam_im * dt), mag * jnp.sin(lam_im * dt)
    den = lam_re * lam_re + lam_im * lam_im
    nr, ni = lb_re - 1.0, lb_im
    f_re = (nr * lam_re + ni * lam_im) / den
    f_im = (ni * lam_re - nr * lam_im) / den
    bb_re = f_re[..., None] * b_re - f_im[..., None] * b_im
    bb_im = f_re[..., None] * b_im + f_im[..., None] * b_re
    eye = jnp.eye(S5_GROUPS_PER_CHUNK, dtype=F32)
    gpc, nch = S5_GROUPS_PER_CHUNK, S5_NCHUNK

    def b_blockdiag(b):
        b = b.reshape(nch, gpc, SSM_STATE, SSM_GROUP_CH)
        m = jnp.einsum('cgph,gk->cghkp', b, eye)
        return m.reshape(nch, gpc * SSM_GROUP_CH, gpc * SSM_STATE).astype(BF16)

    def c_blockdiag(c):
        c = c.reshape(nch, gpc, SSM_GROUP_CH, SSM_STATE)
        m = jnp.einsum('cghp,gk->cgpkh', c, eye)
        return m.reshape(nch, gpc * SSM_STATE, gpc * SSM_GROUP_CH).astype(BF16)

    return (b_blockdiag(bb_re), b_blockdiag(bb_im),
            lb_re.reshape(1, SSM_GROUPS * SSM_STATE), lb_im.reshape(1, SSM_GROUPS * SSM_STATE),
            c_blockdiag(c_re), c_blockdiag(c_im))


def _s5_reorder_matrices():
    n = S5_SUB * BATCH
    out_row = jnp.arange(n)
    src = (out_row % BATCH) * S5_SUB + out_row // BATCH
    fwd = (src[:, None] == jnp.arange(n)[None, :]).astype(BF16)
    return fwd, fwd.T


def _s5(u, disc, d_skip, w_glu):
    rows = S5_CHUNK * BATCH
    blk = pl.BlockSpec((BATCH, S5_CHUNK, SSM_WIDTH), lambda i: (0, i, 0))
    args = _s5_reorder_matrices() + tuple(disc) + (d_skip, w_glu)
    return pl.pallas_call(
        _s5_kernel,
        grid=(SEQ // S5_CHUNK,),
        in_specs=[blk] + [_const_spec(a.shape) for a in args],
        out_specs=blk,
        out_shape=jax.ShapeDtypeStruct((BATCH, SEQ, SSM_WIDTH), BF16),
        scratch_shapes=[pltpu.VMEM((BATCH, SSM_GROUPS * SSM_STATE), F32),
                        pltpu.VMEM((BATCH, SSM_GROUPS * SSM_STATE), F32),
                        pltpu.VMEM((rows, SSM_WIDTH), F32),
                        pltpu.VMEM((rows, S5_STATE_LANES), F32),
                        pltpu.VMEM((rows, S5_STATE_LANES), F32),
                        pltpu.VMEM((rows, SSM_WIDTH), F32)],
        compiler_params=_params("arbitrary"),
        name="s5",
    )(u, *args)


def _outproj_kernel(attn_ref, ssm_ref, x_ref, ga_ref, gs_ref, w_ref, g2_ref, whi_ref, wlo_ref, b_ref,
                    xa_ref, bucket_ref):
    a_n = _rms(attn_ref[...], ga_ref[...]).astype(BF16)
    s_n = _rms(ssm_ref[...].astype(F32), gs_ref[...]).astype(BF16)
    y = _dot(a_n, w_ref[0:ATTN_WIDTH, :]) + _dot(s_n, w_ref[ATTN_WIDTH:, :])
    x = x_ref[...] + y
    rec, bucket = _route(_rms(x, g2_ref[...]), whi_ref[...], wlo_ref[...], b_ref[...])
    xa_ref[:, 0:D_MODEL] = x
    xa_ref[:, D_MODEL:] = rec
    bucket_ref[...] = bucket


def _outproj(attn, ssm, x, ga, gs, w, g2, whi, wlo, bias):
    aw = ATTN_WIDTH
    return pl.pallas_call(
        _outproj_kernel,
        grid=(TOKENS // TOK_TILE,),
        in_specs=[_tok_spec(aw), _tok_spec(SSM_WIDTH), _tok_spec(D_MODEL),
                  _const_spec((1, aw)), _const_spec((1, SSM_WIDTH)),
                  _const_spec((aw + SSM_WIDTH, D_MODEL)), _const_spec((1, D_MODEL)),
                  _const_spec((ROUTER_ROWS, D_MODEL)), _const_spec((ROUTER_ROWS, D_MODEL)),
                  _const_spec((ROUTER_ROWS, 1))],
        out_specs=[_tok_spec(MOE_ROW), pl.BlockSpec((1, TOK_TILE), lambda i: (0, i))],
        out_shape=[jax.ShapeDtypeStruct((TOKENS, MOE_ROW), F32),
                   jax.ShapeDtypeStruct((1, TOKENS), F32)],
        compiler_params=_params("parallel"),
        name="outproj_router",
    )(attn, ssm, x, ga, gs, w, g2, whi, wlo, bias)


ROUTER_ROWS = 32
EXPERT_ROW0 = 8
PAIRS_PER_GROUP = EXPERTS_PER_GROUP * (EXPERTS_PER_GROUP - 1) // 2
N_BUCKETS = MOE_GROUPS * PAIRS_PER_GROUP
REC_BUCKET, REC_W_LO, REC_W_HI = 0, 1, 2
TOKENS = BATCH * SEQ
PLAN_SIDE = 128
MOE_TILE = 256
MOE_NTILES = TOKENS // MOE_TILE + N_BUCKETS
MOE_SLOTS = MOE_NTILES * MOE_TILE
MOE_ROW = D_MODEL + LANES
DUMP_ROWS = 2 * MOE_TILE
X_ROWS = TOKENS + DUMP_ROWS
SPARE_TILE = MOE_NTILES + 1
INV_LEN = (MOE_NTILES + 2) * MOE_TILE
assert PLAN_SIDE * PLAN_SIDE == TOKENS and MOE_NTILES <= PLAN_SIDE
assert DUMP_ROWS & (DUMP_ROWS - 1) == 0


def _route(h, whi, wlo, bias):
    h_hi, h_lo = _split_bf16(h)
    nt = (((1,), (1,)), ((), ()))
    logits = (lax.dot_general(whi, h_hi, nt, preferred_element_type=F32)
              + lax.dot_general(wlo, h_hi, nt, preferred_element_type=F32)
              + lax.dot_general(whi, h_lo, nt, preferred_element_type=F32)
              + bias)
    ng, ne = MOE_GROUPS, EXPERTS_PER_GROUP
    gl = [logits[g:g + 1, :] for g in range(ng)]
    best, grp = gl[0], jnp.zeros_like(gl[0], dtype=jnp.int32)
    for g in range(1, ng):
        better = gl[g] > best
        grp = jnp.where(better, g, grp)
        best = jnp.where(better, gl[g], best)
    g1 = 1.0 / sum(jnp.exp(x - best) for x in gl)
    sel = []
    for e in range(ne):
        acc = jnp.zeros_like(best)
        for g in range(ng):
            r = EXPERT_ROW0 + g * ne + e
            acc = jnp.where(grp == g, logits[r:r + 1, :], acc)
        sel.append(acc)

    def first_argmax(vals):
        bv, bi = vals[0], jnp.zeros_like(grp)
        for e in range(1, ne):
            better = vals[e] > bv
            bi = jnp.where(better, e, bi)
            bv = jnp.where(better, vals[e], bv)
        return bv, bi

    v1, i1 = first_argmax(sel)
    v2, i2 = first_argmax([jnp.where(i1 == e, -jnp.inf, sel[e]) for e in range(ne)])
    e2 = jnp.exp(v2 - v1)
    w1 = g1 / (1.0 + e2)
    w2 = g1 * e2 / (1.0 + e2)
    first_is_low = i1 < i2
    lo = jnp.where(first_is_low, i1, i2)
    hi = jnp.where(first_is_low, i2, i1)
    pair = jnp.where(lo == 0, 0, jnp.where(lo == 1, 3, 5)) + hi - lo - 1
    bucket = (grp * PAIRS_PER_GROUP + pair).astype(F32)
    w_lo = jnp.where(first_is_low, w1, w2)
    w_hi = jnp.where(first_is_low, w2, w1)
    tokens = logits.shape[1]
    rowid = lax.broadcasted_iota(jnp.int32, (LANES, tokens), 0)
    table = jnp.where(rowid == REC_BUCKET, bucket,
                      jnp.where(rowid == REC_W_LO, w_lo, jnp.where(rowid == REC_W_HI, w_hi, 0.0)))
    return table.T, bucket


def _plan_kernel(bucket_ref, pos_ref, tile_bucket_ref):
    n = PLAN_SIDE
    bucket = bucket_ref[...]
    r = lax.broadcasted_iota(jnp.int32, (n, n), 0)
    c = lax.broadcasted_iota(jnp.int32, (n, n), 1)
    before_in_row = (r < c).astype(BF16)
    rows_before = (c < r).astype(BF16)
    ones = jnp.ones((n, n), BF16)
    tile_start = (c * MOE_TILE).astype(F32)
    pos = jnp.zeros((n, n), F32)
    base = jnp.zeros((n, n), F32)
    ended = jnp.zeros((n, n), F32)
    for k in range(N_BUCKETS):
        member = bucket == float(k)
        mb = member.astype(BF16)
        in_row = _dot(mb, before_in_row)
        row_count = _dot(mb, ones).astype(BF16)
        rank = _dot(rows_before, row_count) + in_row
        total = _dot(ones, row_count)
        pos = jnp.where(member, base + rank, pos)
        base = base + jnp.ceil(total * (1.0 / MOE_TILE)) * MOE_TILE
        ended = ended + (tile_start >= base).astype(F32)
    pos_ref[...] = pos.astype(jnp.int32)
    tile_bucket_ref[...] = ended.astype(jnp.int32)


def _plan(bucket):
    n = PLAN_SIDE
    pos, tile_bucket = pl.pallas_call(
        _plan_kernel,
        out_shape=[jax.ShapeDtypeStruct((n, n), jnp.int32)] * 2,
        name="moe_plan",
    )(bucket.reshape(n, n))
    return pos.reshape(TOKENS), tile_bucket[0, :MOE_NTILES]


def _invert_kernel(pos_ref, inv_ref):
    def init(blk, carry):
        for r in range(DUMP_ROWS):
            inv_ref[blk * DUMP_ROWS + r] = TOKENS + r
        return carry

    def place(t, carry):
        inv_ref[pos_ref[t]] = t
        return carry

    lax.fori_loop(0, INV_LEN // DUMP_ROWS, init, 0)
    lax.fori_loop(0, TOKENS, place, 0, unroll=16)


def _invert(pos):
    smem = pl.BlockSpec(memory_space=pltpu.SMEM)
    return pl.pallas_call(
        _invert_kernel,
        in_specs=[smem],
        out_specs=smem,
        out_shape=jax.ShapeDtypeStruct((INV_LEN,), jnp.int32),
        name="moe_invert",
    )(pos)


def _row_copy(src, src_row, dst, dst_row, sem):
    return pltpu.make_async_copy(src.at[pl.ds(src_row, 1), :], dst.at[pl.ds(dst_row, 1), :], sem)


def _experts_kernel(inv_ref, e_lo_ref, e_hi_ref, nused_ref, xa_ref, g_ref, fg_ref,
                    wg_lo, wu_lo, wd_lo, wg_hi, wu_hi, wd_hi, xo_ref,
                    xin, yout, gsem, ssem, fsem, *, final_norm):
    i = pl.program_id(0)
    n_used = nused_ref[0]
    s = i % 2

    def start_gather(tile, slot):
        for r in range(MOE_TILE):
            src = jnp.minimum(inv_ref[tile * MOE_TILE + r], TOKENS - 1)
            _row_copy(xa_ref, src, xin.at[slot], r, gsem.at[slot]).start()

    def start_scatter(tile, slot):
        for r in range(MOE_TILE):
            _row_copy(yout.at[slot], r, xo_ref, inv_ref[tile * MOE_TILE + r], ssem.at[slot]).start()

    def wait_gather(slot):
        pltpu.make_async_copy(xa_ref.at[pl.ds(0, MOE_TILE), :], xin.at[slot], gsem.at[slot]).wait()

    def wait_scatter(slot):
        pltpu.make_async_copy(yout.at[slot], xo_ref.at[pl.ds(0, MOE_TILE), :], ssem.at[slot]).wait()

    @pl.when(i == 0)
    def _():
        yout[1] = jnp.zeros((MOE_TILE, D_MODEL), F32)
        for half in range(DUMP_ROWS // MOE_TILE):
            fill = pltpu.make_async_copy(
                yout.at[1], xo_ref.at[pl.ds(TOKENS + half * MOE_TILE, MOE_TILE), :], fsem)
            fill.start()
            fill.wait()
        start_gather(0, 0)

    @pl.when(i <= n_used)
    def _():
        wait_gather(s)

        @pl.when(i >= 1)
        def _():
            wait_scatter(s)

        start_gather(i + 1, 1 - s)
        start_scatter(jnp.where(i >= 1, i - 1, SPARE_TILE), 1 - s)

        xt = xin[s]
        x_rows = xt[:, 0:D_MODEL]
        rec = xt[:, D_MODEL:]
        h = _rms(x_rows, g_ref[...]).astype(BF16)

        def expert(wg, wu, wd, lane):
            hg = _dot(h, wg[...].astype(BF16))
            hu = _dot(h, wu[...].astype(BF16))
            act = jax.nn.silu(hg) * hu * rec[:, lane:lane + 1]
            return _dot(act.astype(BF16), wd[...].astype(BF16))

        out = x_rows + expert(wg_lo, wu_lo, wd_lo, REC_W_LO) + expert(wg_hi, wu_hi, wd_hi, REC_W_HI)
        yout[s] = _rms(out, fg_ref[...]) if final_norm else out

        @pl.when(i == n_used)
        def _():
            wait_gather(1 - s)
            wait_scatter(1 - s)


def _experts(layer, inv, e_lo, e_hi, n_used, xa, g, final_g, w_gate, w_up, w_down):
    final_norm = final_g is not None
    if not final_norm:
        final_g = jnp.ones((1, D_MODEL), F32)
    up = lambda pick: pl.BlockSpec((None, None, D_MODEL, EXPERT_FF),
                                   lambda i, inv, lo, hi, nu: (layer, pick(lo, hi)[i], 0, 0))
    down = lambda pick: pl.BlockSpec((None, None, EXPERT_FF, D_MODEL),
                                     lambda i, inv, lo, hi, nu: (layer, pick(lo, hi)[i], 0, 0))
    first, second = (lambda lo, hi: lo), (lambda lo, hi: hi)
    grid_spec = pltpu.PrefetchScalarGridSpec(
        num_scalar_prefetch=4,
        grid=(MOE_NTILES + 1,),
        in_specs=[pl.BlockSpec(memory_space=pl.ANY), _const_spec((1, D_MODEL)), _const_spec((1, D_MODEL)),
                  up(first), up(first), down(first), up(second), up(second), down(second)],
        out_specs=pl.BlockSpec(memory_space=pl.ANY),
        scratch_shapes=[pltpu.VMEM((2, MOE_TILE, MOE_ROW), F32),
                        pltpu.VMEM((2, MOE_TILE, D_MODEL), F32),
                        pltpu.SemaphoreType.DMA((2,)),
                        pltpu.SemaphoreType.DMA((2,)),
                        pltpu.SemaphoreType.DMA(())])
    return pl.pallas_call(
        functools.partial(_experts_kernel, final_norm=final_norm),
        grid_spec=grid_spec,
        out_shape=jax.ShapeDtypeStruct((X_ROWS, D_MODEL), F32),
        compiler_params=_params("arbitrary"),
        name="moe_experts_final" if final_norm else "moe_experts",
    )(inv, e_lo, e_hi, n_used, xa, g, final_g, w_gate, w_up, w_down, w_gate, w_up, w_down)


def _bucket_experts():
    pairs = [(a, b) for a in range(EXPERTS_PER_GROUP) for b in range(a + 1, EXPERTS_PER_GROUP)]
    lo = [g * EXPERTS_PER_GROUP + a for g in range(MOE_GROUPS) for a, _ in pairs]
    hi = [g * EXPERTS_PER_GROUP + b for g in range(MOE_GROUPS) for _, b in pairs]
    return jnp.array(lo, jnp.int32), jnp.array(hi, jnp.int32)


def _routed_moe(layer, xa, bucket, g, w_gate, w_up, w_down, final_g=None):
    pos, tile_bucket = _plan(bucket)
    n_used = jnp.sum((tile_bucket < N_BUCKETS).astype(jnp.int32)).reshape(1)
    tile_bucket = jnp.minimum(jnp.concatenate([tile_bucket, tile_bucket[-1:]]), N_BUCKETS - 1)
    lo, hi = _bucket_experts()
    return _experts(layer, _invert(pos), lo[tile_bucket], hi[tile_bucket], n_used, xa, g, final_g,
                    w_gate, w_up, w_down)


def kernel(x, ln1_g, w_in, lam_re, lam_im, log_dt, b_re, b_im, c_re, c_im, d_skip, w_glu,
           gn_attn, gn_ssm, w_out, ln2_g, w_router_grp, b_router_grp, w_router_exp,
           b_router_exp, w_gate, w_up, w_down, final_g):
    assert x.shape == (BATCH, SEQ, D_MODEL)
    ng = MOE_GROUPS
    x = x.reshape(TOKENS, D_MODEL)

    for l in range(DEPTH):
        q, k, v, u = _inproj(x, ln1_g[l][None, :], w_in[l].astype(BF16))
        attn = _attention(q, k, v).reshape(TOKENS, ATTN_WIDTH)
        disc = _s5_discretize(lam_re[l], lam_im[l], log_dt[l], b_re[l], b_im[l], c_re[l], c_im[l])
        ssm = _s5(u, disc, d_skip[l].reshape(1, SSM_WIDTH), w_glu[l].astype(BF16))
        ssm = ssm.reshape(TOKENS, SSM_WIDTH)

        w_r = jnp.zeros((ROUTER_ROWS, D_MODEL), F32)
        w_r = w_r.at[0:ng].set(w_router_grp[l].T)
        w_r = w_r.at[EXPERT_ROW0:EXPERT_ROW0 + N_EXPERTS].set(
            jnp.transpose(w_router_exp[l], (0, 2, 1)).reshape(N_EXPERTS, D_MODEL))
        b_r = jnp.zeros((ROUTER_ROWS, 1), F32)
        b_r = b_r.at[0:ng, 0].set(b_router_grp[l])
        b_r = b_r.at[EXPERT_ROW0:EXPERT_ROW0 + N_EXPERTS, 0].set(b_router_exp[l].reshape(N_EXPERTS))
        w_r_hi, w_r_lo = _split_bf16(w_r)
        g2 = ln2_g[l][None, :]
        xa, bucket = _outproj(attn, ssm, x, gn_attn[l][None, :], gn_ssm[l][None, :],
                              w_out[l].astype(BF16), g2, w_r_hi, w_r_lo, b_r)
        x = _routed_moe(l, xa, bucket, g2, w_gate, w_up, w_down,
                        final_g[None, :] if l == DEPTH - 1 else None)
    return x[:TOKENS].reshape(BATCH, SEQ, D_MODEL)
```

```python
import functools
import math

import jax
import jax.numpy as jnp
from jax import lax
from jax.experimental import pallas as pl
from jax.experimental.pallas import tpu as pltpu

D_MODEL = 1024
BATCH = 8
SEQ = 2048
DEPTH = 4
ATTN_WIDTH = 512
HEAD_DIM = 64
ATTN_HEADS = 8
DILATIONS = (1, 4, 16)
SPAN = 128
SSM_WIDTH = 512
SSM_GROUP_CH = 16
SSM_GROUPS = 32
SSM_STATE = 64
MOE_GROUPS = 4
EXPERTS_PER_GROUP = 4
N_EXPERTS = 16
EXPERT_FF = 256
RMS_EPS = 1e-6

LANES = 128
BF16_ROWS = 16
VMEM_LIMIT_BYTES = 56 * 1024 * 1024

TOK_TILE = 512
PERM_TILE = 256
S5_CHUNK = 128
S5_GROUPS_PER_CHUNK = 8
S5_NCHUNK = SSM_GROUPS // S5_GROUPS_PER_CHUNK
S5_STATE_LANES = S5_GROUPS_PER_CHUNK * SSM_STATE
S5_SUB = BF16_ROWS

F32 = jnp.float32
BF16 = jnp.bfloat16


def _params(*sem):
    return pltpu.CompilerParams(dimension_semantics=sem, vmem_limit_bytes=VMEM_LIMIT_BYTES)


def _rms(x, g):
    return x * lax.rsqrt(jnp.mean(x * x, axis=-1, keepdims=True) + RMS_EPS) * g


def _split_bf16(a):
    hi = a.astype(BF16)
    lo = (a - hi.astype(F32)).astype(BF16)
    return hi, lo


def _dot(a, b):
    return jnp.dot(a, b, preferred_element_type=F32)


def _dot_hilo(p, a):
    hi, lo = _split_bf16(a)
    return _dot(p, hi) + _dot(p, lo)


def _tok_spec(width):
    return pl.BlockSpec((TOK_TILE, width), lambda i: (i, 0))


def _const_spec(shape):
    return pl.BlockSpec(shape, lambda *_: (0,) * len(shape))


def _inproj_kernel(x_ref, g_ref, w_ref, q_ref, k_ref, v_ref, u_ref):
    h = _rms(x_ref[...], g_ref[...]).astype(BF16)
    p = _dot(h, w_ref[...])
    aw = ATTN_WIDTH
    q_ref[...] = (p[:, :aw] * (1.0 / math.sqrt(HEAD_DIM))).astype(BF16)
    k_ref[...] = p[:, aw:2 * aw].astype(BF16)
    v_ref[...] = p[:, 2 * aw:3 * aw].astype(BF16)
    u_ref[...] = p[:, 3 * aw:].astype(BF16)


def _inproj(x, g, w):
    aw = ATTN_WIDTH
    out = jax.ShapeDtypeStruct((TOKENS, aw), BF16)
    outs = pl.pallas_call(
        _inproj_kernel,
        grid=(TOKENS // TOK_TILE,),
        in_specs=[_tok_spec(D_MODEL), _const_spec((1, D_MODEL)),
                  _const_spec((D_MODEL, 3 * aw + SSM_WIDTH))],
        out_specs=[_tok_spec(aw)] * 4,
        out_shape=[out] * 4,
        compiler_params=_params("parallel"),
        name="inproj",
    )(x, g, w)
    return [o.reshape(BATCH, SEQ, aw) for o in outs]


def _attn_block(q, kk, vv, bias, low_half, lane):
    heads = range(ATTN_HEADS)
    nt = (((1,), (1,)), ((), ()))
    scores = []
    for h in heads:
        sl = slice((h // 2) * LANES, (h // 2 + 1) * LANES)
        keep = low_half if h % 2 == 0 else jnp.logical_not(low_half)
        qm = jnp.where(keep, q[:, sl], jnp.zeros_like(q[:, sl]))
        scores.append(lax.dot_general(qm, kk[:, sl], nt, preferred_element_type=F32) + bias)
    probs, dens, lses = [], [], []
    for h in heads:
        m = jnp.max(scores[h], axis=-1, keepdims=True)
        p = jnp.exp(scores[h] - m)
        den = jnp.sum(p, axis=-1, keepdims=True)
        probs.append(p.astype(BF16))
        dens.append(den)
        lses.append(m + jnp.log(den))
    pv = [_dot(probs[h], vv[:, (h // 2) * LANES:(h // 2 + 1) * LANES]) / dens[h] for h in heads]
    lse_tile = jnp.zeros((SPAN, LANES), F32)
    for h in heads:
        lse_tile = jnp.where(lane == h, lses[h], lse_tile)
    outs = [jnp.where(low_half, pv[2 * j], pv[2 * j + 1]) for j in range(ATTN_HEADS // 2)]
    return jnp.concatenate(outs, axis=-1), lse_tile


def _attn_kernel(q_ref, k_ref, v_ref, p4_ref, p4t_ref, p16_ref, p16t_ref, e_ref, o_ref,
                 qkv4, qkv16, lse1, o4p, lse4p, o16p, lse16p):
    lane = lax.broadcasted_iota(jnp.int32, (SPAN, LANES), 1)
    low_half = lane < HEAD_DIM
    row0 = lax.broadcasted_iota(jnp.int32, (SPAN, SPAN), 0)
    col0 = lax.broadcasted_iota(jnp.int32, (SPAN, SPAN), 1)
    bias_first = jnp.where(col0 <= row0, 0.0, -jnp.inf).astype(F32)
    row = lax.broadcasted_iota(jnp.int32, (SPAN, 2 * SPAN), 0)
    col = lax.broadcasted_iota(jnp.int32, (SPAN, 2 * SPAN), 1)
    bias_band = jnp.where((col >= row) & (col <= row + SPAN), 0.0, -jnp.inf).astype(F32)
    block = functools.partial(_attn_block, low_half=low_half, lane=lane)
    srcs = (q_ref, k_ref, v_ref)

    def permute(i, carry):
        rows = pl.ds(pl.multiple_of(i * PERM_TILE, PERM_TILE), PERM_TILE)
        for a, src in enumerate(srcs):
            x = src[rows, :]
            y4 = _dot(p4_ref[...], x).astype(BF16)
            y16 = _dot(p16_ref[...], x).astype(BF16)
            n4, n16 = PERM_TILE // 4, PERM_TILE // 16
            for r in range(4):
                qkv4[a, r, pl.ds(pl.multiple_of(i * n4, n4), n4), :] = y4[r * n4:(r + 1) * n4]
            for r in range(16):
                qkv16[a, r, pl.ds(pl.multiple_of(i * n16, n16), n16), :] = y16[r * n16:(r + 1) * n16]
        return carry

    lax.fori_loop(0, SEQ // PERM_TILE, permute, 0)

    def branch(get, put, nblocks):
        head = pl.ds(0, SPAN)
        o, lse = block(get(0, head), get(1, head), get(2, head), bias_first)
        put(head, o, lse)
        if nblocks > 1:
            def body(n, carry):
                qs = pl.ds(pl.multiple_of(n * SPAN, SPAN), SPAN)
                ks = pl.ds(pl.multiple_of((n - 1) * SPAN, SPAN), 2 * SPAN)
                o, lse = block(get(0, qs), get(1, ks), get(2, ks), bias_band)
                put(qs, o, lse)
                return carry
            lax.fori_loop(1, nblocks, body, 0)

    def put1(rows, o, lse):
        o_ref[rows, :] = o
        lse1[rows, :] = lse

    branch(lambda a, rows: srcs[a][rows, :], put1, SEQ // SPAN)

    def class4(r, carry):
        def put(rows, o, lse):
            o4p[r, rows, :] = o.astype(BF16)
            lse4p[r, rows, :] = lse
        branch(lambda a, rows: qkv4[a, r, rows, :], put, SEQ // 4 // SPAN)
        return carry

    lax.fori_loop(0, 4, class4, 0)

    def class16(r, carry):
        def put(rows, o, lse):
            o16p[r, rows, :] = o.astype(BF16)
            lse16p[r, rows, :] = lse
        branch(lambda a, rows: qkv16[a, r, rows, :], put, SEQ // 16 // SPAN)
        return carry

    lax.fori_loop(0, 16, class16, 0)

    def expand(w):
        hi, lo = _split_bf16(w)
        return _dot(hi, e_ref[...]) + _dot(lo, e_ref[...])

    def combine(i, carry):
        rows = pl.ds(pl.multiple_of(i * PERM_TILE, PERM_TILE), PERM_TILE)
        n4, n16 = PERM_TILE // 4, PERM_TILE // 16
        r4 = pl.ds(pl.multiple_of(i * n4, n4), n4)
        r16 = pl.ds(pl.multiple_of(i * n16, n16), n16)
        o4 = _dot(p4t_ref[...], jnp.concatenate([o4p[r, r4, :] for r in range(4)], axis=0))
        l4 = _dot_hilo(p4t_ref[...], jnp.concatenate([lse4p[r, r4, :] for r in range(4)], axis=0))
        o16 = _dot(p16t_ref[...], jnp.concatenate([o16p[r, r16, :] for r in range(16)], axis=0))
        l16 = _dot_hilo(p16t_ref[...], jnp.concatenate([lse16p[r, r16, :] for r in range(16)], axis=0))
        l1 = lse1[rows, :]
        m = jnp.maximum(jnp.maximum(l1, l4), l16)
        e1, e4, e16 = jnp.exp(l1 - m), jnp.exp(l4 - m), jnp.exp(l16 - m)
        den = e1 + e4 + e16
        o_ref[rows, :] = (expand(e1 / den) * o_ref[rows, :] + expand(e4 / den) * o4
                          + expand(e16 / den) * o16)
        return carry

    lax.fori_loop(0, SEQ // PERM_TILE, combine, 0)


def _perm_matrix(dil):
    n = PERM_TILE // dil
    out_row = jnp.arange(PERM_TILE)
    src = dil * (out_row % n) + out_row // n
    return (src[:, None] == jnp.arange(PERM_TILE)[None, :]).astype(BF16)


def _head_expand_matrix():
    r = jnp.arange(LANES)[:, None]
    c = jnp.arange(ATTN_WIDTH)[None, :] // HEAD_DIM
    return (r == c).astype(BF16)


def _attention(q, k, v):
    aw = ATTN_WIDTH
    seq_spec = pl.BlockSpec((None, SEQ, aw), lambda b: (b, 0, 0))
    p4, p16 = _perm_matrix(4), _perm_matrix(16)
    consts = (p4, p4.T, p16, p16.T, _head_expand_matrix())
    return pl.pallas_call(
        _attn_kernel,
        grid=(BATCH,),
        in_specs=[seq_spec] * 3 + [_const_spec(c.shape) for c in consts],
        out_specs=seq_spec,
        out_shape=jax.ShapeDtypeStruct((BATCH, SEQ, aw), F32),
        scratch_shapes=[pltpu.VMEM((3, 4, SEQ // 4, aw), BF16),
                        pltpu.VMEM((3, 16, SEQ // 16, aw), BF16),
                        pltpu.VMEM((SEQ, LANES), F32),
                        pltpu.VMEM((4, SEQ // 4, aw), BF16),
                        pltpu.VMEM((4, SEQ // 4, LANES), F32),
                        pltpu.VMEM((16, SEQ // 16, aw), BF16),
                        pltpu.VMEM((16, SEQ // 16, LANES), F32)],
        compiler_params=_params("parallel"),
        name="attention",
    )(q, k, v, *consts)


def _s5_kernel(u_ref, pf_ref, pb_ref, bre_ref, bim_ref, lre_ref, lim_ref, cre_ref, cim_ref, d_ref,
               wglu_ref, o_ref, st_re, st_im, u_buf, xr_buf, xi_buf, y_buf):
    rows = S5_CHUNK * BATCH
    sub_rows = S5_SUB * BATCH

    @pl.when(pl.program_id(0) == 0)
    def _():
        st_re[...] = jnp.zeros_like(st_re)
        st_im[...] = jnp.zeros_like(st_im)

    for tb in range(S5_CHUNK // S5_SUB):
        t = slice(tb * S5_SUB, (tb + 1) * S5_SUB)
        piece = jnp.concatenate([u_ref[b, t, :] for b in range(BATCH)], axis=0)
        u_buf[tb * sub_rows:(tb + 1) * sub_rows, :] = _dot(pf_ref[...], piece)

    for c in range(S5_NCHUNK):
        ch = slice(c * LANES, (c + 1) * LANES)
        stl = slice(c * S5_STATE_LANES, (c + 1) * S5_STATE_LANES)
        ub = u_buf[:, ch].astype(BF16)
        xr_buf[...] = _dot(ub, bre_ref[c])
        xi_buf[...] = _dot(ub, bim_ref[c])
        lr = jnp.broadcast_to(lre_ref[:, stl], (BATCH, S5_STATE_LANES))
        li = jnp.broadcast_to(lim_ref[:, stl], (BATCH, S5_STATE_LANES))

        def step(s, carry):
            xr, xi = carry
            sl = pl.ds(pl.multiple_of(s * BATCH, BATCH), BATCH)
            nr = lr * xr - li * xi + xr_buf[sl, :]
            ni = lr * xi + li * xr + xi_buf[sl, :]
            xr_buf[sl, :] = nr
            xi_buf[sl, :] = ni
            return nr, ni

        xr, xi = lax.fori_loop(0, S5_CHUNK, step, (st_re[:, stl], st_im[:, stl]), unroll=4)
        st_re[:, stl] = xr
        st_im[:, stl] = xi
        yc = (_dot(xr_buf[...].astype(BF16), cre_ref[c]) - _dot(xi_buf[...].astype(BF16), cim_ref[c]))
        y_buf[:, ch] = yc + d_ref[:, ch] * u_buf[:, ch]

    y = jax.nn.gelu(y_buf[...])
    z = _dot(y.astype(BF16), wglu_ref[...])
    y_buf[...] = y * jax.nn.sigmoid(z)
    for tb in range(S5_CHUNK // S5_SUB):
        back = _dot(pb_ref[...], y_buf[tb * sub_rows:(tb + 1) * sub_rows, :].astype(BF16)).astype(BF16)
        for b in range(BATCH):
            o_ref[b, tb * S5_SUB:(tb + 1) * S5_SUB, :] = back[b * S5_SUB:(b + 1) * S5_SUB]


def _s5_discretize(lam_re, lam_im, log_dt, b_re, b_im, c_re, c_im):
    dt = jnp.exp(log_dt)[:, None]
    mag = jnp.exp(lam_re * dt)
    lb_re, lb_im = mag * jnp.cos(lam_im * dt), mag * jnp.sin(lam_im * dt)
    den = lam_re * lam_re + lam_im * lam_im
    nr, ni = lb_re - 1.0, lb_im
    f_re = (nr * lam_re + ni * lam_im) / den
    f_im = (ni * lam_re - nr * lam_im) / den
    bb_re = f_re[..., None] * b_re - f_im[..., None] * b_im
    bb_im = f_re[..., None] * b_im + f_im[..., None] * b_re
    eye = jnp.eye(S5_GROUPS_PER_CHUNK, dtype=F32)
    gpc, nch = S5_GROUPS_PER_CHUNK, S5_NCHUNK

    def b_blockdiag(b):
        b = b.reshape(nch, gpc, SSM_STATE, SSM_GROUP_CH)
        m = jnp.einsum('cgph,gk->cghkp', b, eye)
        return m.reshape(nch, gpc * SSM_GROUP_CH, gpc * SSM_STATE).astype(BF16)

    def c_blockdiag(c):
        c = c.reshape(nch, gpc, SSM_GROUP_CH, SSM_STATE)
        m = jnp.einsum('cghp,gk->cgpkh', c, eye)
        return m.reshape(nch, gpc * SSM_STATE, gpc * SSM_GROUP_CH).astype(BF16)

    return (b_blockdiag(bb_re), b_blockdiag(bb_im),
            lb_re.reshape(1, SSM_GROUPS * SSM_STATE), lb_im.reshape(1, SSM_GROUPS * SSM_STATE),
            c_blockdiag(c_re), c_blockdiag(c_im))


def _s5_reorder_matrices():
    n = S5_SUB * BATCH
    out_row = jnp.arange(n)
    src = (out_row % BATCH) * S5_SUB + out_row // BATCH
    fwd = (src[:, None] == jnp.arange(n)[None, :]).astype(BF16)
    return fwd, fwd.T


def _s5(u, disc, d_skip, w_glu):
    rows = S5_CHUNK * BATCH
    blk = pl.BlockSpec((BATCH, S5_CHUNK, SSM_WIDTH), lambda i: (0, i, 0))
    args = _s5_reorder_matrices() + tuple(disc) + (d_skip, w_glu)
    return pl.pallas_call(
        _s5_kernel,
        grid=(SEQ // S5_CHUNK,),
        in_specs=[blk] + [_const_spec(a.shape) for a in args],
        out_specs=blk,
        out_shape=jax.ShapeDtypeStruct((BATCH, SEQ, SSM_WIDTH), BF16),
        scratch_shapes=[pltpu.VMEM((BATCH, SSM_GROUPS * SSM_STATE), F32),
                        pltpu.VMEM((BATCH, SSM_GROUPS * SSM_STATE), F32),
                        pltpu.VMEM((rows, SSM_WIDTH), F32),
                        pltpu.VMEM((rows, S5_STATE_LANES), F32),
                        pltpu.VMEM((rows, S5_STATE_LANES), F32),
                        pltpu.VMEM((rows, SSM_WIDTH), F32)],
        compiler_params=_params("arbitrary"),
        name="s5",
    )(u, *args)


def _outproj_kernel(attn_ref, ssm_ref, x_ref, ga_ref, gs_ref, w_ref, g2_ref, whi_ref, wlo_ref, b_ref,
                    xa_ref, bucket_ref):
    a_n = _rms(attn_ref[...], ga_ref[...]).astype(BF16)
    s_n = _rms(ssm_ref[...].astype(F32), gs_ref[...]).astype(BF16)
    y = _dot(a_n, w_ref[0:ATTN_WIDTH, :]) + _dot(s_n, w_ref[ATTN_WIDTH:, :])
    x = x_ref[...] + y
    rec, bucket = _route(_rms(x, g2_ref[...]), whi_ref[...], wlo_ref[...], b_ref[...])
    xa_ref[:, 0:D_MODEL] = x
    xa_ref[:, D_MODEL:] = rec
    bucket_ref[...] = bucket


def _outproj(attn, ssm, x, ga, gs, w, g2, whi, wlo, bias):
    aw = ATTN_WIDTH
    return pl.pallas_call(
        _outproj_kernel,
        grid=(TOKENS // TOK_TILE,),
        in_specs=[_tok_spec(aw), _tok_spec(SSM_WIDTH), _tok_spec(D_MODEL),
                  _const_spec((1, aw)), _const_spec((1, SSM_WIDTH)),
                  _const_spec((aw + SSM_WIDTH, D_MODEL)), _const_spec((1, D_MODEL)),
                  _const_spec((ROUTER_ROWS, D_MODEL)), _const_spec((ROUTER_ROWS, D_MODEL)),
                  _const_spec((ROUTER_ROWS, 1))],
        out_specs=[_tok_spec(MOE_ROW), pl.BlockSpec((1, TOK_TILE), lambda i: (0, i))],
        out_shape=[jax.ShapeDtypeStruct((TOKENS, MOE_ROW), F32),
                   jax.ShapeDtypeStruct((1, TOKENS), F32)],
        compiler_params=_params("parallel"),
        name="outproj_router",
    )(attn, ssm, x, ga, gs, w, g2, whi, wlo, bias)


ROUTER_ROWS = 32
EXPERT_ROW0 = 8
PAIRS_PER_GROUP = EXPERTS_PER_GROUP * (EXPERTS_PER_GROUP - 1) // 2
N_BUCKETS = MOE_GROUPS * PAIRS_PER_GROUP
REC_BUCKET, REC_W_LO, REC_W_HI = 0, 1, 2
TOKENS = BATCH * SEQ
PLAN_SIDE = 128
MOE_TILE = 256
MOE_NTILES = TOKENS // MOE_TILE + N_BUCKETS
MOE_SLOTS = MOE_NTILES * MOE_TILE
MOE_ROW = D_MODEL + LANES
DUMP_ROWS = 2 * MOE_TILE
X_ROWS = TOKENS + DUMP_ROWS
MOE_BUFS = 3
SPARE_TILE = MOE_NTILES + 3
INV_LEN = (MOE_NTILES + 4) * MOE_TILE
assert PLAN_SIDE * PLAN_SIDE == TOKENS and MOE_NTILES <= PLAN_SIDE
assert INV_LEN % DUMP_ROWS == 0 and SPARE_TILE * MOE_TILE < INV_LEN
assert SPARE_TILE % 2 == 1


def _route(h, whi, wlo, bias):
    h_hi, h_lo = _split_bf16(h)
    nt = (((1,), (1,)), ((), ()))
    logits = (lax.dot_general(whi, h_hi, nt, preferred_element_type=F32)
              + lax.dot_general(wlo, h_hi, nt, preferred_element_type=F32)
              + lax.dot_general(whi, h_lo, nt, preferred_element_type=F32)
              + bias)
    ng, ne = MOE_GROUPS, EXPERTS_PER_GROUP
    gl = [logits[g:g + 1, :] for g in range(ng)]
    best, grp = gl[0], jnp.zeros_like(gl[0], dtype=jnp.int32)
    for g in range(1, ng):
        better = gl[g] > best
        grp = jnp.where(better, g, grp)
        best = jnp.where(better, gl[g], best)
    g1 = 1.0 / sum(jnp.exp(x - best) for x in gl)
    sel = []
    for e in range(ne):
        acc = jnp.zeros_like(best)
        for g in range(ng):
            r = EXPERT_ROW0 + g * ne + e
            acc = jnp.where(grp == g, logits[r:r + 1, :], acc)
        sel.append(acc)

    def first_argmax(vals):
        bv, bi = vals[0], jnp.zeros_like(grp)
        for e in range(1, ne):
            better = vals[e] > bv
            bi = jnp.where(better, e, bi)
            bv = jnp.where(better, vals[e], bv)
        return bv, bi

    v1, i1 = first_argmax(sel)
    v2, i2 = first_argmax([jnp.where(i1 == e, -jnp.inf, sel[e]) for e in range(ne)])
    e2 = jnp.exp(v2 - v1)
    w1 = g1 / (1.0 + e2)
    w2 = g1 * e2 / (1.0 + e2)
    first_is_low = i1 < i2
    lo = jnp.where(first_is_low, i1, i2)
    hi = jnp.where(first_is_low, i2, i1)
    pair = jnp.where(lo == 0, 0, jnp.where(lo == 1, 3, 5)) + hi - lo - 1
    bucket = (grp * PAIRS_PER_GROUP + pair).astype(F32)
    w_lo = jnp.where(first_is_low, w1, w2)
    w_hi = jnp.where(first_is_low, w2, w1)
    tokens = logits.shape[1]
    rowid = lax.broadcasted_iota(jnp.int32, (LANES, tokens), 0)
    table = jnp.where(rowid == REC_BUCKET, bucket,
                      jnp.where(rowid == REC_W_LO, w_lo, jnp.where(rowid == REC_W_HI, w_hi, 0.0)))
    return table.T, bucket


def _plan_kernel(bucket_ref, pos_ref, tile_bucket_ref):
    n = PLAN_SIDE
    bucket = bucket_ref[...]
    r = lax.broadcasted_iota(jnp.int32, (n, n), 0)
    c = lax.broadcasted_iota(jnp.int32, (n, n), 1)
    before_in_row = (r < c).astype(BF16)
    rows_before = (c < r).astype(BF16)
    ones = jnp.ones((n, n), BF16)
    tile_start = (c * MOE_TILE).astype(F32)
    pos = jnp.zeros((n, n), F32)
    base = jnp.zeros((n, n), F32)
    ended = jnp.zeros((n, n), F32)
    for k in range(N_BUCKETS):
        member = bucket == float(k)
        mb = member.astype(BF16)
        in_row = _dot(mb, before_in_row)
        row_count = _dot(mb, ones).astype(BF16)
        rank = _dot(rows_before, row_count) + in_row
        total = _dot(ones, row_count)
        pos = jnp.where(member, base + rank, pos)
        base = base + jnp.ceil(total * (1.0 / MOE_TILE)) * MOE_TILE
        ended = ended + (tile_start >= base).astype(F32)
    pos_ref[...] = pos.astype(jnp.int32)
    tile_bucket_ref[...] = ended.astype(jnp.int32)


def _plan(bucket):
    n = PLAN_SIDE
    pos, tile_bucket = pl.pallas_call(
        _plan_kernel,
        out_shape=[jax.ShapeDtypeStruct((n, n), jnp.int32)] * 2,
        name="moe_plan",
    )(bucket.reshape(n, n))
    return pos.reshape(TOKENS), tile_bucket[0, :MOE_NTILES]


def _invert_kernel(pos_ref, inv_ref):
    def init(blk, carry):
        for r in range(DUMP_ROWS):
            inv_ref[blk * DUMP_ROWS + r] = TOKENS + r
        return carry

    def place(t, carry):
        inv_ref[pos_ref[t]] = t
        return carry

    lax.fori_loop(0, INV_LEN // DUMP_ROWS, init, 0)
    lax.fori_loop(0, TOKENS, place, 0, unroll=16)


def _invert(pos):
    smem = pl.BlockSpec(memory_space=pltpu.SMEM)
    return pl.pallas_call(
        _invert_kernel,
        in_specs=[smem],
        out_specs=smem,
        out_shape=jax.ShapeDtypeStruct((INV_LEN,), jnp.int32),
        name="moe_invert",
    )(pos)


def _row_copy(src, src_row, dst, dst_row, sem):
    return pltpu.make_async_copy(src.at[pl.ds(src_row, 1), :], dst.at[pl.ds(dst_row, 1), :], sem)


def _experts_kernel(inv_ref, e_lo_ref, e_hi_ref, nused_ref, xa_ref, g_ref, fg_ref,
                    wg_lo, wu_lo, wd_lo, wg_hi, wu_hi, wd_hi, xo_ref,
                    xin, yout, gsem, ssem, fsem, *, final_norm):
    i = pl.program_id(0)
    n_used = nused_ref[0]
    s = lax.rem(i, MOE_BUFS)
    s_prev = lax.rem(i + 2, MOE_BUFS)
    s_next = lax.rem(i + 1, MOE_BUFS)

    def start_gather(tile, slot):
        for r in range(MOE_TILE):
            src = jnp.minimum(inv_ref[tile * MOE_TILE + r], TOKENS - 1)
            _row_copy(xa_ref, src, xin.at[slot], r, gsem.at[slot]).start(priority=r % 2)

    def start_scatter(tile, slot):
        for r in range(MOE_TILE):
            _row_copy(yout.at[slot], r, xo_ref, inv_ref[tile * MOE_TILE + r],
                      ssem.at[slot]).start(priority=r % 2)

    def wait_gather(slot):
        pltpu.make_async_copy(xa_ref.at[pl.ds(0, MOE_TILE), :], xin.at[slot], gsem.at[slot]).wait()

    def wait_scatter(slot):
        pltpu.make_async_copy(yout.at[slot], xo_ref.at[pl.ds(0, MOE_TILE), :], ssem.at[slot]).wait()

    @pl.when(i == 0)
    def _():
        yout[MOE_BUFS - 1] = jnp.zeros((MOE_TILE, D_MODEL), F32)
        for half in range(DUMP_ROWS // MOE_TILE):
            fill = pltpu.make_async_copy(
                yout.at[MOE_BUFS - 1], xo_ref.at[pl.ds(TOKENS + half * MOE_TILE, MOE_TILE), :], fsem)
            fill.start()
            fill.wait()
        start_gather(0, 0)
        start_gather(1, 1)

    @pl.when(i <= n_used)
    def _():
        wait_gather(s)

        @pl.when(i >= 2)
        def _():
            wait_scatter(s)

        start_gather(i + 2, s_prev)
        start_scatter(jnp.where(i >= 1, i - 1, SPARE_TILE), s_prev)

        xt = xin[s]
        x_rows = xt[:, 0:D_MODEL]
        rec = xt[:, D_MODEL:]
        h = _rms(x_rows, g_ref[...]).astype(BF16)

        def expert(wg, wu, wd, lane):
            hg = _dot(h, wg[...].astype(BF16))
            hu = _dot(h, wu[...].astype(BF16))
            act = jax.nn.silu(hg) * hu * rec[:, lane:lane + 1]
            return _dot(act.astype(BF16), wd[...].astype(BF16))

        out = x_rows + expert(wg_lo, wu_lo, wd_lo, REC_W_LO) + expert(wg_hi, wu_hi, wd_hi, REC_W_HI)
        yout[s] = _rms(out, fg_ref[...]) if final_norm else out

        @pl.when(i == n_used)
        def _():
            wait_gather(s_next)
            wait_gather(s_prev)
            wait_scatter(s_next)
            wait_scatter(s_prev)


def _experts(layer, inv, e_lo, e_hi, n_used, xa, g, final_g, w_gate, w_up, w_down):
    final_norm = final_g is not None
    if not final_norm:
        final_g = jnp.ones((1, D_MODEL), F32)
    up = lambda pick: pl.BlockSpec((None, None, D_MODEL, EXPERT_FF),
                                   lambda i, inv, lo, hi, nu: (layer, pick(lo, hi)[i], 0, 0))
    down = lambda pick: pl.BlockSpec((None, None, EXPERT_FF, D_MODEL),
                                     lambda i, inv, lo, hi, nu: (layer, pick(lo, hi)[i], 0, 0))
    first, second = (lambda lo, hi: lo), (lambda lo, hi: hi)
    grid_spec = pltpu.PrefetchScalarGridSpec(
        num_scalar_prefetch=4,
        grid=(MOE_NTILES + 1,),
        in_specs=[pl.BlockSpec(memory_space=pl.ANY), _const_spec((1, D_MODEL)), _const_spec((1, D_MODEL)),
                  up(first), up(first), down(first), up(second), up(second), down(second)],
        out_specs=pl.BlockSpec(memory_space=pl.ANY),
        scratch_shapes=[pltpu.VMEM((MOE_BUFS, MOE_TILE, MOE_ROW), F32),
                        pltpu.VMEM((MOE_BUFS, MOE_TILE, D_MODEL), F32),
                        pltpu.SemaphoreType.DMA((MOE_BUFS,)),
                        pltpu.SemaphoreType.DMA((MOE_BUFS,)),
                        pltpu.SemaphoreType.DMA(())])
    return pl.pallas_call(
        functools.partial(_experts_kernel, final_norm=final_norm),
        grid_spec=grid_spec,
        out_shape=jax.ShapeDtypeStruct((X_ROWS, D_MODEL), F32),
        compiler_params=_params("arbitrary"),
        name="moe_experts_final" if final_norm else "moe_experts",
    )(inv, e_lo, e_hi, n_used, xa, g, final_g, w_gate, w_up, w_down, w_gate, w_up, w_down)


def _bucket_experts():
    pairs = [(a, b) for a in range(EXPERTS_PER_GROUP) for b in range(a + 1, EXPERTS_PER_GROUP)]
    lo = [g * EXPERTS_PER_GROUP + a for g in range(MOE_GROUPS) for a, _ in pairs]
    hi = [g * EXPERTS_PER_GROUP + b for g in range(MOE_GROUPS) for _, b in pairs]
    return jnp.array(lo, jnp.int32), jnp.array(hi, jnp.int32)


def _routed_moe(layer, xa, bucket, g, w_gate, w_up, w_down, final_g=None):
    pos, tile_bucket = _plan(bucket)
    n_used = jnp.sum((tile_bucket < N_BUCKETS).astype(jnp.int32)).reshape(1)
    tile_bucket = jnp.minimum(jnp.concatenate([tile_bucket, tile_bucket[-1:]]), N_BUCKETS - 1)
    lo, hi = _bucket_experts()
    return _experts(layer, _invert(pos), lo[tile_bucket], hi[tile_bucket], n_used, xa, g, final_g,
                    w_gate, w_up, w_down)


def kernel(x, ln1_g, w_in, lam_re, lam_im, log_dt, b_re, b_im, c_re, c_im, d_skip, w_glu,
           gn_attn, gn_ssm, w_out, ln2_g, w_router_grp, b_router_grp, w_router_exp,
           b_router_exp, w_gate, w_up, w_down, final_g):
    assert x.shape == (BATCH, SEQ, D_MODEL)
    ng = MOE_GROUPS
    x = x.reshape(TOKENS, D_MODEL)

    for l in range(DEPTH):
        q, k, v, u = _inproj(x, ln1_g[l][None, :], w_in[l].astype(BF16))
        attn = _attention(q, k, v).reshape(TOKENS, ATTN_WIDTH)
        disc = _s5_discretize(lam_re[l], lam_im[l], log_dt[l], b_re[l], b_im[l], c_re[l], c_im[l])
        ssm = _s5(u, disc, d_skip[l].reshape(1, SSM_WIDTH), w_glu[l].astype(BF16))
        ssm = ssm.reshape(TOKENS, SSM_WIDTH)

        w_r = jnp.zeros((ROUTER_ROWS, D_MODEL), F32)
        w_r = w_r.at[0:ng].set(w_router_grp[l].T)
        w_r = w_r.at[EXPERT_ROW0:EXPERT_ROW0 + N_EXPERTS].set(
            jnp.transpose(w_router_exp[l], (0, 2, 1)).reshape(N_EXPERTS, D_MODEL))
        b_r = jnp.zeros((ROUTER_ROWS, 1), F32)
        b_r = b_r.at[0:ng, 0].set(b_router_grp[l])
        b_r = b_r.at[EXPERT_ROW0:EXPERT_ROW0 + N_EXPERTS, 0].set(b_router_exp[l].reshape(N_EXPERTS))
        w_r_hi, w_r_lo = _split_bf16(w_r)
        g2 = ln2_g[l][None, :]
        xa, bucket = _outproj(attn, ssm, x, gn_attn[l][None, :], gn_ssm[l][None, :],
                              w_out[l].astype(BF16), g2, w_r_hi, w_r_lo, b_r)
        x = _routed_moe(l, xa, bucket, g2, w_gate, w_up, w_down,
                        final_g[None, :] if l == DEPTH - 1 else None)
    return x[:TOKENS].reshape(BATCH, SEQ, D_MODEL)
```

```python
import functools
import math

import jax
import jax.numpy as jnp
from jax import lax
from jax.experimental import pallas as pl
from jax.experimental.pallas import tpu as pltpu

D_MODEL = 1024
BATCH = 8
SEQ = 2048
DEPTH = 4
ATTN_WIDTH = 512
HEAD_DIM = 64
ATTN_HEADS = 8
DILATIONS = (1, 4, 16)
SPAN = 128
SSM_WIDTH = 512
SSM_GROUP_CH = 16
SSM_GROUPS = 32
SSM_STATE = 64
MOE_GROUPS = 4
EXPERTS_PER_GROUP = 4
N_EXPERTS = 16
EXPERT_FF = 256
RMS_EPS = 1e-6

LANES = 128
BF16_ROWS = 16
VMEM_LIMIT_BYTES = 56 * 1024 * 1024

TOK_TILE = 512
PERM_TILE = 256
S5_CHUNK = 128
S5_GROUPS_PER_CHUNK = 8
S5_NCHUNK = SSM_GROUPS // S5_GROUPS_PER_CHUNK
S5_STATE_LANES = S5_GROUPS_PER_CHUNK * SSM_STATE
S5_SUB = BF16_ROWS

F32 = jnp.float32
BF16 = jnp.bfloat16


def _params(*sem):
    return pltpu.CompilerParams(dimension_semantics=sem, vmem_limit_bytes=VMEM_LIMIT_BYTES)


def _rms(x, g):
    return x * lax.rsqrt(jnp.mean(x * x, axis=-1, keepdims=True) + RMS_EPS) * g


def _split_bf16(a):
    hi = a.astype(BF16)
    lo = (a - hi.astype(F32)).astype(BF16)
    return hi, lo


def _dot(a, b):
    return jnp.dot(a, b, preferred_element_type=F32)


def _dot_hilo(p, a):
    hi, lo = _split_bf16(a)
    return _dot(p, hi) + _dot(p, lo)


def _tok_spec(width):
    return pl.BlockSpec((TOK_TILE, width), lambda i: (i, 0))


def _const_spec(shape):
    return pl.BlockSpec(shape, lambda *_: (0,) * len(shape))


def _inproj_kernel(x_ref, g_ref, w_ref, q_ref, k_ref, v_ref, u_ref):
    h = _rms(x_ref[...], g_ref[...]).astype(BF16)
    p = _dot(h, w_ref[...])
    aw = ATTN_WIDTH
    q_ref[...] = (p[:, :aw] * (1.0 / math.sqrt(HEAD_DIM))).astype(BF16)
    k_ref[...] = p[:, aw:2 * aw].astype(BF16)
    v_ref[...] = p[:, 2 * aw:3 * aw].astype(BF16)
    u_ref[...] = p[:, 3 * aw:].astype(BF16)


def _inproj(x, g, w):
    aw = ATTN_WIDTH
    out = jax.ShapeDtypeStruct((TOKENS, aw), BF16)
    outs = pl.pallas_call(
        _inproj_kernel,
        grid=(TOKENS // TOK_TILE,),
        in_specs=[_tok_spec(D_MODEL), _const_spec((1, D_MODEL)),
                  _const_spec((D_MODEL, 3 * aw + SSM_WIDTH))],
        out_specs=[_tok_spec(aw)] * 4,
        out_shape=[out] * 4,
        compiler_params=_params("parallel"),
        name="inproj",
    )(x, g, w)
    return [o.reshape(BATCH, SEQ, aw) for o in outs]


def _attn_block(q, kk, vv, bias, low_half, lane):
    heads = range(ATTN_HEADS)
    nt = (((1,), (1,)), ((), ()))
    scores = []
    for h in heads:
        sl = slice((h // 2) * LANES, (h // 2 + 1) * LANES)
        keep = low_half if h % 2 == 0 else jnp.logical_not(low_half)
        qm = jnp.where(keep, q[:, sl], jnp.zeros_like(q[:, sl]))
        scores.append(lax.dot_general(qm, kk[:, sl], nt, preferred_element_type=F32) + bias)
    probs, dens, lses = [], [], []
    for h in heads:
        m = jnp.max(scores[h], axis=-1, keepdims=True)
        p = jnp.exp(scores[h] - m)
        den = jnp.sum(p, axis=-1, keepdims=True)
        probs.append(p.astype(BF16))
        dens.append(den)
        lses.append(m + jnp.log(den))
    pv = [_dot(probs[h], vv[:, (h // 2) * LANES:(h // 2 + 1) * LANES]) / dens[h] for h in heads]
    lse_tile = jnp.zeros((SPAN, LANES), F32)
    for h in heads:
        lse_tile = jnp.where(lane == h, lses[h], lse_tile)
    outs = [jnp.where(low_half, pv[2 * j], pv[2 * j + 1]) for j in range(ATTN_HEADS // 2)]
    return jnp.concatenate(outs, axis=-1), lse_tile


def _attn_kernel(q_ref, k_ref, v_ref, p4_ref, p4t_ref, p16_ref, p16t_ref, e_ref, o_ref,
                 qkv4, qkv16, lse1, o4p, lse4p, o16p, lse16p):
    lane = lax.broadcasted_iota(jnp.int32, (SPAN, LANES), 1)
    low_half = lane < HEAD_DIM
    row0 = lax.broadcasted_iota(jnp.int32, (SPAN, SPAN), 0)
    col0 = lax.broadcasted_iota(jnp.int32, (SPAN, SPAN), 1)
    bias_first = jnp.where(col0 <= row0, 0.0, -jnp.inf).astype(F32)
    row = lax.broadcasted_iota(jnp.int32, (SPAN, 2 * SPAN), 0)
    col = lax.broadcasted_iota(jnp.int32, (SPAN, 2 * SPAN), 1)
    bias_band = jnp.where((col >= row) & (col <= row + SPAN), 0.0, -jnp.inf).astype(F32)
    block = functools.partial(_attn_block, low_half=low_half, lane=lane)
    srcs = (q_ref, k_ref, v_ref)

    def permute(i, carry):
        rows = pl.ds(pl.multiple_of(i * PERM_TILE, PERM_TILE), PERM_TILE)
        for a, src in enumerate(srcs):
            x = src[rows, :]
            y4 = _dot(p4_ref[...], x).astype(BF16)
            y16 = _dot(p16_ref[...], x).astype(BF16)
            n4, n16 = PERM_TILE // 4, PERM_TILE // 16
            for r in range(4):
                qkv4[a, r, pl.ds(pl.multiple_of(i * n4, n4), n4), :] = y4[r * n4:(r + 1) * n4]
            for r in range(16):
                qkv16[a, r, pl.ds(pl.multiple_of(i * n16, n16), n16), :] = y16[r * n16:(r + 1) * n16]
        return carry

    lax.fori_loop(0, SEQ // PERM_TILE, permute, 0)

    def branch(get, put, nblocks):
        head = pl.ds(0, SPAN)
        o, lse = block(get(0, head), get(1, head), get(2, head), bias_first)
        put(head, o, lse)
        if nblocks > 1:
            def body(n, carry):
                qs = pl.ds(pl.multiple_of(n * SPAN, SPAN), SPAN)
                ks = pl.ds(pl.multiple_of((n - 1) * SPAN, SPAN), 2 * SPAN)
                o, lse = block(get(0, qs), get(1, ks), get(2, ks), bias_band)
                put(qs, o, lse)
                return carry
            lax.fori_loop(1, nblocks, body, 0)

    def put1(rows, o, lse):
        o_ref[rows, :] = o
        lse1[rows, :] = lse

    branch(lambda a, rows: srcs[a][rows, :], put1, SEQ // SPAN)

    def class4(r, carry):
        def put(rows, o, lse):
            o4p[r, rows, :] = o.astype(BF16)
            lse4p[r, rows, :] = lse
        branch(lambda a, rows: qkv4[a, r, rows, :], put, SEQ // 4 // SPAN)
        return carry

    lax.fori_loop(0, 4, class4, 0)

    def class16(r, carry):
        def put(rows, o, lse):
            o16p[r, rows, :] = o.astype(BF16)
            lse16p[r, rows, :] = lse
        branch(lambda a, rows: qkv16[a, r, rows, :], put, SEQ // 16 // SPAN)
        return carry

    lax.fori_loop(0, 16, class16, 0)

    def expand(w):
        hi, lo = _split_bf16(w)
        return _dot(hi, e_ref[...]) + _dot(lo, e_ref[...])

    def combine(i, carry):
        rows = pl.ds(pl.multiple_of(i * PERM_TILE, PERM_TILE), PERM_TILE)
        n4, n16 = PERM_TILE // 4, PERM_TILE // 16
        r4 = pl.ds(pl.multiple_of(i * n4, n4), n4)
        r16 = pl.ds(pl.multiple_of(i * n16, n16), n16)
        o4 = _dot(p4t_ref[...], jnp.concatenate([o4p[r, r4, :] for r in range(4)], axis=0))
        l4 = _dot_hilo(p4t_ref[...], jnp.concatenate([lse4p[r, r4, :] for r in range(4)], axis=0))
        o16 = _dot(p16t_ref[...], jnp.concatenate([o16p[r, r16, :] for r in range(16)], axis=0))
        l16 = _dot_hilo(p16t_ref[...], jnp.concatenate([lse16p[r, r16, :] for r in range(16)], axis=0))
        l1 = lse1[rows, :]
        m = jnp.maximum(jnp.maximum(l1, l4), l16)
        e1, e4, e16 = jnp.exp(l1 - m), jnp.exp(l4 - m), jnp.exp(l16 - m)
        den = e1 + e4 + e16
        w4, w16 = expand(e4 / den), expand(e16 / den)
        o1 = o_ref[rows, :]
        o_ref[rows, :] = o1 + w4 * (o4 - o1) + w16 * (o16 - o1)
        return carry

    lax.fori_loop(0, SEQ // PERM_TILE, combine, 0)


def _perm_matrix(dil):
    n = PERM_TILE // dil
    out_row = jnp.arange(PERM_TILE)
    src = dil * (out_row % n) + out_row // n
    return (src[:, None] == jnp.arange(PERM_TILE)[None, :]).astype(BF16)


def _head_expand_matrix():
    r = jnp.arange(LANES)[:, None]
    c = jnp.arange(ATTN_WIDTH)[None, :] // HEAD_DIM
    return (r == c).astype(BF16)


def _attention(q, k, v):
    aw = ATTN_WIDTH
    seq_spec = pl.BlockSpec((None, SEQ, aw), lambda b: (b, 0, 0))
    p4, p16 = _perm_matrix(4), _perm_matrix(16)
    consts = (p4, p4.T, p16, p16.T, _head_expand_matrix())
    return pl.pallas_call(
        _attn_kernel,
        grid=(BATCH,),
        in_specs=[seq_spec] * 3 + [_const_spec(c.shape) for c in consts],
        out_specs=seq_spec,
        out_shape=jax.ShapeDtypeStruct((BATCH, SEQ, aw), F32),
        scratch_shapes=[pltpu.VMEM((3, 4, SEQ // 4, aw), BF16),
                        pltpu.VMEM((3, 16, SEQ // 16, aw), BF16),
                        pltpu.VMEM((SEQ, LANES), F32),
                        pltpu.VMEM((4, SEQ // 4, aw), BF16),
                        pltpu.VMEM((4, SEQ // 4, LANES), F32),
                        pltpu.VMEM((16, SEQ // 16, aw), BF16),
                        pltpu.VMEM((16, SEQ // 16, LANES), F32)],
        compiler_params=_params("parallel"),
        name="attention",
    )(q, k, v, *consts)


def _s5_kernel(u_ref, pf_ref, pb_ref, bre_ref, bim_ref, lre_ref, lim_ref, cre_ref, cim_ref, d_ref,
               wglu_ref, o_ref, st_re, st_im, u_buf, xr_buf, xi_buf, y_buf):
    rows = S5_CHUNK * BATCH
    sub_rows = S5_SUB * BATCH

    @pl.when(pl.program_id(0) == 0)
    def _():
        st_re[...] = jnp.zeros_like(st_re)
        st_im[...] = jnp.zeros_like(st_im)

    for tb in range(S5_CHUNK // S5_SUB):
        t = slice(tb * S5_SUB, (tb + 1) * S5_SUB)
        piece = jnp.concatenate([u_ref[b, t, :] for b in range(BATCH)], axis=0)
        u_buf[tb * sub_rows:(tb + 1) * sub_rows, :] = _dot(pf_ref[...], piece)

    for c in range(S5_NCHUNK):
        ub = u_buf[:, c * LANES:(c + 1) * LANES].astype(BF16)
        xr_buf[c] = _dot(ub, bre_ref[c])
        xi_buf[c] = _dot(ub, bim_ref[c])

    for c in range(S5_NCHUNK):
        ch = slice(c * LANES, (c + 1) * LANES)
        stl = slice(c * S5_STATE_LANES, (c + 1) * S5_STATE_LANES)
        lr = jnp.broadcast_to(lre_ref[:, stl], (BATCH, S5_STATE_LANES))
        li = jnp.broadcast_to(lim_ref[:, stl], (BATCH, S5_STATE_LANES))
        xr, xi = st_re[:, stl], st_im[:, stl]
        for s in range(S5_CHUNK):
            sl = slice(s * BATCH, (s + 1) * BATCH)
            xr, xi = (lr * xr - li * xi + xr_buf[c, sl, :], lr * xi + li * xr + xi_buf[c, sl, :])
            xr_buf[c, sl, :] = xr
            xi_buf[c, sl, :] = xi
        st_re[:, stl] = xr
        st_im[:, stl] = xi
        yc = (_dot(xr_buf[c].astype(BF16), cre_ref[c]) - _dot(xi_buf[c].astype(BF16), cim_ref[c]))
        y_buf[:, ch] = yc + d_ref[:, ch] * u_buf[:, ch]

    y = jax.nn.gelu(y_buf[...])
    z = _dot(y.astype(BF16), wglu_ref[...])
    y_buf[...] = y * jax.nn.sigmoid(z)
    for tb in range(S5_CHUNK // S5_SUB):
        back = _dot(pb_ref[...], y_buf[tb * sub_rows:(tb + 1) * sub_rows, :].astype(BF16)).astype(BF16)
        for b in range(BATCH):
            o_ref[b, tb * S5_SUB:(tb + 1) * S5_SUB, :] = back[b * S5_SUB:(b + 1) * S5_SUB]


def _s5_discretize(lam_re, lam_im, log_dt, b_re, b_im, c_re, c_im):
    dt = jnp.exp(log_dt)[:, None]
    mag = jnp.exp(lam_re * dt)
    lb_re, lb_im = mag * jnp.cos(lam_im * dt), mag * jnp.sin(lam_im * dt)
    den = lam_re * lam_re + lam_im * lam_im
    nr, ni = lb_re - 1.0, lb_im
    f_re = (nr * lam_re + ni * lam_im) / den
    f_im = (ni * lam_re - nr * lam_im) / den
    bb_re = f_re[..., None] * b_re - f_im[..., None] * b_im
    bb_im = f_re[..., None] * b_im + f_im[..., None] * b_re
    eye = jnp.eye(S5_GROUPS_PER_CHUNK, dtype=F32)
    gpc, nch = S5_GROUPS_PER_CHUNK, S5_NCHUNK

    def b_blockdiag(b):
        b = b.reshape(nch, gpc, SSM_STATE, SSM_GROUP_CH)
        m = jnp.einsum('cgph,gk->cghkp', b, eye)
        return m.reshape(nch, gpc * SSM_GROUP_CH, gpc * SSM_STATE).astype(BF16)

    def c_blockdiag(c):
        c = c.reshape(nch, gpc, SSM_GROUP_CH, SSM_STATE)
        m = jnp.einsum('cghp,gk->cgpkh', c, eye)
        return m.reshape(nch, gpc * SSM_STATE, gpc * SSM_GROUP_CH).astype(BF16)

    return (b_blockdiag(bb_re), b_blockdiag(bb_im),
            lb_re.reshape(1, SSM_GROUPS * SSM_STATE), lb_im.reshape(1, SSM_GROUPS * SSM_STATE),
            c_blockdiag(c_re), c_blockdiag(c_im))


def _s5_reorder_matrices():
    n = S5_SUB * BATCH
    out_row = jnp.arange(n)
    src = (out_row % BATCH) * S5_SUB + out_row // BATCH
    fwd = (src[:, None] == jnp.arange(n)[None, :]).astype(BF16)
    return fwd, fwd.T


def _s5(u, disc, d_skip, w_glu):
    rows = S5_CHUNK * BATCH
    blk = pl.BlockSpec((BATCH, S5_CHUNK, SSM_WIDTH), lambda i: (0, i, 0))
    args = _s5_reorder_matrices() + tuple(disc) + (d_skip, w_glu)
    return pl.pallas_call(
        _s5_kernel,
        grid=(SEQ // S5_CHUNK,),
        in_specs=[blk] + [_const_spec(a.shape) for a in args],
        out_specs=blk,
        out_shape=jax.ShapeDtypeStruct((BATCH, SEQ, SSM_WIDTH), BF16),
        scratch_shapes=[pltpu.VMEM((BATCH, SSM_GROUPS * SSM_STATE), F32),
                        pltpu.VMEM((BATCH, SSM_GROUPS * SSM_STATE), F32),
                        pltpu.VMEM((rows, SSM_WIDTH), F32),
                        pltpu.VMEM((S5_NCHUNK, rows, S5_STATE_LANES), F32),
                        pltpu.VMEM((S5_NCHUNK, rows, S5_STATE_LANES), F32),
                        pltpu.VMEM((rows, SSM_WIDTH), F32)],
        compiler_params=_params("arbitrary"),
        name="s5",
    )(u, *args)


def _outproj_kernel(attn_ref, ssm_ref, x_ref, ga_ref, gs_ref, w_ref, g2_ref, whi_ref, wlo_ref, b_ref,
                    xa_ref, bucket_ref):
    a_n = _rms(attn_ref[...], ga_ref[...]).astype(BF16)
    s_n = _rms(ssm_ref[...].astype(F32), gs_ref[...]).astype(BF16)
    y = _dot(a_n, w_ref[0:ATTN_WIDTH, :]) + _dot(s_n, w_ref[ATTN_WIDTH:, :])
    x = x_ref[...] + y
    rec, bucket = _route(_rms(x, g2_ref[...]), whi_ref[...], wlo_ref[...], b_ref[...])
    xa_ref[:, 0:D_MODEL] = x
    xa_ref[:, D_MODEL:] = rec
    bucket_ref[...] = bucket


def _outproj(attn, ssm, x, ga, gs, w, g2, whi, wlo, bias):
    aw = ATTN_WIDTH
    return pl.pallas_call(
        _outproj_kernel,
        grid=(TOKENS // TOK_TILE,),
        in_specs=[_tok_spec(aw), _tok_spec(SSM_WIDTH), _tok_spec(D_MODEL),
                  _const_spec((1, aw)), _const_spec((1, SSM_WIDTH)),
                  _const_spec((aw + SSM_WIDTH, D_MODEL)), _const_spec((1, D_MODEL)),
                  _const_spec((ROUTER_ROWS, D_MODEL)), _const_spec((ROUTER_ROWS, D_MODEL)),
                  _const_spec((ROUTER_ROWS, 1))],
        out_specs=[_tok_spec(MOE_ROW), pl.BlockSpec((1, TOK_TILE), lambda i: (0, i))],
        out_shape=[jax.ShapeDtypeStruct((TOKENS, MOE_ROW), F32),
                   jax.ShapeDtypeStruct((1, TOKENS), F32)],
        compiler_params=_params("parallel"),
        name="outproj_router",
    )(attn, ssm, x, ga, gs, w, g2, whi, wlo, bias)


ROUTER_ROWS = 32
EXPERT_ROW0 = 8
PAIRS_PER_GROUP = EXPERTS_PER_GROUP * (EXPERTS_PER_GROUP - 1) // 2
N_BUCKETS = MOE_GROUPS * PAIRS_PER_GROUP
REC_BUCKET, REC_W_LO, REC_W_HI = 0, 1, 2
TOKENS = BATCH * SEQ
PLAN_SIDE = 128
MOE_TILE = 256
MOE_NTILES = TOKENS // MOE_TILE + N_BUCKETS
MOE_SLOTS = MOE_NTILES * MOE_TILE
MOE_ROW = D_MODEL + LANES
DUMP_ROWS = 2 * MOE_TILE
X_ROWS = TOKENS + DUMP_ROWS
MOE_BUFS = 3
SPARE_TILE = MOE_NTILES + 3
INV_LEN = (MOE_NTILES + 4) * MOE_TILE
assert PLAN_SIDE * PLAN_SIDE == TOKENS and MOE_NTILES <= PLAN_SIDE
assert INV_LEN % DUMP_ROWS == 0 and SPARE_TILE * MOE_TILE < INV_LEN
assert SPARE_TILE % 2 == 1


def _route(h, whi, wlo, bias):
    h_hi, h_lo = _split_bf16(h)
    nt = (((1,), (1,)), ((), ()))
    logits = (lax.dot_general(whi, h_hi, nt, preferred_element_type=F32)
              + lax.dot_general(wlo, h_hi, nt, preferred_element_type=F32)
              + lax.dot_general(whi, h_lo, nt, preferred_element_type=F32)
              + bias)
    ng, ne = MOE_GROUPS, EXPERTS_PER_GROUP
    gl = [logits[g:g + 1, :] for g in range(ng)]
    best, grp = gl[0], jnp.zeros_like(gl[0], dtype=jnp.int32)
    for g in range(1, ng):
        better = gl[g] > best
        grp = jnp.where(better, g, grp)
        best = jnp.where(better, gl[g], best)
    g1 = 1.0 / sum(jnp.exp(x - best) for x in gl)
    sel = []
    for e in range(ne):
        acc = jnp.zeros_like(best)
        for g in range(ng):
            r = EXPERT_ROW0 + g * ne + e
            acc = jnp.where(grp == g, logits[r:r + 1, :], acc)
        sel.append(acc)

    def first_argmax(vals):
        bv, bi = vals[0], jnp.zeros_like(grp)
        for e in range(1, ne):
            better = vals[e] > bv
            bi = jnp.where(better, e, bi)
            bv = jnp.where(better, vals[e], bv)
        return bv, bi

    v1, i1 = first_argmax(sel)
    v2, i2 = first_argmax([jnp.where(i1 == e, -jnp.inf, sel[e]) for e in range(ne)])
    e2 = jnp.exp(v2 - v1)
    w1 = g1 / (1.0 + e2)
    w2 = g1 * e2 / (1.0 + e2)
    first_is_low = i1 < i2
    lo = jnp.where(first_is_low, i1, i2)
    hi = jnp.where(first_is_low, i2, i1)
    pair = jnp.where(lo == 0, 0, jnp.where(lo == 1, 3, 5)) + hi - lo - 1
    bucket = (grp * PAIRS_PER_GROUP + pair).astype(F32)
    w_lo = jnp.where(first_is_low, w1, w2)
    w_hi = jnp.where(first_is_low, w2, w1)
    tokens = logits.shape[1]
    rowid = lax.broadcasted_iota(jnp.int32, (LANES, tokens), 0)
    table = jnp.where(rowid == REC_BUCKET, bucket,
                      jnp.where(rowid == REC_W_LO, w_lo, jnp.where(rowid == REC_W_HI, w_hi, 0.0)))
    return table.T, bucket


def _plan_kernel(bucket_ref, pos_ref, tile_bucket_ref):
    n = PLAN_SIDE
    bucket = bucket_ref[...]
    r = lax.broadcasted_iota(jnp.int32, (n, n), 0)
    c = lax.broadcasted_iota(jnp.int32, (n, n), 1)
    before_in_row = (r < c).astype(BF16)
    rows_before = (c < r).astype(BF16)
    ones = jnp.ones((n, n), BF16)
    tile_start = (c * MOE_TILE).astype(F32)
    pos = jnp.zeros((n, n), F32)
    base = jnp.zeros((n, n), F32)
    ended = jnp.zeros((n, n), F32)
    for k in range(N_BUCKETS):
        member = bucket == float(k)
        mb = member.astype(BF16)
        in_row = _dot(mb, before_in_row)
        row_count = _dot(mb, ones).astype(BF16)
        rank = _dot(rows_before, row_count) + in_row
        total = _dot(ones, row_count)
        pos = jnp.where(member, base + rank, pos)
        base = base + jnp.ceil(total * (1.0 / MOE_TILE)) * MOE_TILE
        ended = ended + (tile_start >= base).astype(F32)
    pos_ref[...] = pos.astype(jnp.int32)
    tile_bucket_ref[...] = ended.astype(jnp.int32)


def _plan(bucket):
    n = PLAN_SIDE
    pos, tile_bucket = pl.pallas_call(
        _plan_kernel,
        out_shape=[jax.ShapeDtypeStruct((n, n), jnp.int32)] * 2,
        name="moe_plan",
    )(bucket.reshape(n, n))
    return pos.reshape(TOKENS), tile_bucket[0, :MOE_NTILES]


def _invert_kernel(pos_ref, inv_ref):
    def init(blk, carry):
        for r in range(DUMP_ROWS):
            inv_ref[blk * DUMP_ROWS + r] = TOKENS + r
        return carry

    def place(t, carry):
        inv_ref[pos_ref[t]] = t
        return carry

    lax.fori_loop(0, INV_LEN // DUMP_ROWS, init, 0)
    lax.fori_loop(0, TOKENS, place, 0, unroll=16)


def _invert(pos):
    smem = pl.BlockSpec(memory_space=pltpu.SMEM)
    return pl.pallas_call(
        _invert_kernel,
        in_specs=[smem],
        out_specs=smem,
        out_shape=jax.ShapeDtypeStruct((INV_LEN,), jnp.int32),
        name="moe_invert",
    )(pos)


def _row_copy(src, src_row, dst, dst_row, sem):
    return pltpu.make_async_copy(src.at[pl.ds(src_row, 1), :], dst.at[pl.ds(dst_row, 1), :], sem)


def _experts_kernel(inv_ref, e_lo_ref, e_hi_ref, nused_ref, xa_ref, g_ref, fg_ref,
                    wg_lo, wu_lo, wd_lo, wg_hi, wu_hi, wd_hi, xo_ref,
                    xin, yout, gsem, ssem, fsem, *, final_norm):
    i = pl.program_id(0)
    n_used = nused_ref[0]
    s = lax.rem(i, MOE_BUFS)
    s_prev = lax.rem(i + 2, MOE_BUFS)
    s_next = lax.rem(i + 1, MOE_BUFS)

    def start_gather(tile, slot):
        for r in range(MOE_TILE):
            src = jnp.minimum(inv_ref[tile * MOE_TILE + r], TOKENS - 1)
            _row_copy(xa_ref, src, xin.at[slot], r, gsem.at[slot]).start(priority=r % 2)

    def start_scatter(tile, slot):
        for r in range(MOE_TILE):
            _row_copy(yout.at[slot], r, xo_ref, inv_ref[tile * MOE_TILE + r],
                      ssem.at[slot]).start(priority=r % 2)

    def wait_gather(slot):
        pltpu.make_async_copy(xa_ref.at[pl.ds(0, MOE_TILE), :], xin.at[slot], gsem.at[slot]).wait()

    def wait_scatter(slot):
        pltpu.make_async_copy(yout.at[slot], xo_ref.at[pl.ds(0, MOE_TILE), :], ssem.at[slot]).wait()

    @pl.when(i == 0)
    def _():
        yout[MOE_BUFS - 1] = jnp.zeros((MOE_TILE, D_MODEL), F32)
        for half in range(DUMP_ROWS // MOE_TILE):
            fill = pltpu.make_async_copy(
                yout.at[MOE_BUFS - 1], xo_ref.at[pl.ds(TOKENS + half * MOE_TILE, MOE_TILE), :], fsem)
            fill.start()
            fill.wait()
        start_gather(0, 0)
        start_gather(1, 1)

    @pl.when(i <= n_used)
    def _():
        wait_gather(s)

        @pl.when(i >= 2)
        def _():
            wait_scatter(s)

        start_gather(i + 2, s_prev)
        start_scatter(jnp.where(i >= 1, i - 1, SPARE_TILE), s_prev)

        xt = xin[s]
        x_rows = xt[:, 0:D_MODEL]
        rec = xt[:, D_MODEL:]
        h = _rms(x_rows, g_ref[...]).astype(BF16)

        def expert(wg, wu, wd, lane):
            hg = _dot(h, wg[...].astype(BF16))
            hu = _dot(h, wu[...].astype(BF16))
            act = jax.nn.silu(hg) * hu * rec[:, lane:lane + 1]
            return _dot(act.astype(BF16), wd[...].astype(BF16))

        out = x_rows + expert(wg_lo, wu_lo, wd_lo, REC_W_LO) + expert(wg_hi, wu_hi, wd_hi, REC_W_HI)
        yout[s] = _rms(out, fg_ref[...]) if final_norm else out

        @pl.when(i == n_used)
        def _():
            wait_gather(s_next)
            wait_gather(s_prev)
            wait_scatter(s_next)
            wait_scatter(s_prev)


def _experts(layer, inv, e_lo, e_hi, n_used, xa, g, final_g, w_gate, w_up, w_down):
    final_norm = final_g is not None
    if not final_norm:
        final_g = jnp.ones((1, D_MODEL), F32)
    up = lambda pick: pl.BlockSpec((None, None, D_MODEL, EXPERT_FF),
                                   lambda i, inv, lo, hi, nu: (layer, pick(lo, hi)[i], 0, 0))
    down = lambda pick: pl.BlockSpec((None, None, EXPERT_FF, D_MODEL),
                                     lambda i, inv, lo, hi, nu: (layer, pick(lo, hi)[i], 0, 0))
    first, second = (lambda lo, hi: lo), (lambda lo, hi: hi)
    grid_spec = pltpu.PrefetchScalarGridSpec(
        num_scalar_prefetch=4,
        grid=(MOE_NTILES + 1,),
        in_specs=[pl.BlockSpec(memory_space=pl.ANY), _const_spec((1, D_MODEL)), _const_spec((1, D_MODEL)),
                  up(first), up(first), down(first), up(second), up(second), down(second)],
        out_specs=pl.BlockSpec(memory_space=pl.ANY),
        scratch_shapes=[pltpu.VMEM((MOE_BUFS, MOE_TILE, MOE_ROW), F32),
                        pltpu.VMEM((MOE_BUFS, MOE_TILE, D_MODEL), F32),
                        pltpu.SemaphoreType.DMA((MOE_BUFS,)),
                        pltpu.SemaphoreType.DMA((MOE_BUFS,)),
                        pltpu.SemaphoreType.DMA(())])
    return pl.pallas_call(
        functools.partial(_experts_kernel, final_norm=final_norm),
        grid_spec=grid_spec,
        out_shape=jax.ShapeDtypeStruct((X_ROWS, D_MODEL), F32),
        compiler_params=_params("arbitrary"),
        name="moe_experts_final" if final_norm else "moe_experts",
    )(inv, e_lo, e_hi, n_used, xa, g, final_g, w_gate, w_up, w_down, w_gate, w_up, w_down)


def _bucket_experts():
    pairs = [(a, b) for a in range(EXPERTS_PER_GROUP) for b in range(a + 1, EXPERTS_PER_GROUP)]
    lo = [g * EXPERTS_PER_GROUP + a for g in range(MOE_GROUPS) for a, _ in pairs]
    hi = [g * EXPERTS_PER_GROUP + b for g in range(MOE_GROUPS) for _, b in pairs]
    return jnp.array(lo, jnp.int32), jnp.array(hi, jnp.int32)


def _routed_moe(layer, xa, bucket, g, w_gate, w_up, w_down, final_g=None):
    pos, tile_bucket = _plan(bucket)
    n_used = jnp.sum((tile_bucket < N_BUCKETS).astype(jnp.int32)).reshape(1)
    tile_bucket = jnp.minimum(jnp.concatenate([tile_bucket, tile_bucket[-1:]]), N_BUCKETS - 1)
    lo, hi = _bucket_experts()
    return _experts(layer, _invert(pos), lo[tile_bucket], hi[tile_bucket], n_used, xa, g, final_g,
                    w_gate, w_up, w_down)


def kernel(x, ln1_g, w_in, lam_re, lam_im, log_dt, b_re, b_im, c_re, c_im, d_skip, w_glu,
           gn_attn, gn_ssm, w_out, ln2_g, w_router_grp, b_router_grp, w_router_exp,
           b_router_exp, w_gate, w_up, w_down, final_g):
    assert x.shape == (BATCH, SEQ, D_MODEL)
    ng = MOE_GROUPS
    x = x.reshape(TOKENS, D_MODEL)

    for l in range(DEPTH):
        q, k, v, u = _inproj(x, ln1_g[l][None, :], w_in[l].astype(BF16))
        attn = _attention(q, k, v).reshape(TOKENS, ATTN_WIDTH)
        disc = _s5_discretize(lam_re[l], lam_im[l], log_dt[l], b_re[l], b_im[l], c_re[l], c_im[l])
        ssm = _s5(u, disc, d_skip[l].reshape(1, SSM_WIDTH), w_glu[l].astype(BF16))
        ssm = ssm.reshape(TOKENS, SSM_WIDTH)

        w_r = jnp.zeros((ROUTER_ROWS, D_MODEL), F32)
        w_r = w_r.at[0:ng].set(w_router_grp[l].T)
        w_r = w_r.at[EXPERT_ROW0:EXPERT_ROW0 + N_EXPERTS].set(
            jnp.transpose(w_router_exp[l], (0, 2, 1)).reshape(N_EXPERTS, D_MODEL))
        b_r = jnp.zeros((ROUTER_ROWS, 1), F32)
        b_r = b_r.at[0:ng, 0].set(b_router_grp[l])
        b_r = b_r.at[EXPERT_ROW0:EXPERT_ROW0 + N_EXPERTS, 0].set(b_router_exp[l].reshape(N_EXPERTS))
        w_r_hi, w_r_lo = _split_bf16(w_r)
        g2 = ln2_g[l][None, :]
        xa, bucket = _outproj(attn, ssm, x, gn_attn[l][None, :], gn_ssm[l][None, :],
                              w_out[l].astype(BF16), g2, w_r_hi, w_r_lo, b_r)
        x = _routed_moe(l, xa, bucket, g2, w_gate, w_up, w_down,
                        final_g[None, :] if l == DEPTH - 1 else None)
    return x[:TOKENS].reshape(BATCH, SEQ, D_MODEL)
```

```python
import functools
import math

import jax
import jax.numpy as jnp
from jax import lax
from jax.experimental import pallas as pl
from jax.experimental.pallas import tpu as pltpu

D_MODEL = 1024
BATCH = 8
SEQ = 2048
DEPTH = 4
ATTN_WIDTH = 512
HEAD_DIM = 64
ATTN_HEADS = 8
DILATIONS = (1, 4, 16)
SPAN = 128
SSM_WIDTH = 512
SSM_GROUP_CH = 16
SSM_GROUPS = 32
SSM_STATE = 64
MOE_GROUPS = 4
EXPERTS_PER_GROUP = 4
N_EXPERTS = 16
EXPERT_FF = 256
RMS_EPS = 1e-6

LANES = 128
BF16_ROWS = 16
VMEM_LIMIT_BYTES = 56 * 1024 * 1024

TOK_TILE = 512
PERM_TILE = 256
S5_CHUNK = 128
S5_GROUPS_PER_CHUNK = 8
S5_NCHUNK = SSM_GROUPS // S5_GROUPS_PER_CHUNK
S5_STATE_LANES = S5_GROUPS_PER_CHUNK * SSM_STATE
S5_SUB = BF16_ROWS

F32 = jnp.float32
BF16 = jnp.bfloat16


def _params(*sem):
    return pltpu.CompilerParams(dimension_semantics=sem, vmem_limit_bytes=VMEM_LIMIT_BYTES)


def _rms(x, g):
    return x * lax.rsqrt(jnp.mean(x * x, axis=-1, keepdims=True) + RMS_EPS) * g


def _split_bf16(a):
    hi = a.astype(BF16)
    lo = (a - hi.astype(F32)).astype(BF16)
    return hi, lo


def _dot(a, b):
    return jnp.dot(a, b, preferred_element_type=F32)


def _dot_hilo(p, a):
    hi, lo = _split_bf16(a)
    return _dot(p, hi) + _dot(p, lo)


def _tok_spec(width):
    return pl.BlockSpec((TOK_TILE, width), lambda i: (i, 0))


def _const_spec(shape):
    return pl.BlockSpec(shape, lambda *_: (0,) * len(shape))


X_SUB = D_MODEL // LANES
_RESIDUAL_SPEC = pl.BlockSpec((TOK_TILE * X_SUB, LANES), lambda i: (i, 0))


def _load_rows(ref, tokens):
    return jnp.concatenate([ref[pl.ds(c, tokens, stride=X_SUB), :] for c in range(X_SUB)], axis=1)


def _store_rows(ref, tokens, value):
    for c in range(X_SUB):
        ref[pl.ds(c, tokens, stride=X_SUB), :] = value[:, c * LANES:(c + 1) * LANES]


def _inproj_kernel(x_ref, g_ref, w_ref, q_ref, k_ref, v_ref, u_ref):
    h = _rms(_load_rows(x_ref, TOK_TILE), g_ref[...]).astype(BF16)
    p = _dot(h, w_ref[...])
    aw = ATTN_WIDTH
    q_ref[...] = (p[:, :aw] * (1.0 / math.sqrt(HEAD_DIM))).astype(BF16)
    k_ref[...] = p[:, aw:2 * aw].astype(BF16)
    v_ref[...] = p[:, 2 * aw:3 * aw].astype(BF16)
    u_ref[...] = p[:, 3 * aw:].astype(BF16)


def _inproj(x, g, w):
    aw = ATTN_WIDTH
    out = jax.ShapeDtypeStruct((TOKENS, aw), BF16)
    outs = pl.pallas_call(
        _inproj_kernel,
        grid=(TOKENS // TOK_TILE,),
        in_specs=[_RESIDUAL_SPEC, _const_spec((1, D_MODEL)),
                  _const_spec((D_MODEL, 3 * aw + SSM_WIDTH))],
        out_specs=[_tok_spec(aw)] * 4,
        out_shape=[out] * 4,
        compiler_params=_params("parallel"),
        name="inproj",
    )(x, g, w)
    return [o.reshape(BATCH, SEQ, aw) for o in outs]


def _attn_block(q, kk, vv, bias, low_half, lane):
    heads = range(ATTN_HEADS)
    nt = (((1,), (1,)), ((), ()))
    scores = []
    for h in heads:
        sl = slice((h // 2) * LANES, (h // 2 + 1) * LANES)
        keep = low_half if h % 2 == 0 else jnp.logical_not(low_half)
        qm = jnp.where(keep, q[:, sl], jnp.zeros_like(q[:, sl]))
        scores.append(lax.dot_general(qm, kk[:, sl], nt, preferred_element_type=F32) + bias)
    probs, dens, lses = [], [], []
    for h in heads:
        m = jnp.max(scores[h], axis=-1, keepdims=True)
        p = jnp.exp(scores[h] - m)
        den = jnp.sum(p, axis=-1, keepdims=True)
        probs.append(p.astype(BF16))
        dens.append(den)
        lses.append(m + jnp.log(den))
    pv = [_dot(probs[h], vv[:, (h // 2) * LANES:(h // 2 + 1) * LANES]) / dens[h] for h in heads]
    lse_tile = jnp.zeros((SPAN, LANES), F32)
    for h in heads:
        lse_tile = jnp.where(lane == h, lses[h], lse_tile)
    outs = [jnp.where(low_half, pv[2 * j], pv[2 * j + 1]) for j in range(ATTN_HEADS // 2)]
    return jnp.concatenate(outs, axis=-1), lse_tile


def _attn_kernel(q_ref, k_ref, v_ref, p4_ref, p4t_ref, p16_ref, p16t_ref, e_ref, o_ref,
                 qkv4, qkv16, lse1, o4p, lse4p, o16p, lse16p):
    lane = lax.broadcasted_iota(jnp.int32, (SPAN, LANES), 1)
    low_half = lane < HEAD_DIM
    row0 = lax.broadcasted_iota(jnp.int32, (SPAN, SPAN), 0)
    col0 = lax.broadcasted_iota(jnp.int32, (SPAN, SPAN), 1)
    bias_first = jnp.where(col0 <= row0, 0.0, -jnp.inf).astype(F32)
    row = lax.broadcasted_iota(jnp.int32, (SPAN, 2 * SPAN), 0)
    col = lax.broadcasted_iota(jnp.int32, (SPAN, 2 * SPAN), 1)
    bias_band = jnp.where((col >= row) & (col <= row + SPAN), 0.0, -jnp.inf).astype(F32)
    block = functools.partial(_attn_block, low_half=low_half, lane=lane)
    srcs = (q_ref, k_ref, v_ref)

    def permute(i, carry):
        rows = pl.ds(pl.multiple_of(i * PERM_TILE, PERM_TILE), PERM_TILE)
        for a, src in enumerate(srcs):
            x = src[rows, :]
            y4 = _dot(p4_ref[...], x).astype(BF16)
            y16 = _dot(p16_ref[...], x).astype(BF16)
            n4, n16 = PERM_TILE // 4, PERM_TILE // 16
            for r in range(4):
                qkv4[a, r, pl.ds(pl.multiple_of(i * n4, n4), n4), :] = y4[r * n4:(r + 1) * n4]
            for r in range(16):
                qkv16[a, r, pl.ds(pl.multiple_of(i * n16, n16), n16), :] = y16[r * n16:(r + 1) * n16]
        return carry

    lax.fori_loop(0, SEQ // PERM_TILE, permute, 0)

    def branch(get, put, nblocks):
        head = pl.ds(0, SPAN)
        o, lse = block(get(0, head), get(1, head), get(2, head), bias_first)
        put(head, o, lse)
        if nblocks > 1:
            def body(n, carry):
                qs = pl.ds(pl.multiple_of(n * SPAN, SPAN), SPAN)
                ks = pl.ds(pl.multiple_of((n - 1) * SPAN, SPAN), 2 * SPAN)
                o, lse = block(get(0, qs), get(1, ks), get(2, ks), bias_band)
                put(qs, o, lse)
                return carry
            lax.fori_loop(1, nblocks, body, 0)

    def put1(rows, o, lse):
        o_ref[rows, :] = o
        lse1[rows, :] = lse

    branch(lambda a, rows: srcs[a][rows, :], put1, SEQ // SPAN)

    def class4(r, carry):
        def put(rows, o, lse):
            o4p[r, rows, :] = o.astype(BF16)
            lse4p[r, rows, :] = lse
        branch(lambda a, rows: qkv4[a, r, rows, :], put, SEQ // 4 // SPAN)
        return carry

    lax.fori_loop(0, 4, class4, 0)

    def class16(r, carry):
        def put(rows, o, lse):
            o16p[r, rows, :] = o.astype(BF16)
            lse16p[r, rows, :] = lse
        branch(lambda a, rows: qkv16[a, r, rows, :], put, SEQ // 16 // SPAN)
        return carry

    lax.fori_loop(0, 16, class16, 0)

    def expand(w):
        hi, lo = _split_bf16(w)
        return _dot(hi, e_ref[...]) + _dot(lo, e_ref[...])

    def combine(i, carry):
        rows = pl.ds(pl.multiple_of(i * PERM_TILE, PERM_TILE), PERM_TILE)
        n4, n16 = PERM_TILE // 4, PERM_TILE // 16
        r4 = pl.ds(pl.multiple_of(i * n4, n4), n4)
        r16 = pl.ds(pl.multiple_of(i * n16, n16), n16)
        o4 = _dot(p4t_ref[...], jnp.concatenate([o4p[r, r4, :] for r in range(4)], axis=0))
        l4 = _dot_hilo(p4t_ref[...], jnp.concatenate([lse4p[r, r4, :] for r in range(4)], axis=0))
        o16 = _dot(p16t_ref[...], jnp.concatenate([o16p[r, r16, :] for r in range(16)], axis=0))
        l16 = _dot_hilo(p16t_ref[...], jnp.concatenate([lse16p[r, r16, :] for r in range(16)], axis=0))
        l1 = lse1[rows, :]
        m = jnp.maximum(jnp.maximum(l1, l4), l16)
        e1, e4, e16 = jnp.exp(l1 - m), jnp.exp(l4 - m), jnp.exp(l16 - m)
        den = e1 + e4 + e16
        w4, w16 = expand(e4 / den), expand(e16 / den)
        o1 = o_ref[rows, :]
        o_ref[rows, :] = o1 + w4 * (o4 - o1) + w16 * (o16 - o1)
        return carry

    lax.fori_loop(0, SEQ // PERM_TILE, combine, 0)


def _perm_matrix(dil):
    n = PERM_TILE // dil
    out_row = jnp.arange(PERM_TILE)
    src = dil * (out_row % n) + out_row // n
    return (src[:, None] == jnp.arange(PERM_TILE)[None, :]).astype(BF16)


def _head_expand_matrix():
    r = jnp.arange(LANES)[:, None]
    c = jnp.arange(ATTN_WIDTH)[None, :] // HEAD_DIM
    return (r == c).astype(BF16)


def _attention(q, k, v):
    aw = ATTN_WIDTH
    seq_spec = pl.BlockSpec((None, SEQ, aw), lambda b: (b, 0, 0))
    p4, p16 = _perm_matrix(4), _perm_matrix(16)
    consts = (p4, p4.T, p16, p16.T, _head_expand_matrix())
    return pl.pallas_call(
        _attn_kernel,
        grid=(BATCH,),
        in_specs=[seq_spec] * 3 + [_const_spec(c.shape) for c in consts],
        out_specs=seq_spec,
        out_shape=jax.ShapeDtypeStruct((BATCH, SEQ, aw), F32),
        scratch_shapes=[pltpu.VMEM((3, 4, SEQ // 4, aw), BF16),
                        pltpu.VMEM((3, 16, SEQ // 16, aw), BF16),
                        pltpu.VMEM((SEQ, LANES), F32),
                        pltpu.VMEM((4, SEQ // 4, aw), BF16),
                        pltpu.VMEM((4, SEQ // 4, LANES), F32),
                        pltpu.VMEM((16, SEQ // 16, aw), BF16),
                        pltpu.VMEM((16, SEQ // 16, LANES), F32)],
        compiler_params=_params("parallel"),
        name="attention",
    )(q, k, v, *consts)


def _s5_kernel(u_ref, pf_ref, pb_ref, bre_ref, bim_ref, lre_ref, lim_ref, cre_ref, cim_ref, d_ref,
               wglu_ref, o_ref, st_re, st_im, u_buf, xr_buf, xi_buf, y_buf):
    rows = S5_CHUNK * BATCH
    sub_rows = S5_SUB * BATCH

    @pl.when(pl.program_id(0) == 0)
    def _():
        st_re[...] = jnp.zeros_like(st_re)
        st_im[...] = jnp.zeros_like(st_im)

    for tb in range(S5_CHUNK // S5_SUB):
        t = slice(tb * S5_SUB, (tb + 1) * S5_SUB)
        piece = jnp.concatenate([u_ref[b, t, :] for b in range(BATCH)], axis=0)
        u_buf[tb * sub_rows:(tb + 1) * sub_rows, :] = _dot(pf_ref[...], piece)

    for c in range(S5_NCHUNK):
        ub = u_buf[:, c * LANES:(c + 1) * LANES].astype(BF16)
        xr_buf[c] = _dot(ub, bre_ref[c])
        xi_buf[c] = _dot(ub, bim_ref[c])

    for c in range(S5_NCHUNK):
        ch = slice(c * LANES, (c + 1) * LANES)
        stl = slice(c * S5_STATE_LANES, (c + 1) * S5_STATE_LANES)
        lr = jnp.broadcast_to(lre_ref[:, stl], (BATCH, S5_STATE_LANES))
        li = jnp.broadcast_to(lim_ref[:, stl], (BATCH, S5_STATE_LANES))
        xr, xi = st_re[:, stl], st_im[:, stl]
        for s in range(S5_CHUNK):
            sl = slice(s * BATCH, (s + 1) * BATCH)
            xr, xi = (lr * xr - li * xi + xr_buf[c, sl, :], lr * xi + li * xr + xi_buf[c, sl, :])
            xr_buf[c, sl, :] = xr
            xi_buf[c, sl, :] = xi
        st_re[:, stl] = xr
        st_im[:, stl] = xi
        yc = (_dot(xr_buf[c].astype(BF16), cre_ref[c]) - _dot(xi_buf[c].astype(BF16), cim_ref[c]))
        y_buf[:, ch] = yc + d_ref[:, ch] * u_buf[:, ch]

    y = jax.nn.gelu(y_buf[...])
    z = _dot(y.astype(BF16), wglu_ref[...])
    y_buf[...] = y * jax.nn.sigmoid(z)
    for tb in range(S5_CHUNK // S5_SUB):
        back = _dot(pb_ref[...], y_buf[tb * sub_rows:(tb + 1) * sub_rows, :].astype(BF16)).astype(BF16)
        for b in range(BATCH):
            o_ref[b, tb * S5_SUB:(tb + 1) * S5_SUB, :] = back[b * S5_SUB:(b + 1) * S5_SUB]


def _s5_discretize(lam_re, lam_im, log_dt, b_re, b_im, c_re, c_im):
    dt = jnp.exp(log_dt)[:, None]
    mag = jnp.exp(lam_re * dt)
    lb_re, lb_im = mag * jnp.cos(lam_im * dt), mag * jnp.sin(lam_im * dt)
    den = lam_re * lam_re + lam_im * lam_im
    nr, ni = lb_re - 1.0, lb_im
    f_re = (nr * lam_re + ni * lam_im) / den
    f_im = (ni * lam_re - nr * lam_im) / den
    bb_re = f_re[..., None] * b_re - f_im[..., None] * b_im
    bb_im = f_re[..., None] * b_im + f_im[..., None] * b_re
    eye = jnp.eye(S5_GROUPS_PER_CHUNK, dtype=F32)
    gpc, nch = S5_GROUPS_PER_CHUNK, S5_NCHUNK

    def b_blockdiag(b):
        b = b.reshape(nch, gpc, SSM_STATE, SSM_GROUP_CH)
        m = jnp.einsum('cgph,gk->cghkp', b, eye)
        return m.reshape(nch, gpc * SSM_GROUP_CH, gpc * SSM_STATE).astype(BF16)

    def c_blockdiag(c):
        c = c.reshape(nch, gpc, SSM_GROUP_CH, SSM_STATE)
        m = jnp.einsum('cghp,gk->cgpkh', c, eye)
        return m.reshape(nch, gpc * SSM_STATE, gpc * SSM_GROUP_CH).astype(BF16)

    return (b_blockdiag(bb_re), b_blockdiag(bb_im),
            lb_re.reshape(1, SSM_GROUPS * SSM_STATE), lb_im.reshape(1, SSM_GROUPS * SSM_STATE),
            c_blockdiag(c_re), c_blockdiag(c_im))


def _s5_reorder_matrices():
    n = S5_SUB * BATCH
    out_row = jnp.arange(n)
    src = (out_row % BATCH) * S5_SUB + out_row // BATCH
    fwd = (src[:, None] == jnp.arange(n)[None, :]).astype(BF16)
    return fwd, fwd.T


def _s5(u, disc, d_skip, w_glu):
    rows = S5_CHUNK * BATCH
    blk = pl.BlockSpec((BATCH, S5_CHUNK, SSM_WIDTH), lambda i: (0, i, 0))
    args = _s5_reorder_matrices() + tuple(disc) + (d_skip, w_glu)
    return pl.pallas_call(
        _s5_kernel,
        grid=(SEQ // S5_CHUNK,),
        in_specs=[blk] + [_const_spec(a.shape) for a in args],
        out_specs=blk,
        out_shape=jax.ShapeDtypeStruct((BATCH, SEQ, SSM_WIDTH), BF16),
        scratch_shapes=[pltpu.VMEM((BATCH, SSM_GROUPS * SSM_STATE), F32),
                        pltpu.VMEM((BATCH, SSM_GROUPS * SSM_STATE), F32),
                        pltpu.VMEM((rows, SSM_WIDTH), F32),
                        pltpu.VMEM((S5_NCHUNK, rows, S5_STATE_LANES), F32),
                        pltpu.VMEM((S5_NCHUNK, rows, S5_STATE_LANES), F32),
                        pltpu.VMEM((rows, SSM_WIDTH), F32)],
        compiler_params=_params("arbitrary"),
        name="s5",
    )(u, *args)


def _outproj_kernel(attn_ref, ssm_ref, x_ref, ga_ref, gs_ref, w_ref, g2_ref, whi_ref, wlo_ref, b_ref,
                    xo_ref, bucket_ref):
    a_n = _rms(attn_ref[...], ga_ref[...]).astype(BF16)
    s_n = _rms(ssm_ref[...].astype(F32), gs_ref[...]).astype(BF16)
    y = _dot(a_n, w_ref[0:ATTN_WIDTH, :]) + _dot(s_n, w_ref[ATTN_WIDTH:, :])
    x = _load_rows(x_ref, TOK_TILE) + y
    _, bucket = _route(_rms(x, g2_ref[...]), whi_ref[...], wlo_ref[...], b_ref[...])
    _store_rows(xo_ref, TOK_TILE, x)
    bucket_ref[...] = bucket


def _outproj(attn, ssm, x, ga, gs, w, g2, whi, wlo, bias):
    aw = ATTN_WIDTH
    return pl.pallas_call(
        _outproj_kernel,
        grid=(TOKENS // TOK_TILE,),
        in_specs=[_tok_spec(aw), _tok_spec(SSM_WIDTH), _RESIDUAL_SPEC,
                  _const_spec((1, aw)), _const_spec((1, SSM_WIDTH)),
                  _const_spec((aw + SSM_WIDTH, D_MODEL)), _const_spec((1, D_MODEL)),
                  _const_spec((ROUTER_ROWS, D_MODEL)), _const_spec((ROUTER_ROWS, D_MODEL)),
                  _const_spec((ROUTER_ROWS, 1))],
        out_specs=[_RESIDUAL_SPEC, pl.BlockSpec((1, TOK_TILE), lambda i: (0, i))],
        out_shape=[jax.ShapeDtypeStruct((TOKENS * X_SUB, LANES), F32),
                   jax.ShapeDtypeStruct((1, TOKENS), F32)],
        compiler_params=_params("parallel"),
        name="outproj_router",
    )(attn, ssm, x, ga, gs, w, g2, whi, wlo, bias)


ROUTER_ROWS = 32
EXPERT_ROW0 = 8
PAIRS_PER_GROUP = EXPERTS_PER_GROUP * (EXPERTS_PER_GROUP - 1) // 2
N_BUCKETS = MOE_GROUPS * PAIRS_PER_GROUP
REC_BUCKET, REC_W_LO, REC_W_HI = 0, 1, 2
TOKENS = BATCH * SEQ
PLAN_SIDE = 128
MOE_TILE = 256
MOE_NTILES = TOKENS // MOE_TILE + N_BUCKETS
MOE_SLOTS = MOE_NTILES * MOE_TILE
DUMP_ROWS = 2 * MOE_TILE
X_ROWS = TOKENS + DUMP_ROWS
MOE_BUFS = 3
SPARE_TILE = MOE_NTILES + 3
INV_LEN = (MOE_NTILES + 4) * MOE_TILE
assert PLAN_SIDE * PLAN_SIDE == TOKENS and MOE_NTILES <= PLAN_SIDE
assert INV_LEN % DUMP_ROWS == 0 and SPARE_TILE * MOE_TILE < INV_LEN
assert SPARE_TILE % 2 == 1


def _route(h, whi, wlo, bias):
    h_hi, h_lo = _split_bf16(h)
    nt = (((1,), (1,)), ((), ()))
    logits = (lax.dot_general(whi, h_hi, nt, preferred_element_type=F32)
              + lax.dot_general(wlo, h_hi, nt, preferred_element_type=F32)
              + lax.dot_general(whi, h_lo, nt, preferred_element_type=F32)
              + bias)
    ng, ne = MOE_GROUPS, EXPERTS_PER_GROUP
    gl = [logits[g:g + 1, :] for g in range(ng)]
    best, grp = gl[0], jnp.zeros_like(gl[0], dtype=jnp.int32)
    for g in range(1, ng):
        better = gl[g] > best
        grp = jnp.where(better, g, grp)
        best = jnp.where(better, gl[g], best)
    g1 = 1.0 / sum(jnp.exp(x - best) for x in gl)
    sel = []
    for e in range(ne):
        acc = jnp.zeros_like(best)
        for g in range(ng):
            r = EXPERT_ROW0 + g * ne + e
            acc = jnp.where(grp == g, logits[r:r + 1, :], acc)
        sel.append(acc)

    def first_argmax(vals):
        bv, bi = vals[0], jnp.zeros_like(grp)
        for e in range(1, ne):
            better = vals[e] > bv
            bi = jnp.where(better, e, bi)
            bv = jnp.where(better, vals[e], bv)
        return bv, bi

    v1, i1 = first_argmax(sel)
    v2, i2 = first_argmax([jnp.where(i1 == e, -jnp.inf, sel[e]) for e in range(ne)])
    e2 = jnp.exp(v2 - v1)
    w1 = g1 / (1.0 + e2)
    w2 = g1 * e2 / (1.0 + e2)
    first_is_low = i1 < i2
    lo = jnp.where(first_is_low, i1, i2)
    hi = jnp.where(first_is_low, i2, i1)
    pair = jnp.where(lo == 0, 0, jnp.where(lo == 1, 3, 5)) + hi - lo - 1
    bucket = (grp * PAIRS_PER_GROUP + pair).astype(F32)
    w_lo = jnp.where(first_is_low, w1, w2)
    w_hi = jnp.where(first_is_low, w2, w1)
    tokens = logits.shape[1]
    rowid = lax.broadcasted_iota(jnp.int32, (LANES, tokens), 0)
    table = jnp.where(rowid == REC_BUCKET, bucket,
                      jnp.where(rowid == REC_W_LO, w_lo, jnp.where(rowid == REC_W_HI, w_hi, 0.0)))
    return table.T, bucket


def _plan_kernel(bucket_ref, pos_ref, tile_bucket_ref):
    n = PLAN_SIDE
    bucket = bucket_ref[...]
    r = lax.broadcasted_iota(jnp.int32, (n, n), 0)
    c = lax.broadcasted_iota(jnp.int32, (n, n), 1)
    before_in_row = (r < c).astype(BF16)
    rows_before = (c < r).astype(BF16)
    ones = jnp.ones((n, n), BF16)
    tile_start = (c * MOE_TILE).astype(F32)
    pos = jnp.zeros((n, n), F32)
    base = jnp.zeros((n, n), F32)
    ended = jnp.zeros((n, n), F32)
    for k in range(N_BUCKETS):
        member = bucket == float(k)
        mb = member.astype(BF16)
        in_row = _dot(mb, before_in_row)
        row_count = _dot(mb, ones).astype(BF16)
        rank = _dot(rows_before, row_count) + in_row
        total = _dot(ones, row_count)
        pos = jnp.where(member, base + rank, pos)
        base = base + jnp.ceil(total * (1.0 / MOE_TILE)) * MOE_TILE
        ended = ended + (tile_start >= base).astype(F32)
    pos_ref[...] = pos.astype(jnp.int32)
    tile_bucket_ref[...] = ended.astype(jnp.int32)


def _plan(bucket):
    n = PLAN_SIDE
    pos, tile_bucket = pl.pallas_call(
        _plan_kernel,
        out_shape=[jax.ShapeDtypeStruct((n, n), jnp.int32)] * 2,
        name="moe_plan",
    )(bucket.reshape(n, n))
    return pos.reshape(TOKENS), tile_bucket[0, :MOE_NTILES]


def _invert_kernel(pos_ref, inv_ref):
    def init(blk, carry):
        for r in range(DUMP_ROWS):
            inv_ref[blk * DUMP_ROWS + r] = TOKENS + r
        return carry

    def place(t, carry):
        inv_ref[pos_ref[t]] = t
        return carry

    lax.fori_loop(0, INV_LEN // DUMP_ROWS, init, 0)
    lax.fori_loop(0, TOKENS, place, 0, unroll=16)


def _invert(pos):
    smem = pl.BlockSpec(memory_space=pltpu.SMEM)
    return pl.pallas_call(
        _invert_kernel,
        in_specs=[smem],
        out_specs=smem,
        out_shape=jax.ShapeDtypeStruct((INV_LEN,), jnp.int32),
        name="moe_invert",
    )(pos)


def _experts_kernel(inv_ref, e_lo_ref, e_hi_ref, nused_ref, x_ref, g_ref, fg_ref, whi_ref, wlo_ref, b_ref,
                    wg_lo, wu_lo, wd_lo, wg_hi, wu_hi, wd_hi, xo_ref,
                    xin, yout, gsem, ssem, fsem, *, final_norm):
    i = pl.program_id(0)
    n_used = nused_ref[0]

    def token_tile(ref, token):
        return ref.at[pl.ds(pl.multiple_of(token * X_SUB, X_SUB), X_SUB), :]

    def start_gather(tile, slot):
        for r in range(MOE_TILE):
            src = jnp.minimum(inv_ref[tile * MOE_TILE + r], TOKENS - 1)
            pltpu.make_async_copy(token_tile(x_ref, src), token_tile(xin.at[slot], r),
                                  gsem.at[slot]).start(priority=r % 2)

    def start_scatter(tile, slot):
        for r in range(MOE_TILE):
            pltpu.make_async_copy(token_tile(yout.at[slot], r),
                                  token_tile(xo_ref, inv_ref[tile * MOE_TILE + r]),
                                  ssem.at[slot]).start(priority=r % 2)

    tile_rows = MOE_TILE * X_SUB

    def wait_gather(slot):
        pltpu.make_async_copy(x_ref.at[pl.ds(0, tile_rows), :], xin.at[slot], gsem.at[slot]).wait()

    def wait_scatter(slot):
        pltpu.make_async_copy(yout.at[slot], xo_ref.at[pl.ds(0, tile_rows), :], ssem.at[slot]).wait()

    @pl.when(i == 0)
    def _():
        yout[MOE_BUFS - 1] = jnp.zeros((tile_rows, LANES), F32)
        for half in range(DUMP_ROWS // MOE_TILE):
            fill = pltpu.make_async_copy(
                yout.at[MOE_BUFS - 1],
                xo_ref.at[pl.ds((TOKENS + half * MOE_TILE) * X_SUB, tile_rows), :], fsem)
            fill.start()
            fill.wait()
        start_gather(0, 0)
        start_gather(1, 1)

    def step(s, s_prev, s_next):
        wait_gather(s)

        @pl.when(i >= 2)
        def _():
            wait_scatter(s)

        start_gather(i + 2, s_prev)
        start_scatter(jnp.where(i >= 1, i - 1, SPARE_TILE), s_prev)

        x_rows = _load_rows(xin.at[s], MOE_TILE)
        h32 = _rms(x_rows, g_ref[...])
        rec, _ = _route(h32, whi_ref[...], wlo_ref[...], b_ref[...])
        h = h32.astype(BF16)

        def expert(wg, wu, wd, lane):
            hg = _dot(h, wg[...].astype(BF16))
            hu = _dot(h, wu[...].astype(BF16))
            act = jax.nn.silu(hg) * hu * rec[:, lane:lane + 1]
            return _dot(act.astype(BF16), wd[...].astype(BF16))

        out = x_rows + expert(wg_lo, wu_lo, wd_lo, REC_W_LO) + expert(wg_hi, wu_hi, wd_hi, REC_W_HI)
        _store_rows(yout.at[s], MOE_TILE, _rms(out, fg_ref[...]) if final_norm else out)

        @pl.when(i == n_used)
        def _():
            wait_gather(s_next)
            wait_gather(s_prev)
            wait_scatter(s_next)
            wait_scatter(s_prev)

    @pl.when(i <= n_used)
    def _():
        step(lax.rem(i, MOE_BUFS), lax.rem(i + 2, MOE_BUFS), lax.rem(i + 1, MOE_BUFS))


def _experts(layer, inv, e_lo, e_hi, n_used, x, g, final_g, router, w_gate, w_up, w_down):
    final_norm = final_g is not None
    if not final_norm:
        final_g = jnp.ones((1, D_MODEL), F32)
    up = lambda pick: pl.BlockSpec((None, None, D_MODEL, EXPERT_FF),
                                   lambda i, inv, lo, hi, nu: (layer, pick(lo, hi)[i], 0, 0))
    down = lambda pick: pl.BlockSpec((None, None, EXPERT_FF, D_MODEL),
                                     lambda i, inv, lo, hi, nu: (layer, pick(lo, hi)[i], 0, 0))
    first, second = (lambda lo, hi: lo), (lambda lo, hi: hi)
    grid_spec = pltpu.PrefetchScalarGridSpec(
        num_scalar_prefetch=4,
        grid=(MOE_NTILES + 1,),
        in_specs=[pl.BlockSpec(memory_space=pl.ANY), _const_spec((1, D_MODEL)), _const_spec((1, D_MODEL)),
                  _const_spec((ROUTER_ROWS, D_MODEL)), _const_spec((ROUTER_ROWS, D_MODEL)),
                  _const_spec((ROUTER_ROWS, 1)),
                  up(first), up(first), down(first), up(second), up(second), down(second)],
        out_specs=pl.BlockSpec(memory_space=pl.ANY),
        scratch_shapes=[pltpu.VMEM((MOE_BUFS, MOE_TILE * X_SUB, LANES), F32),
                        pltpu.VMEM((MOE_BUFS, MOE_TILE * X_SUB, LANES), F32),
                        pltpu.SemaphoreType.DMA((MOE_BUFS,)),
                        pltpu.SemaphoreType.DMA((MOE_BUFS,)),
                        pltpu.SemaphoreType.DMA(())])
    return pl.pallas_call(
        functools.partial(_experts_kernel, final_norm=final_norm),
        grid_spec=grid_spec,
        out_shape=jax.ShapeDtypeStruct((X_ROWS * X_SUB, LANES), F32),
        compiler_params=_params("arbitrary"),
        name="moe_experts_final" if final_norm else "moe_experts",
    )(inv, e_lo, e_hi, n_used, x, g, final_g, *router, w_gate, w_up, w_down, w_gate, w_up, w_down)


def _bucket_experts():
    pairs = [(a, b) for a in range(EXPERTS_PER_GROUP) for b in range(a + 1, EXPERTS_PER_GROUP)]
    lo = [g * EXPERTS_PER_GROUP + a for g in range(MOE_GROUPS) for a, _ in pairs]
    hi = [g * EXPERTS_PER_GROUP + b for g in range(MOE_GROUPS) for _, b in pairs]
    return jnp.array(lo, jnp.int32), jnp.array(hi, jnp.int32)


def _routed_moe(layer, x, bucket, g, router, w_gate, w_up, w_down, final_g=None):
    pos, tile_bucket = _plan(bucket)
    n_used = jnp.sum((tile_bucket < N_BUCKETS).astype(jnp.int32)).reshape(1)
    tile_bucket = jnp.minimum(jnp.concatenate([tile_bucket, tile_bucket[-1:]]), N_BUCKETS - 1)
    lo, hi = _bucket_experts()
    return _experts(layer, _invert(pos), lo[tile_bucket], hi[tile_bucket], n_used, x, g, final_g,
                    router, w_gate, w_up, w_down)


def kernel(x, ln1_g, w_in, lam_re, lam_im, log_dt, b_re, b_im, c_re, c_im, d_skip, w_glu,
           gn_attn, gn_ssm, w_out, ln2_g, w_router_grp, b_router_grp, w_router_exp,
           b_router_exp, w_gate, w_up, w_down, final_g):
    assert x.shape == (BATCH, SEQ, D_MODEL)
    ng = MOE_GROUPS
    x = x.reshape(TOKENS * X_SUB, LANES)

    for l in range(DEPTH):
        q, k, v, u = _inproj(x, ln1_g[l][None, :], w_in[l].astype(BF16))
        attn = _attention(q, k, v).reshape(TOKENS, ATTN_WIDTH)
        disc = _s5_discretize(lam_re[l], lam_im[l], log_dt[l], b_re[l], b_im[l], c_re[l], c_im[l])
        ssm = _s5(u, disc, d_skip[l].reshape(1, SSM_WIDTH), w_glu[l].astype(BF16))
        ssm = ssm.reshape(TOKENS, SSM_WIDTH)

        w_r = jnp.zeros((ROUTER_ROWS, D_MODEL), F32)
        w_r = w_r.at[0:ng].set(w_router_grp[l].T)
        w_r = w_r.at[EXPERT_ROW0:EXPERT_ROW0 + N_EXPERTS].set(
            jnp.transpose(w_router_exp[l], (0, 2, 1)).reshape(N_EXPERTS, D_MODEL))
        b_r = jnp.zeros((ROUTER_ROWS, 1), F32)
        b_r = b_r.at[0:ng, 0].set(b_router_grp[l])
        b_r = b_r.at[EXPERT_ROW0:EXPERT_ROW0 + N_EXPERTS, 0].set(b_router_exp[l].reshape(N_EXPERTS))
        w_r_hi, w_r_lo = _split_bf16(w_r)
        g2 = ln2_g[l][None, :]
        router = (w_r_hi, w_r_lo, b_r)
        x, bucket = _outproj(attn, ssm, x, gn_attn[l][None, :], gn_ssm[l][None, :],
                             w_out[l].astype(BF16), g2, *router)
        x = _routed_moe(l, x, bucket, g2, router, w_gate, w_up, w_down,
                        final_g[None, :] if l == DEPTH - 1 else None)
    return x[:TOKENS * X_SUB].reshape(BATCH, SEQ, D_MODEL)
```

```python
import functools
import math

import jax
import jax.numpy as jnp
from jax import lax
from jax.experimental import pallas as pl
from jax.experimental.pallas import tpu as pltpu

D_MODEL = 1024
BATCH = 8
SEQ = 2048
DEPTH = 4
ATTN_WIDTH = 512
HEAD_DIM = 64
ATTN_HEADS = 8
DILATIONS = (1, 4, 16)
SPAN = 128
SSM_WIDTH = 512
SSM_GROUP_CH = 16
SSM_GROUPS = 32
SSM_STATE = 64
MOE_GROUPS = 4
EXPERTS_PER_GROUP = 4
N_EXPERTS = 16
EXPERT_FF = 256
RMS_EPS = 1e-6

LANES = 128
BF16_ROWS = 16
VMEM_LIMIT_BYTES = 56 * 1024 * 1024

TOK_TILE = 512
PERM_TILE = 256
S5_CHUNK = 128
S5_GROUPS_PER_CHUNK = 8
S5_NCHUNK = SSM_GROUPS // S5_GROUPS_PER_CHUNK
S5_STATE_LANES = S5_GROUPS_PER_CHUNK * SSM_STATE
S5_SUB = BF16_ROWS

F32 = jnp.float32
BF16 = jnp.bfloat16


def _params(*sem):
    return pltpu.CompilerParams(dimension_semantics=sem, vmem_limit_bytes=VMEM_LIMIT_BYTES)


def _rms(x, g):
    return x * lax.rsqrt(jnp.mean(x * x, axis=-1, keepdims=True) + RMS_EPS) * g


def _split_bf16(a):
    hi = a.astype(BF16)
    lo = (a - hi.astype(F32)).astype(BF16)
    return hi, lo


def _dot(a, b):
    return jnp.dot(a, b, preferred_element_type=F32)


def _dot_hilo(p, a):
    hi, lo = _split_bf16(a)
    return _dot(p, hi) + _dot(p, lo)


def _tok_spec(width):
    return pl.BlockSpec((TOK_TILE, width), lambda i: (i, 0))


def _const_spec(shape):
    return pl.BlockSpec(shape, lambda *_: (0,) * len(shape))


def _layer_spec(shape, layer):
    return pl.BlockSpec((None,) + tuple(shape), lambda *_: (layer,) + (0,) * len(shape))


def _inproj_kernel(x_ref, g_ref, w_ref, q_ref, k_ref, v_ref, u_ref):
    h = _rms(x_ref[...], g_ref[...]).astype(BF16)
    p = _dot(h, w_ref[...])
    aw = ATTN_WIDTH
    q_ref[...] = (p[:, :aw] * (1.0 / math.sqrt(HEAD_DIM))).astype(BF16)
    k_ref[...] = p[:, aw:2 * aw].astype(BF16)
    v_ref[...] = p[:, 2 * aw:3 * aw].astype(BF16)
    u_ref[...] = p[:, 3 * aw:].astype(BF16)


def _inproj(layer, x, g, w):
    aw = ATTN_WIDTH
    out = jax.ShapeDtypeStruct((TOKENS, aw), BF16)
    outs = pl.pallas_call(
        _inproj_kernel,
        grid=(TOKENS // TOK_TILE,),
        in_specs=[_tok_spec(D_MODEL), _layer_spec((1, D_MODEL), layer),
                  _layer_spec((D_MODEL, 3 * aw + SSM_WIDTH), layer)],
        out_specs=[_tok_spec(aw)] * 4,
        out_shape=[out] * 4,
        compiler_params=_params("parallel"),
        name="inproj",
    )(x, g, w)
    return [o.reshape(BATCH, SEQ, aw) for o in outs]


def _attn_block(q, kk, vv, bias, low_half, lane):
    heads = range(ATTN_HEADS)
    nt = (((1,), (1,)), ((), ()))
    scores = []
    for h in heads:
        sl = slice((h // 2) * LANES, (h // 2 + 1) * LANES)
        keep = low_half if h % 2 == 0 else jnp.logical_not(low_half)
        qm = jnp.where(keep, q[:, sl], jnp.zeros_like(q[:, sl]))
        scores.append(lax.dot_general(qm, kk[:, sl], nt, preferred_element_type=F32) + bias)
    probs, dens, lses = [], [], []
    for h in heads:
        m = jnp.max(scores[h], axis=-1, keepdims=True)
        p = jnp.exp(scores[h] - m)
        den = jnp.sum(p, axis=-1, keepdims=True)
        probs.append(p.astype(BF16))
        dens.append(den)
        lses.append(m + jnp.log(den))
    pv = [_dot(probs[h], vv[:, (h // 2) * LANES:(h // 2 + 1) * LANES]) / dens[h] for h in heads]
    lse_tile = jnp.zeros((SPAN, LANES), F32)
    for h in heads:
        lse_tile = jnp.where(lane == h, lses[h], lse_tile)
    outs = [jnp.where(low_half, pv[2 * j], pv[2 * j + 1]) for j in range(ATTN_HEADS // 2)]
    return jnp.concatenate(outs, axis=-1), lse_tile


def _attn_kernel(q_ref, k_ref, v_ref, p4_ref, p4t_ref, p16_ref, p16t_ref, e_ref, o_ref,
                 qkv4, qkv16, lse1, o4p, lse4p, o16p, lse16p):
    lane = lax.broadcasted_iota(jnp.int32, (SPAN, LANES), 1)
    low_half = lane < HEAD_DIM
    row0 = lax.broadcasted_iota(jnp.int32, (SPAN, SPAN), 0)
    col0 = lax.broadcasted_iota(jnp.int32, (SPAN, SPAN), 1)
    bias_first = jnp.where(col0 <= row0, 0.0, -jnp.inf).astype(F32)
    row = lax.broadcasted_iota(jnp.int32, (SPAN, 2 * SPAN), 0)
    col = lax.broadcasted_iota(jnp.int32, (SPAN, 2 * SPAN), 1)
    bias_band = jnp.where((col >= row) & (col <= row + SPAN), 0.0, -jnp.inf).astype(F32)
    block = functools.partial(_attn_block, low_half=low_half, lane=lane)
    srcs = (q_ref, k_ref, v_ref)

    def permute(i, carry):
        rows = pl.ds(pl.multiple_of(i * PERM_TILE, PERM_TILE), PERM_TILE)
        for a, src in enumerate(srcs):
            x = src[rows, :]
            y4 = _dot(p4_ref[...], x).astype(BF16)
            y16 = _dot(p16_ref[...], x).astype(BF16)
            n4, n16 = PERM_TILE // 4, PERM_TILE // 16
            for r in range(4):
                qkv4[a, r, pl.ds(pl.multiple_of(i * n4, n4), n4), :] = y4[r * n4:(r + 1) * n4]
            for r in range(16):
                qkv16[a, r, pl.ds(pl.multiple_of(i * n16, n16), n16), :] = y16[r * n16:(r + 1) * n16]
        return carry

    lax.fori_loop(0, SEQ // PERM_TILE, permute, 0)

    def branch(get, put, nblocks):
        head = pl.ds(0, SPAN)
        o, lse = block(get(0, head), get(1, head), get(2, head), bias_first)
        put(head, o, lse)
        if nblocks > 1:
            def body(n, carry):
                qs = pl.ds(pl.multiple_of(n * SPAN, SPAN), SPAN)
                ks = pl.ds(pl.multiple_of((n - 1) * SPAN, SPAN), 2 * SPAN)
                o, lse = block(get(0, qs), get(1, ks), get(2, ks), bias_band)
                put(qs, o, lse)
                return carry
            lax.fori_loop(1, nblocks, body, 0)

    def put1(rows, o, lse):
        o_ref[rows, :] = o
        lse1[rows, :] = lse

    branch(lambda a, rows: srcs[a][rows, :], put1, SEQ // SPAN)

    def class4(r, carry):
        def put(rows, o, lse):
            o4p[r, rows, :] = o.astype(BF16)
            lse4p[r, rows, :] = lse
        branch(lambda a, rows: qkv4[a, r, rows, :], put, SEQ // 4 // SPAN)
        return carry

    lax.fori_loop(0, 4, class4, 0)

    def class16(r, carry):
        def put(rows, o, lse):
            o16p[r, rows, :] = o.astype(BF16)
            lse16p[r, rows, :] = lse
        branch(lambda a, rows: qkv16[a, r, rows, :], put, SEQ // 16 // SPAN)
        return carry

    lax.fori_loop(0, 16, class16, 0)

    def expand(w):
        hi, lo = _split_bf16(w)
        return _dot(hi, e_ref[...]) + _dot(lo, e_ref[...])

    def combine(i, carry):
        rows = pl.ds(pl.multiple_of(i * PERM_TILE, PERM_TILE), PERM_TILE)
        n4, n16 = PERM_TILE // 4, PERM_TILE // 16
        r4 = pl.ds(pl.multiple_of(i * n4, n4), n4)
        r16 = pl.ds(pl.multiple_of(i * n16, n16), n16)
        o4 = _dot(p4t_ref[...], jnp.concatenate([o4p[r, r4, :] for r in range(4)], axis=0))
        l4 = _dot_hilo(p4t_ref[...], jnp.concatenate([lse4p[r, r4, :] for r in range(4)], axis=0))
        o16 = _dot(p16t_ref[...], jnp.concatenate([o16p[r, r16, :] for r in range(16)], axis=0))
        l16 = _dot_hilo(p16t_ref[...], jnp.concatenate([lse16p[r, r16, :] for r in range(16)], axis=0))
        l1 = lse1[rows, :]
        m = jnp.maximum(jnp.maximum(l1, l4), l16)
        e1, e4, e16 = jnp.exp(l1 - m), jnp.exp(l4 - m), jnp.exp(l16 - m)
        den = e1 + e4 + e16
        w4, w16 = expand(e4 / den), expand(e16 / den)
        o1 = o_ref[rows, :]
        o_ref[rows, :] = o1 + w4 * (o4 - o1) + w16 * (o16 - o1)
        return carry

    lax.fori_loop(0, SEQ // PERM_TILE, combine, 0)


def _perm_matrix(dil):
    n = PERM_TILE // dil
    out_row = jnp.arange(PERM_TILE)
    src = dil * (out_row % n) + out_row // n
    return (src[:, None] == jnp.arange(PERM_TILE)[None, :]).astype(BF16)


def _head_expand_matrix():
    r = jnp.arange(LANES)[:, None]
    c = jnp.arange(ATTN_WIDTH)[None, :] // HEAD_DIM
    return (r == c).astype(BF16)


def _attention_constants():
    p4, p16 = _perm_matrix(4), _perm_matrix(16)
    return (p4, p4.T, p16, p16.T, _head_expand_matrix())


def _attention(q, k, v, consts):
    aw = ATTN_WIDTH
    seq_spec = pl.BlockSpec((None, SEQ, aw), lambda b: (b, 0, 0))
    return pl.pallas_call(
        _attn_kernel,
        grid=(BATCH,),
        in_specs=[seq_spec] * 3 + [_const_spec(c.shape) for c in consts],
        out_specs=seq_spec,
        out_shape=jax.ShapeDtypeStruct((BATCH, SEQ, aw), F32),
        scratch_shapes=[pltpu.VMEM((3, 4, SEQ // 4, aw), BF16),
                        pltpu.VMEM((3, 16, SEQ // 16, aw), BF16),
                        pltpu.VMEM((SEQ, LANES), F32),
                        pltpu.VMEM((4, SEQ // 4, aw), BF16),
                        pltpu.VMEM((4, SEQ // 4, LANES), F32),
                        pltpu.VMEM((16, SEQ // 16, aw), BF16),
                        pltpu.VMEM((16, SEQ // 16, LANES), F32)],
        compiler_params=_params("parallel"),
        name="attention",
    )(q, k, v, *consts)


def _s5_kernel(u_ref, pf_ref, pb_ref, bre_ref, bim_ref, lre_ref, lim_ref, cre_ref, cim_ref, d_ref,
               wglu_ref, o_ref, st_re, st_im, u_buf, xr_buf, xi_buf, y_buf):
    rows = S5_CHUNK * BATCH
    sub_rows = S5_SUB * BATCH

    @pl.when(pl.program_id(0) == 0)
    def _():
        st_re[...] = jnp.zeros_like(st_re)
        st_im[...] = jnp.zeros_like(st_im)

    for tb in range(S5_CHUNK // S5_SUB):
        t = slice(tb * S5_SUB, (tb + 1) * S5_SUB)
        piece = jnp.concatenate([u_ref[b, t, :] for b in range(BATCH)], axis=0)
        u_buf[tb * sub_rows:(tb + 1) * sub_rows, :] = _dot(pf_ref[...], piece)

    for c in range(S5_NCHUNK):
        ub = u_buf[:, c * LANES:(c + 1) * LANES].astype(BF16)
        xr_buf[c] = _dot(ub, bre_ref[c])
        xi_buf[c] = _dot(ub, bim_ref[c])

    for c in range(S5_NCHUNK):
        ch = slice(c * LANES, (c + 1) * LANES)
        stl = slice(c * S5_STATE_LANES, (c + 1) * S5_STATE_LANES)
        lr = jnp.broadcast_to(lre_ref[:, stl], (BATCH, S5_STATE_LANES))
        li = jnp.broadcast_to(lim_ref[:, stl], (BATCH, S5_STATE_LANES))
        xr, xi = st_re[:, stl], st_im[:, stl]
        for s in range(S5_CHUNK):
            sl = slice(s * BATCH, (s + 1) * BATCH)
            xr, xi = (lr * xr - li * xi + xr_buf[c, sl, :], lr * xi + li * xr + xi_buf[c, sl, :])
            xr_buf[c, sl, :] = xr
            xi_buf[c, sl, :] = xi
        st_re[:, stl] = xr
        st_im[:, stl] = xi
        yc = (_dot(xr_buf[c].astype(BF16), cre_ref[c]) - _dot(xi_buf[c].astype(BF16), cim_ref[c]))
        y_buf[:, ch] = yc + d_ref[:, ch] * u_buf[:, ch]

    y = jax.nn.gelu(y_buf[...])
    z = _dot(y.astype(BF16), wglu_ref[...])
    y_buf[...] = y * jax.nn.sigmoid(z)
    for tb in range(S5_CHUNK // S5_SUB):
        back = _dot(pb_ref[...], y_buf[tb * sub_rows:(tb + 1) * sub_rows, :].astype(BF16)).astype(BF16)
        for b in range(BATCH):
            o_ref[b, tb * S5_SUB:(tb + 1) * S5_SUB, :] = back[b * S5_SUB:(b + 1) * S5_SUB]


def _s5_discretize(lam_re, lam_im, log_dt, b_re, b_im, c_re, c_im):
    dt = jnp.exp(log_dt)[:, None]
    mag = jnp.exp(lam_re * dt)
    lb_re, lb_im = mag * jnp.cos(lam_im * dt), mag * jnp.sin(lam_im * dt)
    den = lam_re * lam_re + lam_im * lam_im
    nr, ni = lb_re - 1.0, lb_im
    f_re = (nr * lam_re + ni * lam_im) / den
    f_im = (ni * lam_re - nr * lam_im) / den
    bb_re = f_re[..., None] * b_re - f_im[..., None] * b_im
    bb_im = f_re[..., None] * b_im + f_im[..., None] * b_re
    eye = jnp.eye(S5_GROUPS_PER_CHUNK, dtype=F32)
    gpc, nch = S5_GROUPS_PER_CHUNK, S5_NCHUNK

    def b_blockdiag(b):
        b = b.reshape(nch, gpc, SSM_STATE, SSM_GROUP_CH)
        m = jnp.einsum('cgph,gk->cghkp', b, eye)
        return m.reshape(nch, gpc * SSM_GROUP_CH, gpc * SSM_STATE).astype(BF16)

    def c_blockdiag(c):
        c = c.reshape(nch, gpc, SSM_GROUP_CH, SSM_STATE)
        m = jnp.einsum('cghp,gk->cgpkh', c, eye)
        return m.reshape(nch, gpc * SSM_STATE, gpc * SSM_GROUP_CH).astype(BF16)

    return (b_blockdiag(bb_re), b_blockdiag(bb_im),
            lb_re.reshape(1, SSM_GROUPS * SSM_STATE), lb_im.reshape(1, SSM_GROUPS * SSM_STATE),
            c_blockdiag(c_re), c_blockdiag(c_im))


def _s5_reorder_matrices():
    n = S5_SUB * BATCH
    out_row = jnp.arange(n)
    src = (out_row % BATCH) * S5_SUB + out_row // BATCH
    fwd = (src[:, None] == jnp.arange(n)[None, :]).astype(BF16)
    return fwd, fwd.T


def _s5(layer, u, reorder, disc, d_skip, w_glu):
    rows = S5_CHUNK * BATCH
    blk = pl.BlockSpec((BATCH, S5_CHUNK, SSM_WIDTH), lambda i: (0, i, 0))
    per_layer = tuple(disc) + (d_skip, w_glu)
    args = tuple(reorder) + per_layer
    return pl.pallas_call(
        _s5_kernel,
        grid=(SEQ // S5_CHUNK,),
        in_specs=([blk] + [_const_spec(a.shape) for a in reorder]
                  + [_layer_spec(a.shape[1:], layer) for a in per_layer]),
        out_specs=blk,
        out_shape=jax.ShapeDtypeStruct((BATCH, SEQ, SSM_WIDTH), BF16),
        scratch_shapes=[pltpu.VMEM((BATCH, SSM_GROUPS * SSM_STATE), F32),
                        pltpu.VMEM((BATCH, SSM_GROUPS * SSM_STATE), F32),
                        pltpu.VMEM((rows, SSM_WIDTH), F32),
                        pltpu.VMEM((S5_NCHUNK, rows, S5_STATE_LANES), F32),
                        pltpu.VMEM((S5_NCHUNK, rows, S5_STATE_LANES), F32),
                        pltpu.VMEM((rows, SSM_WIDTH), F32)],
        compiler_params=_params("arbitrary"),
        name="s5",
    )(u, *args)


def _outproj_kernel(attn_ref, ssm_ref, x_ref, ga_ref, gs_ref, w_ref, g2_ref, whi_ref, wlo_ref, b_ref,
                    xa_ref, bucket_ref):
    a_n = _rms(attn_ref[...], ga_ref[...]).astype(BF16)
    s_n = _rms(ssm_ref[...].astype(F32), gs_ref[...]).astype(BF16)
    y = _dot(a_n, w_ref[0:ATTN_WIDTH, :]) + _dot(s_n, w_ref[ATTN_WIDTH:, :])
    x = x_ref[...] + y
    rec, bucket = _route(_rms(x, g2_ref[...]), whi_ref[...], wlo_ref[...], b_ref[...])
    xa_ref[:, 0:D_MODEL] = x
    xa_ref[:, D_MODEL:] = rec
    bucket_ref[...] = bucket


def _outproj(layer, attn, ssm, x, ga, gs, w, g2, whi, wlo, bias):
    aw = ATTN_WIDTH
    return pl.pallas_call(
        _outproj_kernel,
        grid=(TOKENS // TOK_TILE,),
        in_specs=[_tok_spec(aw), _tok_spec(SSM_WIDTH), _tok_spec(D_MODEL),
                  _layer_spec((1, aw), layer), _layer_spec((1, SSM_WIDTH), layer),
                  _layer_spec((aw + SSM_WIDTH, D_MODEL), layer), _layer_spec((1, D_MODEL), layer),
                  _layer_spec((ROUTER_ROWS, D_MODEL), layer), _layer_spec((ROUTER_ROWS, D_MODEL), layer),
                  _layer_spec((ROUTER_ROWS, 1), layer)],
        out_specs=[_tok_spec(MOE_ROW), pl.BlockSpec((1, TOK_TILE), lambda i: (0, i))],
        out_shape=[jax.ShapeDtypeStruct((TOKENS, MOE_ROW), F32),
                   jax.ShapeDtypeStruct((1, TOKENS), F32)],
        compiler_params=_params("parallel"),
        name="outproj_router",
    )(attn, ssm, x, ga, gs, w, g2, whi, wlo, bias)


ROUTER_ROWS = 32
EXPERT_ROW0 = 8
PAIRS_PER_GROUP = EXPERTS_PER_GROUP * (EXPERTS_PER_GROUP - 1) // 2
N_BUCKETS = MOE_GROUPS * PAIRS_PER_GROUP
REC_BUCKET, REC_W_LO, REC_W_HI = 0, 1, 2
TOKENS = BATCH * SEQ
PLAN_SIDE = 128
MOE_TILE = 256
MOE_NTILES = TOKENS // MOE_TILE + N_BUCKETS
MOE_SLOTS = MOE_NTILES * MOE_TILE
MOE_ROW = D_MODEL + LANES
DUMP_ROWS = 2 * MOE_TILE
X_ROWS = TOKENS + DUMP_ROWS
MOE_BUFS = 3
SPARE_TILE = MOE_NTILES + 3
INV_LEN = (MOE_NTILES + 4) * MOE_TILE
assert PLAN_SIDE * PLAN_SIDE == TOKENS and MOE_NTILES <= PLAN_SIDE
assert INV_LEN % DUMP_ROWS == 0 and SPARE_TILE * MOE_TILE < INV_LEN
assert SPARE_TILE % 2 == 1


def _route(h, whi, wlo, bias):
    h_hi, h_lo = _split_bf16(h)
    nt = (((1,), (1,)), ((), ()))
    logits = (lax.dot_general(whi, h_hi, nt, preferred_element_type=F32)
              + lax.dot_general(wlo, h_hi, nt, preferred_element_type=F32)
              + lax.dot_general(whi, h_lo, nt, preferred_element_type=F32)
              + bias)
    ng, ne = MOE_GROUPS, EXPERTS_PER_GROUP
    gl = [logits[g:g + 1, :] for g in range(ng)]
    best, grp = gl[0], jnp.zeros_like(gl[0], dtype=jnp.int32)
    for g in range(1, ng):
        better = gl[g] > best
        grp = jnp.where(better, g, grp)
        best = jnp.where(better, gl[g], best)
    g1 = 1.0 / sum(jnp.exp(x - best) for x in gl)
    sel = []
    for e in range(ne):
        acc = jnp.zeros_like(best)
        for g in range(ng):
            r = EXPERT_ROW0 + g * ne + e
            acc = jnp.where(grp == g, logits[r:r + 1, :], acc)
        sel.append(acc)

    def first_argmax(vals):
        bv, bi = vals[0], jnp.zeros_like(grp)
        for e in range(1, ne):
            better = vals[e] > bv
            bi = jnp.where(better, e, bi)
            bv = jnp.where(better, vals[e], bv)
        return bv, bi

    v1, i1 = first_argmax(sel)
    v2, i2 = first_argmax([jnp.where(i1 == e, -jnp.inf, sel[e]) for e in range(ne)])
    e2 = jnp.exp(v2 - v1)
    w1 = g1 / (1.0 + e2)
    w2 = g1 * e2 / (1.0 + e2)
    first_is_low = i1 < i2
    lo = jnp.where(first_is_low, i1, i2)
    hi = jnp.where(first_is_low, i2, i1)
    pair = jnp.where(lo == 0, 0, jnp.where(lo == 1, 3, 5)) + hi - lo - 1
    bucket = (grp * PAIRS_PER_GROUP + pair).astype(F32)
    w_lo = jnp.where(first_is_low, w1, w2)
    w_hi = jnp.where(first_is_low, w2, w1)
    tokens = logits.shape[1]
    rowid = lax.broadcasted_iota(jnp.int32, (LANES, tokens), 0)
    table = jnp.where(rowid == REC_BUCKET, bucket,
                      jnp.where(rowid == REC_W_LO, w_lo, jnp.where(rowid == REC_W_HI, w_hi, 0.0)))
    return table.T, bucket


def _plan_kernel(bucket_ref, pos_ref, tile_bucket_ref):
    n = PLAN_SIDE
    bucket = bucket_ref[...]
    r = lax.broadcasted_iota(jnp.int32, (n, n), 0)
    c = lax.broadcasted_iota(jnp.int32, (n, n), 1)
    before_in_row = (r < c).astype(BF16)
    rows_before = (c < r).astype(BF16)
    ones = jnp.ones((n, n), BF16)
    tile_start = (c * MOE_TILE).astype(F32)
    pos = jnp.zeros((n, n), F32)
    base = jnp.zeros((n, n), F32)
    ended = jnp.zeros((n, n), F32)
    for k in range(N_BUCKETS):
        member = bucket == float(k)
        mb = member.astype(BF16)
        in_row = _dot(mb, before_in_row)
        row_count = _dot(mb, ones).astype(BF16)
        rank = _dot(rows_before, row_count) + in_row
        total = _dot(ones, row_count)
        pos = jnp.where(member, base + rank, pos)
        base = base + jnp.ceil(total * (1.0 / MOE_TILE)) * MOE_TILE
        ended = ended + (tile_start >= base).astype(F32)
    pos_ref[...] = pos.astype(jnp.int32)
    tile_bucket_ref[...] = ended.astype(jnp.int32)


def _plan(bucket):
    n = PLAN_SIDE
    pos, tile_bucket = pl.pallas_call(
        _plan_kernel,
        out_shape=[jax.ShapeDtypeStruct((n, n), jnp.int32)] * 2,
        name="moe_plan",
    )(bucket.reshape(n, n))
    return pos.reshape(TOKENS), tile_bucket[0, :MOE_NTILES]


def _invert_kernel(pos_ref, inv_ref):
    def init(blk, carry):
        for r in range(DUMP_ROWS):
            inv_ref[blk * DUMP_ROWS + r] = TOKENS + r
        return carry

    def place(t, carry):
        inv_ref[pos_ref[t]] = t
        return carry

    lax.fori_loop(0, INV_LEN // DUMP_ROWS, init, 0)
    lax.fori_loop(0, TOKENS, place, 0, unroll=16)


def _invert(pos):
    smem = pl.BlockSpec(memory_space=pltpu.SMEM)
    return pl.pallas_call(
        _invert_kernel,
        in_specs=[smem],
        out_specs=smem,
        out_shape=jax.ShapeDtypeStruct((INV_LEN,), jnp.int32),
        name="moe_invert",
    )(pos)


def _row_copy(src, src_row, dst, dst_row, sem):
    return pltpu.make_async_copy(src.at[pl.ds(src_row, 1), :], dst.at[pl.ds(dst_row, 1), :], sem)


def _experts_kernel(inv_ref, e_lo_ref, e_hi_ref, nused_ref, xa_ref, g_ref, fg_ref,
                    wg_lo, wu_lo, wd_lo, wg_hi, wu_hi, wd_hi, xo_ref,
                    xin, yout, gsem, ssem, fsem, *, final_norm):
    i = pl.program_id(0)
    n_used = nused_ref[0]
    s = lax.rem(i, MOE_BUFS)
    s_prev = lax.rem(i + 2, MOE_BUFS)
    s_next = lax.rem(i + 1, MOE_BUFS)

    def start_gather(tile, slot):
        for r in range(MOE_TILE):
            src = jnp.minimum(inv_ref[tile * MOE_TILE + r], TOKENS - 1)
            _row_copy(xa_ref, src, xin.at[slot], r, gsem.at[slot]).start(priority=r % 2)

    def start_scatter(tile, slot):
        for r in range(MOE_TILE):
            _row_copy(yout.at[slot], r, xo_ref, inv_ref[tile * MOE_TILE + r],
                      ssem.at[slot]).start(priority=r % 2)

    def wait_gather(slot):
        pltpu.make_async_copy(xa_ref.at[pl.ds(0, MOE_TILE), :], xin.at[slot], gsem.at[slot]).wait()

    def wait_scatter(slot):
        pltpu.make_async_copy(yout.at[slot], xo_ref.at[pl.ds(0, MOE_TILE), :], ssem.at[slot]).wait()

    @pl.when(i == 0)
    def _():
        yout[MOE_BUFS - 1] = jnp.zeros((MOE_TILE, D_MODEL), F32)
        for half in range(DUMP_ROWS // MOE_TILE):
            fill = pltpu.make_async_copy(
                yout.at[MOE_BUFS - 1], xo_ref.at[pl.ds(TOKENS + half * MOE_TILE, MOE_TILE), :], fsem)
            fill.start()
            fill.wait()
        start_gather(0, 0)
        start_gather(1, 1)

    @pl.when(i <= n_used)
    def _():
        wait_gather(s)

        @pl.when(i >= 2)
        def _():
            wait_scatter(s)

        start_gather(i + 2, s_prev)
        start_scatter(jnp.where(i >= 1, i - 1, SPARE_TILE), s_prev)

        xt = xin[s]
        x_rows = xt[:, 0:D_MODEL]
        rec = xt[:, D_MODEL:]
        h = _rms(x_rows, g_ref[...]).astype(BF16)

        def expert(wg, wu, wd, lane):
            hg = _dot(h, wg[...].astype(BF16))
            hu = _dot(h, wu[...].astype(BF16))
            act = jax.nn.silu(hg) * hu * rec[:, lane:lane + 1]
            return _dot(act.astype(BF16), wd[...].astype(BF16))

        out = x_rows + expert(wg_lo, wu_lo, wd_lo, REC_W_LO) + expert(wg_hi, wu_hi, wd_hi, REC_W_HI)
        yout[s] = _rms(out, fg_ref[...]) if final_norm else out

        @pl.when(i == n_used)
        def _():
            wait_gather(s_next)
            wait_gather(s_prev)
            wait_scatter(s_next)
            wait_scatter(s_prev)


def _experts(layer, inv, e_lo, e_hi, n_used, xa, g, final_g, w_gate, w_up, w_down):
    final_norm = final_g is not None
    if not final_norm:
        final_g = jnp.ones((1, D_MODEL), F32)
    up = lambda pick: pl.BlockSpec((None, None, D_MODEL, EXPERT_FF),
                                   lambda i, inv, lo, hi, nu: (layer, pick(lo, hi)[i], 0, 0))
    down = lambda pick: pl.BlockSpec((None, None, EXPERT_FF, D_MODEL),
                                     lambda i, inv, lo, hi, nu: (layer, pick(lo, hi)[i], 0, 0))
    first, second = (lambda lo, hi: lo), (lambda lo, hi: hi)
    grid_spec = pltpu.PrefetchScalarGridSpec(
        num_scalar_prefetch=4,
        grid=(MOE_NTILES + 1,),
        in_specs=[pl.BlockSpec(memory_space=pl.ANY), _layer_spec((1, D_MODEL), layer),
                  _const_spec((1, D_MODEL)),
                  up(first), up(first), down(first), up(second), up(second), down(second)],
        out_specs=pl.BlockSpec(memory_space=pl.ANY),
        scratch_shapes=[pltpu.VMEM((MOE_BUFS, MOE_TILE, MOE_ROW), F32),
                        pltpu.VMEM((MOE_BUFS, MOE_TILE, D_MODEL), F32),
                        pltpu.SemaphoreType.DMA((MOE_BUFS,)),
                        pltpu.SemaphoreType.DMA((MOE_BUFS,)),
                        pltpu.SemaphoreType.DMA(())])
    return pl.pallas_call(
        functools.partial(_experts_kernel, final_norm=final_norm),
        grid_spec=grid_spec,
        out_shape=jax.ShapeDtypeStruct((X_ROWS, D_MODEL), F32),
        compiler_params=_params("arbitrary"),
        name="moe_experts_final" if final_norm else "moe_experts",
    )(inv, e_lo, e_hi, n_used, xa, g, final_g, w_gate, w_up, w_down, w_gate, w_up, w_down)


def _bucket_experts():
    pairs = [(a, b) for a in range(EXPERTS_PER_GROUP) for b in range(a + 1, EXPERTS_PER_GROUP)]
    lo = [g * EXPERTS_PER_GROUP + a for g in range(MOE_GROUPS) for a, _ in pairs]
    hi = [g * EXPERTS_PER_GROUP + b for g in range(MOE_GROUPS) for _, b in pairs]
    return jnp.array(lo, jnp.int32), jnp.array(hi, jnp.int32)


def _routed_moe(layer, xa, bucket, g, w_gate, w_up, w_down, final_g=None):
    pos, tile_bucket = _plan(bucket)
    n_used = jnp.sum((tile_bucket < N_BUCKETS).astype(jnp.int32)).reshape(1)
    tile_bucket = jnp.minimum(jnp.concatenate([tile_bucket, tile_bucket[-1:]]), N_BUCKETS - 1)
    lo, hi = _bucket_experts()
    return _experts(layer, _invert(pos), lo[tile_bucket], hi[tile_bucket], n_used, xa, g, final_g,
                    w_gate, w_up, w_down)


def _router_tables(w_router_grp, b_router_grp, w_router_exp, b_router_exp):
    ng = MOE_GROUPS
    experts = slice(EXPERT_ROW0, EXPERT_ROW0 + N_EXPERTS)
    w = jnp.zeros((DEPTH, ROUTER_ROWS, D_MODEL), F32)
    w = w.at[:, 0:ng].set(jnp.transpose(w_router_grp, (0, 2, 1)))
    w = w.at[:, experts].set(jnp.transpose(w_router_exp, (0, 1, 3, 2)).reshape(DEPTH, N_EXPERTS, D_MODEL))
    b = jnp.zeros((DEPTH, ROUTER_ROWS, 1), F32)
    b = b.at[:, 0:ng, 0].set(b_router_grp)
    b = b.at[:, experts, 0].set(b_router_exp.reshape(DEPTH, N_EXPERTS))
    return _split_bf16(w) + (b,)


def kernel(x, ln1_g, w_in, lam_re, lam_im, log_dt, b_re, b_im, c_re, c_im, d_skip, w_glu,
           gn_attn, gn_ssm, w_out, ln2_g, w_router_grp, b_router_grp, w_router_exp,
           b_router_exp, w_gate, w_up, w_down, final_g):
    assert x.shape == (BATCH, SEQ, D_MODEL)
    x = x.reshape(TOKENS, D_MODEL)

    row = lambda a: a.reshape(DEPTH, 1, -1)
    ln1, ln2, ga, gs, skip = row(ln1_g), row(ln2_g), row(gn_attn), row(gn_ssm), row(d_skip)
    w_in_b, w_out_b, w_glu_b = w_in.astype(BF16), w_out.astype(BF16), w_glu.astype(BF16)
    disc = jax.vmap(_s5_discretize)(lam_re, lam_im, log_dt, b_re, b_im, c_re, c_im)
    w_r_hi, w_r_lo, b_r = _router_tables(w_router_grp, b_router_grp, w_router_exp, b_router_exp)
    attn_consts = _attention_constants()
    reorder = _s5_reorder_matrices()

    for l in range(DEPTH):
        q, k, v, u = _inproj(l, x, ln1, w_in_b)
        attn = _attention(q, k, v, attn_consts).reshape(TOKENS, ATTN_WIDTH)
        ssm = _s5(l, u, reorder, disc, skip, w_glu_b).reshape(TOKENS, SSM_WIDTH)
        xa, bucket = _outproj(l, attn, ssm, x, ga, gs, w_out_b, ln2, w_r_hi, w_r_lo, b_r)
        x = _routed_moe(l, xa, bucket, ln2, w_gate, w_up, w_down,
                        final_g[None, :] if l == DEPTH - 1 else None)
    return x[:TOKENS].reshape(BATCH, SEQ, D_MODEL)
```

```python
import functools
import math

import jax
import jax.numpy as jnp
from jax import lax
from jax.experimental import pallas as pl
from jax.experimental.pallas import tpu as pltpu

D_MODEL = 1024
BATCH = 8
SEQ = 2048
DEPTH = 4
ATTN_WIDTH = 512
HEAD_DIM = 64
ATTN_HEADS = 8
DILATIONS = (1, 4, 16)
SPAN = 128
SSM_WIDTH = 512
SSM_GROUP_CH = 16
SSM_GROUPS = 32
SSM_STATE = 64
MOE_GROUPS = 4
EXPERTS_PER_GROUP = 4
N_EXPERTS = 16
EXPERT_FF = 256
RMS_EPS = 1e-6

LANES = 128
BF16_ROWS = 16
VMEM_LIMIT_BYTES = 56 * 1024 * 1024

TOK_TILE = 512
PERM_TILE = 256
S5_CHUNK = 128
S5_GROUPS_PER_CHUNK = 8
S5_NCHUNK = SSM_GROUPS // S5_GROUPS_PER_CHUNK
S5_STATE_LANES = S5_GROUPS_PER_CHUNK * SSM_STATE
S5_SUB = BF16_ROWS

F32 = jnp.float32
BF16 = jnp.bfloat16


def _params(*sem):
    return pltpu.CompilerParams(dimension_semantics=sem, vmem_limit_bytes=VMEM_LIMIT_BYTES)


def _rms(x, g):
    return x * lax.rsqrt(jnp.mean(x * x, axis=-1, keepdims=True) + RMS_EPS) * g


def _split_bf16(a):
    hi = a.astype(BF16)
    lo = (a - hi.astype(F32)).astype(BF16)
    return hi, lo


def _dot(a, b):
    return jnp.dot(a, b, preferred_element_type=F32)


def _dot_hilo(p, a):
    hi, lo = _split_bf16(a)
    return _dot(p, hi) + _dot(p, lo)


def _tok_spec(width):
    return pl.BlockSpec((TOK_TILE, width), lambda i: (i, 0))


def _const_spec(shape):
    return pl.BlockSpec(shape, lambda *_: (0,) * len(shape))


def _layer_spec(shape, layer):
    return pl.BlockSpec((None,) + tuple(shape), lambda *_: (layer,) + (0,) * len(shape))


def _inproj_kernel(x_ref, g_ref, w_ref, q_ref, k_ref, v_ref, u_ref):
    h = _rms(x_ref[...], g_ref[...]).astype(BF16)
    p = _dot(h, w_ref[...])
    aw = ATTN_WIDTH
    q_ref[...] = (p[:, :aw] * (1.0 / math.sqrt(HEAD_DIM))).astype(BF16)
    k_ref[...] = p[:, aw:2 * aw].astype(BF16)
    v_ref[...] = p[:, 2 * aw:3 * aw].astype(BF16)
    u_ref[...] = p[:, 3 * aw:].astype(BF16)


def _inproj(layer, x, g, w):
    aw = ATTN_WIDTH
    out = jax.ShapeDtypeStruct((TOKENS, aw), BF16)
    outs = pl.pallas_call(
        _inproj_kernel,
        grid=(TOKENS // TOK_TILE,),
        in_specs=[_tok_spec(D_MODEL), _layer_spec((1, D_MODEL), layer),
                  _layer_spec((D_MODEL, 3 * aw + SSM_WIDTH), layer)],
        out_specs=[_tok_spec(aw)] * 4,
        out_shape=[out] * 4,
        compiler_params=_params("parallel"),
        name="inproj",
    )(x, g, w)
    return [o.reshape(BATCH, SEQ, aw) for o in outs]


def _attn_block(q, kk, vv, bias, low_half, lane):
    heads = range(ATTN_HEADS)
    nt = (((1,), (1,)), ((), ()))
    scores = []
    for h in heads:
        sl = slice((h // 2) * LANES, (h // 2 + 1) * LANES)
        keep = low_half if h % 2 == 0 else jnp.logical_not(low_half)
        qm = jnp.where(keep, q[:, sl], jnp.zeros_like(q[:, sl]))
        scores.append(lax.dot_general(qm, kk[:, sl], nt, preferred_element_type=F32) + bias)
    probs, dens, lses = [], [], []
    for h in heads:
        m = jnp.max(scores[h], axis=-1, keepdims=True)
        p = jnp.exp(scores[h] - m)
        den = jnp.sum(p, axis=-1, keepdims=True)
        probs.append(p.astype(BF16))
        dens.append(den)
        lses.append(m + jnp.log(den))
    pv = [_dot(probs[h], vv[:, (h // 2) * LANES:(h // 2 + 1) * LANES]) / dens[h] for h in heads]
    lse_tile = jnp.zeros((SPAN, LANES), F32)
    for h in heads:
        lse_tile = jnp.where(lane == h, lses[h], lse_tile)
    outs = [jnp.where(low_half, pv[2 * j], pv[2 * j + 1]) for j in range(ATTN_HEADS // 2)]
    return jnp.concatenate(outs, axis=-1), lse_tile


def _attn_kernel(q_ref, k_ref, v_ref, p4_ref, p4t_ref, p16_ref, p16t_ref, e_ref, o_ref,
                 qkv4, qkv16, lse1, o4p, lse4p, o16p, lse16p):
    lane = lax.broadcasted_iota(jnp.int32, (SPAN, LANES), 1)
    low_half = lane < HEAD_DIM
    row0 = lax.broadcasted_iota(jnp.int32, (SPAN, SPAN), 0)
    col0 = lax.broadcasted_iota(jnp.int32, (SPAN, SPAN), 1)
    bias_first = jnp.where(col0 <= row0, 0.0, -jnp.inf).astype(F32)
    row = lax.broadcasted_iota(jnp.int32, (SPAN, 2 * SPAN), 0)
    col = lax.broadcasted_iota(jnp.int32, (SPAN, 2 * SPAN), 1)
    bias_band = jnp.where((col >= row) & (col <= row + SPAN), 0.0, -jnp.inf).astype(F32)
    block = functools.partial(_attn_block, low_half=low_half, lane=lane)
    srcs = (q_ref, k_ref, v_ref)

    def permute(i, carry):
        rows = pl.ds(pl.multiple_of(i * PERM_TILE, PERM_TILE), PERM_TILE)
        for a, src in enumerate(srcs):
            x = src[rows, :]
            y4 = _dot(p4_ref[...], x).astype(BF16)
            y16 = _dot(p16_ref[...], x).astype(BF16)
            n4, n16 = PERM_TILE // 4, PERM_TILE // 16
            for r in range(4):
                qkv4[a, r, pl.ds(pl.multiple_of(i * n4, n4), n4), :] = y4[r * n4:(r + 1) * n4]
            for r in range(16):
                qkv16[a, r, pl.ds(pl.multiple_of(i * n16, n16), n16), :] = y16[r * n16:(r + 1) * n16]
        return carry

    lax.fori_loop(0, SEQ // PERM_TILE, permute, 0)

    def branch(get, put, nblocks):
        head = pl.ds(0, SPAN)
        o, lse = block(get(0, head), get(1, head), get(2, head), bias_first)
        put(head, o, lse)
        if nblocks > 1:
            def body(n, carry):
                qs = pl.ds(pl.multiple_of(n * SPAN, SPAN), SPAN)
                ks = pl.ds(pl.multiple_of((n - 1) * SPAN, SPAN), 2 * SPAN)
                o, lse = block(get(0, qs), get(1, ks), get(2, ks), bias_band)
                put(qs, o, lse)
                return carry
            lax.fori_loop(1, nblocks, body, 0)

    def put1(rows, o, lse):
        o_ref[rows, :] = o
        lse1[rows, :] = lse

    branch(lambda a, rows: srcs[a][rows, :], put1, SEQ // SPAN)

    def class4(r, carry):
        def put(rows, o, lse):
            o4p[r, rows, :] = o.astype(BF16)
            lse4p[r, rows, :] = lse
        branch(lambda a, rows: qkv4[a, r, rows, :], put, SEQ // 4 // SPAN)
        return carry

    lax.fori_loop(0, 4, class4, 0)

    def class16(r, carry):
        def put(rows, o, lse):
            o16p[r, rows, :] = o.astype(BF16)
            lse16p[r, rows, :] = lse
        branch(lambda a, rows: qkv16[a, r, rows, :], put, SEQ // 16 // SPAN)
        return carry

    lax.fori_loop(0, 16, class16, 0)

    def expand(w):
        hi, lo = _split_bf16(w)
        return _dot(hi, e_ref[...]) + _dot(lo, e_ref[...])

    def combine(i, carry):
        rows = pl.ds(pl.multiple_of(i * PERM_TILE, PERM_TILE), PERM_TILE)
        n4, n16 = PERM_TILE // 4, PERM_TILE // 16
        r4 = pl.ds(pl.multiple_of(i * n4, n4), n4)
        r16 = pl.ds(pl.multiple_of(i * n16, n16), n16)
        o4 = _dot(p4t_ref[...], jnp.concatenate([o4p[r, r4, :] for r in range(4)], axis=0))
        l4 = _dot_hilo(p4t_ref[...], jnp.concatenate([lse4p[r, r4, :] for r in range(4)], axis=0))
        o16 = _dot(p16t_ref[...], jnp.concatenate([o16p[r, r16, :] for r in range(16)], axis=0))
        l16 = _dot_hilo(p16t_ref[...], jnp.concatenate([lse16p[r, r16, :] for r in range(16)], axis=0))
        l1 = lse1[rows, :]
        m = jnp.maximum(jnp.maximum(l1, l4), l16)
        e1, e4, e16 = jnp.exp(l1 - m), jnp.exp(l4 - m), jnp.exp(l16 - m)
        den = e1 + e4 + e16
        w4, w16 = expand(e4 / den), expand(e16 / den)
        o1 = o_ref[rows, :]
        o_ref[rows, :] = o1 + w4 * (o4 - o1) + w16 * (o16 - o1)
        return carry

    lax.fori_loop(0, SEQ // PERM_TILE, combine, 0)


def _perm_matrix(dil):
    n = PERM_TILE // dil
    out_row = jnp.arange(PERM_TILE)
    src = dil * (out_row % n) + out_row // n
    return (src[:, None] == jnp.arange(PERM_TILE)[None, :]).astype(BF16)


def _head_expand_matrix():
    r = jnp.arange(LANES)[:, None]
    c = jnp.arange(ATTN_WIDTH)[None, :] // HEAD_DIM
    return (r == c).astype(BF16)


def _attention_constants():
    p4, p16 = _perm_matrix(4), _perm_matrix(16)
    return (p4, p4.T, p16, p16.T, _head_expand_matrix())


def _attention(q, k, v, consts):
    aw = ATTN_WIDTH
    seq_spec = pl.BlockSpec((None, SEQ, aw), lambda b: (b, 0, 0))
    return pl.pallas_call(
        _attn_kernel,
        grid=(BATCH,),
        in_specs=[seq_spec] * 3 + [_const_spec(c.shape) for c in consts],
        out_specs=seq_spec,
        out_shape=jax.ShapeDtypeStruct((BATCH, SEQ, aw), F32),
        scratch_shapes=[pltpu.VMEM((3, 4, SEQ // 4, aw), BF16),
                        pltpu.VMEM((3, 16, SEQ // 16, aw), BF16),
                        pltpu.VMEM((SEQ, LANES), F32),
                        pltpu.VMEM((4, SEQ // 4, aw), BF16),
                        pltpu.VMEM((4, SEQ // 4, LANES), F32),
                        pltpu.VMEM((16, SEQ // 16, aw), BF16),
                        pltpu.VMEM((16, SEQ // 16, LANES), F32)],
        compiler_params=_params("parallel"),
        name="attention",
    )(q, k, v, *consts)


def _s5_kernel(u_ref, pf_ref, pb_ref, bre_ref, bim_ref, lre_ref, lim_ref, cre_ref, cim_ref, d_ref,
               wglu_ref, o_ref, st_re, st_im, u_buf, xr_buf, xi_buf, y_buf):
    rows = S5_CHUNK * BATCH
    sub_rows = S5_SUB * BATCH

    @pl.when(pl.program_id(0) == 0)
    def _():
        st_re[...] = jnp.zeros_like(st_re)
        st_im[...] = jnp.zeros_like(st_im)

    for tb in range(S5_CHUNK // S5_SUB):
        t = slice(tb * S5_SUB, (tb + 1) * S5_SUB)
        piece = jnp.concatenate([u_ref[b, t, :] for b in range(BATCH)], axis=0)
        u_buf[tb * sub_rows:(tb + 1) * sub_rows, :] = _dot(pf_ref[...], piece)

    for c in range(S5_NCHUNK):
        ub = u_buf[:, c * LANES:(c + 1) * LANES].astype(BF16)
        xr_buf[c] = _dot(ub, bre_ref[c])
        xi_buf[c] = _dot(ub, bim_ref[c])

    for c in range(S5_NCHUNK):
        ch = slice(c * LANES, (c + 1) * LANES)
        stl = slice(c * S5_STATE_LANES, (c + 1) * S5_STATE_LANES)
        lr = jnp.broadcast_to(lre_ref[:, stl], (BATCH, S5_STATE_LANES))
        li = jnp.broadcast_to(lim_ref[:, stl], (BATCH, S5_STATE_LANES))
        xr, xi = st_re[:, stl], st_im[:, stl]
        for s in range(S5_CHUNK):
            sl = slice(s * BATCH, (s + 1) * BATCH)
            xr, xi = (lr * xr - li * xi + xr_buf[c, sl, :], lr * xi + li * xr + xi_buf[c, sl, :])
            xr_buf[c, sl, :] = xr
            xi_buf[c, sl, :] = xi
        st_re[:, stl] = xr
        st_im[:, stl] = xi
        yc = (_dot(xr_buf[c].astype(BF16), cre_ref[c]) - _dot(xi_buf[c].astype(BF16), cim_ref[c]))
        y_buf[:, ch] = yc + d_ref[:, ch] * u_buf[:, ch]

    y = jax.nn.gelu(y_buf[...])
    z = _dot(y.astype(BF16), wglu_ref[...])
    y_buf[...] = y * jax.nn.sigmoid(z)
    for tb in range(S5_CHUNK // S5_SUB):
        back = _dot(pb_ref[...], y_buf[tb * sub_rows:(tb + 1) * sub_rows, :].astype(BF16)).astype(BF16)
        for b in range(BATCH):
            o_ref[b, tb * S5_SUB:(tb + 1) * S5_SUB, :] = back[b * S5_SUB:(b + 1) * S5_SUB]


def _s5_discretize(lam_re, lam_im, log_dt, b_re, b_im, c_re, c_im):
    dt = jnp.exp(log_dt)[:, None]
    mag = jnp.exp(lam_re * dt)
    lb_re, lb_im = mag * jnp.cos(lam_im * dt), mag * jnp.sin(lam_im * dt)
    den = lam_re * lam_re + lam_im * lam_im
    nr, ni = lb_re - 1.0, lb_im
    f_re = (nr * lam_re + ni * lam_im) / den
    f_im = (ni * lam_re - nr * lam_im) / den
    bb_re = f_re[..., None] * b_re - f_im[..., None] * b_im
    bb_im = f_re[..., None] * b_im + f_im[..., None] * b_re
    eye = jnp.eye(S5_GROUPS_PER_CHUNK, dtype=F32)
    gpc, nch = S5_GROUPS_PER_CHUNK, S5_NCHUNK

    def b_blockdiag(b):
        b = b.reshape(nch, gpc, SSM_STATE, SSM_GROUP_CH)
        m = jnp.einsum('cgph,gk->cghkp', b, eye)
        return m.reshape(nch, gpc * SSM_GROUP_CH, gpc * SSM_STATE).astype(BF16)

    def c_blockdiag(c):
        c = c.reshape(nch, gpc, SSM_GROUP_CH, SSM_STATE)
        m = jnp.einsum('cghp,gk->cgpkh', c, eye)
        return m.reshape(nch, gpc * SSM_STATE, gpc * SSM_GROUP_CH).astype(BF16)

    return (b_blockdiag(bb_re), b_blockdiag(bb_im),
            lb_re.reshape(1, SSM_GROUPS * SSM_STATE), lb_im.reshape(1, SSM_GROUPS * SSM_STATE),
            c_blockdiag(c_re), c_blockdiag(c_im))


def _s5_reorder_matrices():
    n = S5_SUB * BATCH
    out_row = jnp.arange(n)
    src = (out_row % BATCH) * S5_SUB + out_row // BATCH
    fwd = (src[:, None] == jnp.arange(n)[None, :]).astype(BF16)
    return fwd, fwd.T


def _s5(layer, u, reorder, disc, d_skip, w_glu):
    rows = S5_CHUNK * BATCH
    blk = pl.BlockSpec((BATCH, S5_CHUNK, SSM_WIDTH), lambda i: (0, i, 0))
    per_layer = tuple(disc) + (d_skip, w_glu)
    args = tuple(reorder) + per_layer
    return pl.pallas_call(
        _s5_kernel,
        grid=(SEQ // S5_CHUNK,),
        in_specs=([blk] + [_const_spec(a.shape) for a in reorder]
                  + [_layer_spec(a.shape[1:], layer) for a in per_layer]),
        out_specs=blk,
        out_shape=jax.ShapeDtypeStruct((BATCH, SEQ, SSM_WIDTH), BF16),
        scratch_shapes=[pltpu.VMEM((BATCH, SSM_GROUPS * SSM_STATE), F32),
                        pltpu.VMEM((BATCH, SSM_GROUPS * SSM_STATE), F32),
                        pltpu.VMEM((rows, SSM_WIDTH), F32),
                        pltpu.VMEM((S5_NCHUNK, rows, S5_STATE_LANES), F32),
                        pltpu.VMEM((S5_NCHUNK, rows, S5_STATE_LANES), F32),
                        pltpu.VMEM((rows, SSM_WIDTH), F32)],
        compiler_params=_params("arbitrary"),
        name="s5",
    )(u, *args)


def _outproj_kernel(attn_ref, ssm_ref, x_ref, ga_ref, gs_ref, w_ref, g2_ref, whi_ref, wlo_ref, b_ref,
                    xa_ref, bucket_ref):
    a_n = _rms(attn_ref[...], ga_ref[...]).astype(BF16)
    s_n = _rms(ssm_ref[...].astype(F32), gs_ref[...]).astype(BF16)
    y = _dot(a_n, w_ref[0:ATTN_WIDTH, :]) + _dot(s_n, w_ref[ATTN_WIDTH:, :])
    x = x_ref[...] + y
    rec, bucket = _route(_rms(x, g2_ref[...]), whi_ref[...], wlo_ref[...], b_ref[...])
    xa_ref[:, 0:D_MODEL] = x
    xa_ref[:, D_MODEL:] = rec
    bucket_ref[...] = bucket


def _outproj(layer, attn, ssm, x, ga, gs, w, g2, whi, wlo, bias):
    aw = ATTN_WIDTH
    return pl.pallas_call(
        _outproj_kernel,
        grid=(TOKENS // TOK_TILE,),
        in_specs=[_tok_spec(aw), _tok_spec(SSM_WIDTH), _tok_spec(D_MODEL),
                  _layer_spec((1, aw), layer), _layer_spec((1, SSM_WIDTH), layer),
                  _layer_spec((aw + SSM_WIDTH, D_MODEL), layer), _layer_spec((1, D_MODEL), layer),
                  _layer_spec((ROUTER_ROWS, D_MODEL), layer), _layer_spec((ROUTER_ROWS, D_MODEL), layer),
                  _layer_spec((ROUTER_ROWS, 1), layer)],
        out_specs=[_tok_spec(MOE_ROW), pl.BlockSpec((1, TOK_TILE), lambda i: (0, i))],
        out_shape=[jax.ShapeDtypeStruct((TOKENS, MOE_ROW), F32),
                   jax.ShapeDtypeStruct((1, TOKENS), F32)],
        compiler_params=_params("parallel"),
        name="outproj_router",
    )(attn, ssm, x, ga, gs, w, g2, whi, wlo, bias)


ROUTER_ROWS = 32
EXPERT_ROW0 = 8
PAIRS_PER_GROUP = EXPERTS_PER_GROUP * (EXPERTS_PER_GROUP - 1) // 2
N_BUCKETS = MOE_GROUPS * PAIRS_PER_GROUP
REC_BUCKET, REC_W_LO, REC_W_HI = 0, 1, 2
TOKENS = BATCH * SEQ
PLAN_SIDE = 128
MOE_TILE = 256
MOE_NTILES = TOKENS // MOE_TILE + N_BUCKETS
MOE_SLOTS = MOE_NTILES * MOE_TILE
MOE_ROW = D_MODEL + LANES
DUMP_ROWS = 2 * MOE_TILE
X_ROWS = TOKENS + DUMP_ROWS
MOE_BUFS = 3
SPARE_TILE = MOE_NTILES + 3
INV_LEN = (MOE_NTILES + 4) * MOE_TILE
assert PLAN_SIDE * PLAN_SIDE == TOKENS and MOE_NTILES <= PLAN_SIDE
assert INV_LEN % DUMP_ROWS == 0 and SPARE_TILE * MOE_TILE < INV_LEN
assert SPARE_TILE % 2 == 1


def _route(h, whi, wlo, bias):
    h_hi, h_lo = _split_bf16(h)
    nt = (((1,), (1,)), ((), ()))
    logits = (lax.dot_general(whi, h_hi, nt, preferred_element_type=F32)
              + lax.dot_general(wlo, h_hi, nt, preferred_element_type=F32)
              + lax.dot_general(whi, h_lo, nt, preferred_element_type=F32)
              + bias)
    ng, ne = MOE_GROUPS, EXPERTS_PER_GROUP
    gl = [logits[g:g + 1, :] for g in range(ng)]
    best, grp = gl[0], jnp.zeros_like(gl[0], dtype=jnp.int32)
    for g in range(1, ng):
        better = gl[g] > best
        grp = jnp.where(better, g, grp)
        best = jnp.where(better, gl[g], best)
    g1 = 1.0 / sum(jnp.exp(x - best) for x in gl)
    sel = []
    for e in range(ne):
        acc = jnp.zeros_like(best)
        for g in range(ng):
            r = EXPERT_ROW0 + g * ne + e
            acc = jnp.where(grp == g, logits[r:r + 1, :], acc)
        sel.append(acc)

    def first_argmax(vals):
        bv, bi = vals[0], jnp.zeros_like(grp)
        for e in range(1, ne):
            better = vals[e] > bv
            bi = jnp.where(better, e, bi)
            bv = jnp.where(better, vals[e], bv)
        return bv, bi

    v1, i1 = first_argmax(sel)
    v2, i2 = first_argmax([jnp.where(i1 == e, -jnp.inf, sel[e]) for e in range(ne)])
    e2 = jnp.exp(v2 - v1)
    w1 = g1 / (1.0 + e2)
    w2 = g1 * e2 / (1.0 + e2)
    first_is_low = i1 < i2
    lo = jnp.where(first_is_low, i1, i2)
    hi = jnp.where(first_is_low, i2, i1)
    pair = jnp.where(lo == 0, 0, jnp.where(lo == 1, 3, 5)) + hi - lo - 1
    bucket = (grp * PAIRS_PER_GROUP + pair).astype(F32)
    w_lo = jnp.where(first_is_low, w1, w2)
    w_hi = jnp.where(first_is_low, w2, w1)
    tokens = logits.shape[1]
    rowid = lax.broadcasted_iota(jnp.int32, (LANES, tokens), 0)
    table = jnp.where(rowid == REC_BUCKET, bucket,
                      jnp.where(rowid == REC_W_LO, w_lo, jnp.where(rowid == REC_W_HI, w_hi, 0.0)))
    return table.T, bucket


def _plan_kernel(bucket_ref, pos_ref, tile_bucket_ref):
    n = PLAN_SIDE
    bucket = bucket_ref[...]
    r = lax.broadcasted_iota(jnp.int32, (n, n), 0)
    c = lax.broadcasted_iota(jnp.int32, (n, n), 1)
    before_in_row = (r < c).astype(BF16)
    rows_before = (c < r).astype(BF16)
    ones = jnp.ones((n, n), BF16)
    tile_start = (c * MOE_TILE).astype(F32)
    pos = jnp.zeros((n, n), F32)
    base = jnp.zeros((n, n), F32)
    ended = jnp.zeros((n, n), F32)
    for k in range(N_BUCKETS):
        member = bucket == float(k)
        mb = member.astype(BF16)
        in_row = _dot(mb, before_in_row)
        row_count = _dot(mb, ones).astype(BF16)
        rank = _dot(rows_before, row_count) + in_row
        total = _dot(ones, row_count)
        pos = jnp.where(member, base + rank, pos)
        base = base + jnp.ceil(total * (1.0 / MOE_TILE)) * MOE_TILE
        ended = ended + (tile_start >= base).astype(F32)
    pos_ref[...] = pos.astype(jnp.int32)
    tile_bucket_ref[...] = ended.astype(jnp.int32)


def _plan(bucket):
    n = PLAN_SIDE
    pos, tile_bucket = pl.pallas_call(
        _plan_kernel,
        out_shape=[jax.ShapeDtypeStruct((n, n), jnp.int32)] * 2,
        name="moe_plan",
    )(bucket.reshape(n, n))
    return pos.reshape(TOKENS), tile_bucket[0, :MOE_NTILES]


def _invert_kernel(pos_ref, inv_ref):
    def init(blk, carry):
        for r in range(DUMP_ROWS):
            inv_ref[blk * DUMP_ROWS + r] = TOKENS + r
        return carry

    def place(t, carry):
        inv_ref[pos_ref[t]] = t
        return carry

    lax.fori_loop(0, INV_LEN // DUMP_ROWS, init, 0)
    lax.fori_loop(0, TOKENS, place, 0, unroll=16)


def _invert(pos):
    smem = pl.BlockSpec(memory_space=pltpu.SMEM)
    return pl.pallas_call(
        _invert_kernel,
        in_specs=[smem],
        out_specs=smem,
        out_shape=jax.ShapeDtypeStruct((INV_LEN,), jnp.int32),
        name="moe_invert",
    )(pos)


def _row_copy(src, src_row, dst, dst_row, sem):
    return pltpu.make_async_copy(src.at[pl.ds(src_row, 1), :], dst.at[pl.ds(dst_row, 1), :], sem)


def _experts_kernel(inv_ref, e_lo_ref, e_hi_ref, nused_ref, xa_ref, g_ref, fg_ref,
                    wg_lo, wu_lo, wd_lo, wg_hi, wu_hi, wd_hi, xo_ref,
                    xin, yout, gsem, ssem, fsem, *, final_norm):
    i = pl.program_id(0)
    n_used = nused_ref[0]
    s = lax.rem(i, MOE_BUFS)
    s_prev = lax.rem(i + 2, MOE_BUFS)
    s_next = lax.rem(i + 1, MOE_BUFS)

    def start_gather(tile, slot):
        for r in range(MOE_TILE):
            src = jnp.minimum(inv_ref[tile * MOE_TILE + r], TOKENS - 1)
            _row_copy(xa_ref, src, xin.at[slot], r, gsem.at[slot]).start(priority=r % 2)

    def start_scatter(tile, slot):
        for r in range(MOE_TILE):
            _row_copy(yout.at[slot], r, xo_ref, inv_ref[tile * MOE_TILE + r],
                      ssem.at[slot]).start(priority=r % 2)

    def wait_gather(slot):
        pltpu.make_async_copy(xa_ref.at[pl.ds(0, MOE_TILE), :], xin.at[slot], gsem.at[slot]).wait()

    def wait_scatter(slot):
        pltpu.make_async_copy(yout.at[slot], xo_ref.at[pl.ds(0, MOE_TILE), :], ssem.at[slot]).wait()

    @pl.when(i == 0)
    def _():
        yout[MOE_BUFS - 1] = jnp.zeros((MOE_TILE, D_MODEL), F32)
        for half in range(DUMP_ROWS // MOE_TILE):
            fill = pltpu.make_async_copy(
                yout.at[MOE_BUFS - 1], xo_ref.at[pl.ds(TOKENS + half * MOE_TILE, MOE_TILE), :], fsem)
            fill.start()
            fill.wait()
        start_gather(0, 0)
        start_gather(1, 1)

    @pl.when(i <= n_used)
    def _():
        wait_gather(s)

        @pl.when(i >= 2)
        def _():
            wait_scatter(s)

        start_gather(i + 2, s_prev)
        start_scatter(jnp.where(i >= 1, i - 1, SPARE_TILE), s_prev)

        xt = xin[s]
        x_rows = xt[:, 0:D_MODEL]
        rec = xt[:, D_MODEL:]
        h = _rms(x_rows, g_ref[...]).astype(BF16)

        def expert(wg, wu, wd, lane):
            hg = _dot(h, wg[...].astype(BF16))
            hu = _dot(h, wu[...].astype(BF16))
            act = jax.nn.silu(hg) * hu * rec[:, lane:lane + 1]
            return _dot(act.astype(BF16), wd[...].astype(BF16))

        out = x_rows + expert(wg_lo, wu_lo, wd_lo, REC_W_LO) + expert(wg_hi, wu_hi, wd_hi, REC_W_HI)
        yout[s] = _rms(out, fg_ref[...]) if final_norm else out

        @pl.when(i == n_used)
        def _():
            wait_gather(s_next)
            wait_gather(s_prev)
            wait_scatter(s_next)
            wait_scatter(s_prev)


def _experts(layer, inv, e_lo, e_hi, n_used, xa, g, final_g, w_gate, w_up, w_down):
    final_norm = final_g is not None
    if not final_norm:
        final_g = jnp.ones((1, D_MODEL), F32)
    up = lambda pick: pl.BlockSpec((None, None, D_MODEL, EXPERT_FF),
                                   lambda i, inv, lo, hi, nu: (layer, pick(lo, hi)[i], 0, 0))
    down = lambda pick: pl.BlockSpec((None, None, EXPERT_FF, D_MODEL),
                                     lambda i, inv, lo, hi, nu: (layer, pick(lo, hi)[i], 0, 0))
    first, second = (lambda lo, hi: lo), (lambda lo, hi: hi)
    grid_spec = pltpu.PrefetchScalarGridSpec(
        num_scalar_prefetch=4,
        grid=(MOE_NTILES + 1,),
        in_specs=[pl.BlockSpec(memory_space=pl.ANY), _const_spec((1, D_MODEL)), _const_spec((1, D_MODEL)),
                  up(first), up(first), down(first), up(second), up(second), down(second)],
        out_specs=pl.BlockSpec(memory_space=pl.ANY),
        scratch_shapes=[pltpu.VMEM((MOE_BUFS, MOE_TILE, MOE_ROW), F32),
                        pltpu.VMEM((MOE_BUFS, MOE_TILE, D_MODEL), F32),
                        pltpu.SemaphoreType.DMA((MOE_BUFS,)),
                        pltpu.SemaphoreType.DMA((MOE_BUFS,)),
                        pltpu.SemaphoreType.DMA(())])
    return pl.pallas_call(
        functools.partial(_experts_kernel, final_norm=final_norm),
        grid_spec=grid_spec,
        out_shape=jax.ShapeDtypeStruct((X_ROWS, D_MODEL), F32),
        compiler_params=_params("arbitrary"),
        name="moe_experts_final" if final_norm else "moe_experts",
    )(inv, e_lo, e_hi, n_used, xa, g, final_g, w_gate, w_up, w_down, w_gate, w_up, w_down)


def _bucket_experts():
    pairs = [(a, b) for a in range(EXPERTS_PER_GROUP) for b in range(a + 1, EXPERTS_PER_GROUP)]
    lo = [g * EXPERTS_PER_GROUP + a for g in range(MOE_GROUPS) for a, _ in pairs]
    hi = [g * EXPERTS_PER_GROUP + b for g in range(MOE_GROUPS) for _, b in pairs]
    return jnp.array(lo, jnp.int32), jnp.array(hi, jnp.int32)


def _routed_moe(layer, xa, bucket, g, w_gate, w_up, w_down, final_g=None):
    pos, tile_bucket = _plan(bucket)
    n_used = jnp.sum((tile_bucket < N_BUCKETS).astype(jnp.int32)).reshape(1)
    tile_bucket = jnp.minimum(jnp.concatenate([tile_bucket, tile_bucket[-1:]]), N_BUCKETS - 1)
    lo, hi = _bucket_experts()
    return _experts(layer, _invert(pos), lo[tile_bucket], hi[tile_bucket], n_used, xa, g, final_g,
                    w_gate, w_up, w_down)


def _router_tables(w_router_grp, b_router_grp, w_router_exp, b_router_exp):
    ng = MOE_GROUPS
    experts = slice(EXPERT_ROW0, EXPERT_ROW0 + N_EXPERTS)
    w = jnp.zeros((DEPTH, ROUTER_ROWS, D_MODEL), F32)
    w = w.at[:, 0:ng].set(jnp.transpose(w_router_grp, (0, 2, 1)))
    w = w.at[:, experts].set(jnp.transpose(w_router_exp, (0, 1, 3, 2)).reshape(DEPTH, N_EXPERTS, D_MODEL))
    b = jnp.zeros((DEPTH, ROUTER_ROWS, 1), F32)
    b = b.at[:, 0:ng, 0].set(b_router_grp)
    b = b.at[:, experts, 0].set(b_router_exp.reshape(DEPTH, N_EXPERTS))
    return _split_bf16(w) + (b,)


def kernel(x, ln1_g, w_in, lam_re, lam_im, log_dt, b_re, b_im, c_re, c_im, d_skip, w_glu,
           gn_attn, gn_ssm, w_out, ln2_g, w_router_grp, b_router_grp, w_router_exp,
           b_router_exp, w_gate, w_up, w_down, final_g):
    assert x.shape == (BATCH, SEQ, D_MODEL)
    x = x.reshape(TOKENS, D_MODEL)

    row = lambda a: a.reshape(DEPTH, 1, -1)
    ln1, ln2, ga, gs, skip = row(ln1_g), row(ln2_g), row(gn_attn), row(gn_ssm), row(d_skip)
    w_in_b, w_out_b, w_glu_b = w_in.astype(BF16), w_out.astype(BF16), w_glu.astype(BF16)
    disc = jax.vmap(_s5_discretize)(lam_re, lam_im, log_dt, b_re, b_im, c_re, c_im)
    w_r_hi, w_r_lo, b_r = _router_tables(w_router_grp, b_router_grp, w_router_exp, b_router_exp)
    attn_consts = _attention_constants()
    reorder = _s5_reorder_matrices()

    for l in range(DEPTH):
        q, k, v, u = _inproj(l, x, ln1, w_in_b)
        attn = _attention(q, k, v, attn_consts).reshape(TOKENS, ATTN_WIDTH)
        ssm = _s5(l, u, reorder, disc, skip, w_glu_b).reshape(TOKENS, SSM_WIDTH)
        xa, bucket = _outproj(l, attn, ssm, x, ga, gs, w_out_b, ln2, w_r_hi, w_r_lo, b_r)
        x = _routed_moe(l, xa, bucket, ln2_g[l][None, :], w_gate, w_up, w_down,
                        final_g[None, :] if l == DEPTH - 1 else None)
    return x[:TOKENS].reshape(BATCH, SEQ, D_MODEL)
```

```python
import functools
import math

import jax
import jax.numpy as jnp
from jax import lax
from jax.experimental import pallas as pl
from jax.experimental.pallas import tpu as pltpu

D_MODEL = 1024
BATCH = 8
SEQ = 2048
DEPTH = 4
ATTN_WIDTH = 512
HEAD_DIM = 64
ATTN_HEADS = 8
DILATIONS = (1, 4, 16)
SPAN = 128
SSM_WIDTH = 512
SSM_GROUP_CH = 16
SSM_GROUPS = 32
SSM_STATE = 64
MOE_GROUPS = 4
EXPERTS_PER_GROUP = 4
N_EXPERTS = 16
EXPERT_FF = 256
RMS_EPS = 1e-6

LANES = 128
BF16_ROWS = 16
VMEM_LIMIT_BYTES = 56 * 1024 * 1024

TOK_TILE = 512
PERM_TILE = 256
S5_CHUNK = 128
S5_GROUPS_PER_CHUNK = 8
S5_NCHUNK = SSM_GROUPS // S5_GROUPS_PER_CHUNK
S5_STATE_LANES = S5_GROUPS_PER_CHUNK * SSM_STATE
S5_SUB = BF16_ROWS

F32 = jnp.float32
BF16 = jnp.bfloat16


def _params(*sem):
    return pltpu.CompilerParams(dimension_semantics=sem, vmem_limit_bytes=VMEM_LIMIT_BYTES)


def _rms(x, g):
    return x * lax.rsqrt(jnp.mean(x * x, axis=-1, keepdims=True) + RMS_EPS) * g


def _split_bf16(a):
    hi = a.astype(BF16)
    lo = (a - hi.astype(F32)).astype(BF16)
    return hi, lo


def _dot(a, b):
    return jnp.dot(a, b, preferred_element_type=F32)


def _dot_hilo(p, a):
    hi, lo = _split_bf16(a)
    return _dot(p, hi) + _dot(p, lo)


def _tok_spec(width):
    return pl.BlockSpec((TOK_TILE, width), lambda i: (i, 0))


def _const_spec(shape):
    return pl.BlockSpec(shape, lambda *_: (0,) * len(shape))


def _inproj_kernel(x_ref, g_ref, w_ref, q_ref, k_ref, v_ref, u_ref):
    h = _rms(x_ref[...], g_ref[...]).astype(BF16)
    p = _dot(h, w_ref[...])
    aw = ATTN_WIDTH
    q_ref[...] = (p[:, :aw] * (1.0 / math.sqrt(HEAD_DIM))).astype(BF16)
    k_ref[...] = p[:, aw:2 * aw].astype(BF16)
    v_ref[...] = p[:, 2 * aw:3 * aw].astype(BF16)
    u_ref[...] = p[:, 3 * aw:].astype(BF16)


def _inproj(x, g, w):
    aw = ATTN_WIDTH
    out = jax.ShapeDtypeStruct((TOKENS, aw), BF16)
    outs = pl.pallas_call(
        _inproj_kernel,
        grid=(TOKENS // TOK_TILE,),
        in_specs=[_tok_spec(D_MODEL), _const_spec((1, D_MODEL)),
                  _const_spec((D_MODEL, 3 * aw + SSM_WIDTH))],
        out_specs=[_tok_spec(aw)] * 4,
        out_shape=[out] * 4,
        compiler_params=_params("parallel"),
        name="inproj",
    )(x, g, w)
    return [o.reshape(BATCH, SEQ, aw) for o in outs]


def _attn_block(q, kk, vv, bias, low_half, lane):
    return _attn_blocks([(q, kk, vv)], bias, low_half, lane)[0]


def _attn_blocks(blocks, bias, low_half, lane):
    heads = range(ATTN_HEADS)
    nt = (((1,), (1,)), ((), ()))
    scores = []
    for q, kk, _ in blocks:
        for h in heads:
            sl = slice((h // 2) * LANES, (h // 2 + 1) * LANES)
            keep = low_half if h % 2 == 0 else jnp.logical_not(low_half)
            qm = jnp.where(keep, q[:, sl], jnp.zeros_like(q[:, sl]))
            scores.append(lax.dot_general(qm, kk[:, sl], nt, preferred_element_type=F32) + bias)
    probs, dens, lses = [], [], []
    for s in scores:
        m = jnp.max(s, axis=-1, keepdims=True)
        p = jnp.exp(s - m)
        den = jnp.sum(p, axis=-1, keepdims=True)
        probs.append(p.astype(BF16))
        dens.append(den)
        lses.append(m + jnp.log(den))
    results = []
    for b, (_, _, vv) in enumerate(blocks):
        base = b * ATTN_HEADS
        pv = [_dot(probs[base + h], vv[:, (h // 2) * LANES:(h // 2 + 1) * LANES]) / dens[base + h]
              for h in heads]
        lse_tile = jnp.zeros((SPAN, LANES), F32)
        for h in heads:
            lse_tile = jnp.where(lane == h, lses[base + h], lse_tile)
        outs = [jnp.where(low_half, pv[2 * j], pv[2 * j + 1]) for j in range(ATTN_HEADS // 2)]
        results.append((jnp.concatenate(outs, axis=-1), lse_tile))
    return results


def _attn_kernel(q_ref, k_ref, v_ref, p4_ref, p4t_ref, p16_ref, p16t_ref, e_ref, o_ref,
                 qkv4, qkv16, lse1, o4p, lse4p, o16p, lse16p):
    lane = lax.broadcasted_iota(jnp.int32, (SPAN, LANES), 1)
    low_half = lane < HEAD_DIM
    row0 = lax.broadcasted_iota(jnp.int32, (SPAN, SPAN), 0)
    col0 = lax.broadcasted_iota(jnp.int32, (SPAN, SPAN), 1)
    bias_first = jnp.where(col0 <= row0, 0.0, -jnp.inf).astype(F32)
    row = lax.broadcasted_iota(jnp.int32, (SPAN, 2 * SPAN), 0)
    col = lax.broadcasted_iota(jnp.int32, (SPAN, 2 * SPAN), 1)
    bias_band = jnp.where((col >= row) & (col <= row + SPAN), 0.0, -jnp.inf).astype(F32)
    block = functools.partial(_attn_block, low_half=low_half, lane=lane)
    srcs = (q_ref, k_ref, v_ref)

    def permute(i, carry):
        rows = pl.ds(pl.multiple_of(i * PERM_TILE, PERM_TILE), PERM_TILE)
        for a, src in enumerate(srcs):
            x = src[rows, :]
            y4 = _dot(p4_ref[...], x).astype(BF16)
            y16 = _dot(p16_ref[...], x).astype(BF16)
            n4, n16 = PERM_TILE // 4, PERM_TILE // 16
            for r in range(4):
                qkv4[a, r, pl.ds(pl.multiple_of(i * n4, n4), n4), :] = y4[r * n4:(r + 1) * n4]
            for r in range(16):
                qkv16[a, r, pl.ds(pl.multiple_of(i * n16, n16), n16), :] = y16[r * n16:(r + 1) * n16]
        return carry

    lax.fori_loop(0, SEQ // PERM_TILE, permute, 0)

    def q_rows(n):
        return pl.ds(pl.multiple_of(n * SPAN, SPAN), SPAN)

    def k_rows(n):
        return pl.ds(pl.multiple_of((n - 1) * SPAN, SPAN), 2 * SPAN)

    def band_blocks(items):
        results = _attn_blocks([(get(0, q_rows(n)), get(1, k_rows(n)), get(2, k_rows(n)))
                                for get, _, n in items], bias_band, low_half, lane)
        for (_, put, n), (o, lse) in zip(items, results):
            put(q_rows(n), o, lse)

    def first_blocks(items):
        head = pl.ds(0, SPAN)
        results = _attn_blocks([(get(0, head), get(1, head), get(2, head)) for get, _ in items],
                               bias_first, low_half, lane)
        for (_, put), (o, lse) in zip(items, results):
            put(head, o, lse)

    def get1(a, rows):
        return srcs[a][rows, :]

    def put1(rows, o, lse):
        o_ref[rows, :] = o
        lse1[rows, :] = lse

    n1 = SEQ // SPAN
    first_blocks([(get1, put1)])

    def pair1(j, carry):
        band_blocks([(get1, put1, 1 + 2 * j), (get1, put1, 2 + 2 * j)])
        return carry

    lax.fori_loop(0, (n1 - 1) // 2, pair1, 0)
    if (n1 - 1) % 2:
        band_blocks([(get1, put1, n1 - 1)])

    def access4(r):
        def get(a, rows):
            return qkv4[a, r, rows, :]

        def put(rows, o, lse):
            o4p[r, rows, :] = o.astype(BF16)
            lse4p[r, rows, :] = lse
        return get, put

    def class4_pair(j, carry):
        pair = [access4(2 * j), access4(2 * j + 1)]
        first_blocks(pair)

        def body(n, c):
            band_blocks([(get, put, n) for get, put in pair])
            return c

        lax.fori_loop(1, SEQ // 4 // SPAN, body, 0)
        return carry

    lax.fori_loop(0, 2, class4_pair, 0)

    assert SEQ // 16 == SPAN

    def class16_pair(j, carry):
        classes = (2 * j, 2 * j + 1)
        results = _attn_blocks([tuple(qkv16[a, r] for a in range(3)) for r in classes],
                               bias_first, low_half, lane)
        for r, (o, lse) in zip(classes, results):
            o16p[r] = o.astype(BF16)
            lse16p[r] = lse
        return carry

    lax.fori_loop(0, 8, class16_pair, 0)

    def expand(w):
        hi, lo = _split_bf16(w)
        return _dot(hi, e_ref[...]) + _dot(lo, e_ref[...])

    def combine(i, carry):
        rows = pl.ds(pl.multiple_of(i * PERM_TILE, PERM_TILE), PERM_TILE)
        n4, n16 = PERM_TILE // 4, PERM_TILE // 16
        r4 = pl.ds(pl.multiple_of(i * n4, n4), n4)
        r16 = pl.ds(pl.multiple_of(i * n16, n16), n16)
        o4 = _dot(p4t_ref[...], jnp.concatenate([o4p[r, r4, :] for r in range(4)], axis=0))
        l4 = _dot_hilo(p4t_ref[...], jnp.concatenate([lse4p[r, r4, :] for r in range(4)], axis=0))
        o16 = _dot(p16t_ref[...], jnp.concatenate([o16p[r, r16, :] for r in range(16)], axis=0))
        l16 = _dot_hilo(p16t_ref[...], jnp.concatenate([lse16p[r, r16, :] for r in range(16)], axis=0))
        l1 = lse1[rows, :]
        m = jnp.maximum(jnp.maximum(l1, l4), l16)
        e1, e4, e16 = jnp.exp(l1 - m), jnp.exp(l4 - m), jnp.exp(l16 - m)
        den = e1 + e4 + e16
        w4, w16 = expand(e4 / den), expand(e16 / den)
        o1 = o_ref[rows, :]
        o_ref[rows, :] = o1 + w4 * (o4 - o1) + w16 * (o16 - o1)
        return carry

    lax.fori_loop(0, SEQ // PERM_TILE, combine, 0)


def _perm_matrix(dil):
    n = PERM_TILE // dil
    out_row = jnp.arange(PERM_TILE)
    src = dil * (out_row % n) + out_row // n
    return (src[:, None] == jnp.arange(PERM_TILE)[None, :]).astype(BF16)


def _head_expand_matrix():
    r = jnp.arange(LANES)[:, None]
    c = jnp.arange(ATTN_WIDTH)[None, :] // HEAD_DIM
    return (r == c).astype(BF16)


def _attention(q, k, v):
    aw = ATTN_WIDTH
    seq_spec = pl.BlockSpec((None, SEQ, aw), lambda b: (b, 0, 0))
    p4, p16 = _perm_matrix(4), _perm_matrix(16)
    consts = (p4, p4.T, p16, p16.T, _head_expand_matrix())
    return pl.pallas_call(
        _attn_kernel,
        grid=(BATCH,),
        in_specs=[seq_spec] * 3 + [_const_spec(c.shape) for c in consts],
        out_specs=seq_spec,
        out_shape=jax.ShapeDtypeStruct((BATCH, SEQ, aw), F32),
        scratch_shapes=[pltpu.VMEM((3, 4, SEQ // 4, aw), BF16),
                        pltpu.VMEM((3, 16, SEQ // 16, aw), BF16),
                        pltpu.VMEM((SEQ, LANES), F32),
                        pltpu.VMEM((4, SEQ // 4, aw), BF16),
                        pltpu.VMEM((4, SEQ // 4, LANES), F32),
                        pltpu.VMEM((16, SEQ // 16, aw), BF16),
                        pltpu.VMEM((16, SEQ // 16, LANES), F32)],
        compiler_params=_params("parallel"),
        name="attention",
    )(q, k, v, *consts)


def _s5_kernel(u_ref, pf_ref, pb_ref, bre_ref, bim_ref, lre_ref, lim_ref, cre_ref, cim_ref, d_ref,
               wglu_ref, o_ref, st_re, st_im, u_buf, xr_buf, xi_buf, y_buf):
    rows = S5_CHUNK * BATCH
    sub_rows = S5_SUB * BATCH

    @pl.when(pl.program_id(0) == 0)
    def _():
        st_re[...] = jnp.zeros_like(st_re)
        st_im[...] = jnp.zeros_like(st_im)

    for tb in range(S5_CHUNK // S5_SUB):
        t = slice(tb * S5_SUB, (tb + 1) * S5_SUB)
        piece = jnp.concatenate([u_ref[b, t, :] for b in range(BATCH)], axis=0)
        u_buf[tb * sub_rows:(tb + 1) * sub_rows, :] = _dot(pf_ref[...], piece)

    for c in range(S5_NCHUNK):
        ub = u_buf[:, c * LANES:(c + 1) * LANES].astype(BF16)
        xr_buf[c] = _dot(ub, bre_ref[c])
        xi_buf[c] = _dot(ub, bim_ref[c])

    for c in range(S5_NCHUNK):
        ch = slice(c * LANES, (c + 1) * LANES)
        stl = slice(c * S5_STATE_LANES, (c + 1) * S5_STATE_LANES)
        lr = jnp.broadcast_to(lre_ref[:, stl], (BATCH, S5_STATE_LANES))
        li = jnp.broadcast_to(lim_ref[:, stl], (BATCH, S5_STATE_LANES))
        xr, xi = st_re[:, stl], st_im[:, stl]
        for s in range(S5_CHUNK):
            sl = slice(s * BATCH, (s + 1) * BATCH)
            xr, xi = (lr * xr - li * xi + xr_buf[c, sl, :], lr * xi + li * xr + xi_buf[c, sl, :])
            xr_buf[c, sl, :] = xr
            xi_buf[c, sl, :] = xi
        st_re[:, stl] = xr
        st_im[:, stl] = xi
        yc = (_dot(xr_buf[c].astype(BF16), cre_ref[c]) - _dot(xi_buf[c].astype(BF16), cim_ref[c]))
        y_buf[:, ch] = yc + d_ref[:, ch] * u_buf[:, ch]

    y = jax.nn.gelu(y_buf[...])
    z = _dot(y.astype(BF16), wglu_ref[...])
    y_buf[...] = y * jax.nn.sigmoid(z)
    for tb in range(S5_CHUNK // S5_SUB):
        back = _dot(pb_ref[...], y_buf[tb * sub_rows:(tb + 1) * sub_rows, :].astype(BF16)).astype(BF16)
        for b in range(BATCH):
            o_ref[b, tb * S5_SUB:(tb + 1) * S5_SUB, :] = back[b * S5_SUB:(b + 1) * S5_SUB]


def _s5_discretize(lam_re, lam_im, log_dt, b_re, b_im, c_re, c_im):
    dt = jnp.exp(log_dt)[:, None]
    mag = jnp.exp(lam_re * dt)
    lb_re, lb_im = mag * jnp.cos(lam_im * dt), mag * jnp.sin(lam_im * dt)
    den = lam_re * lam_re + lam_im * lam_im
    nr, ni = lb_re - 1.0, lb_im
    f_re = (nr * lam_re + ni * lam_im) / den
    f_im = (ni * lam_re - nr * lam_im) / den
    bb_re = f_re[..., None] * b_re - f_im[..., None] * b_im
    bb_im = f_re[..., None] * b_im + f_im[..., None] * b_re
    eye = jnp.eye(S5_GROUPS_PER_CHUNK, dtype=F32)
    gpc, nch = S5_GROUPS_PER_CHUNK, S5_NCHUNK

    def b_blockdiag(b):
        b = b.reshape(nch, gpc, SSM_STATE, SSM_GROUP_CH)
        m = jnp.einsum('cgph,gk->cghkp', b, eye)
        return m.reshape(nch, gpc * SSM_GROUP_CH, gpc * SSM_STATE).astype(BF16)

    def c_blockdiag(c):
        c = c.reshape(nch, gpc, SSM_GROUP_CH, SSM_STATE)
        m = jnp.einsum('cghp,gk->cgpkh', c, eye)
        return m.reshape(nch, gpc * SSM_STATE, gpc * SSM_GROUP_CH).astype(BF16)

    return (b_blockdiag(bb_re), b_blockdiag(bb_im),
            lb_re.reshape(1, SSM_GROUPS * SSM_STATE), lb_im.reshape(1, SSM_GROUPS * SSM_STATE),
            c_blockdiag(c_re), c_blockdiag(c_im))


def _s5_reorder_matrices():
    n = S5_SUB * BATCH
    out_row = jnp.arange(n)
    src = (out_row % BATCH) * S5_SUB + out_row // BATCH
    fwd = (src[:, None] == jnp.arange(n)[None, :]).astype(BF16)
    return fwd, fwd.T


def _s5(u, disc, d_skip, w_glu):
    rows = S5_CHUNK * BATCH
    blk = pl.BlockSpec((BATCH, S5_CHUNK, SSM_WIDTH), lambda i: (0, i, 0))
    args = _s5_reorder_matrices() + tuple(disc) + (d_skip, w_glu)
    return pl.pallas_call(
        _s5_kernel,
        grid=(SEQ // S5_CHUNK,),
        in_specs=[blk] + [_const_spec(a.shape) for a in args],
        out_specs=blk,
        out_shape=jax.ShapeDtypeStruct((BATCH, SEQ, SSM_WIDTH), BF16),
        scratch_shapes=[pltpu.VMEM((BATCH, SSM_GROUPS * SSM_STATE), F32),
                        pltpu.VMEM((BATCH, SSM_GROUPS * SSM_STATE), F32),
                        pltpu.VMEM((rows, SSM_WIDTH), F32),
                        pltpu.VMEM((S5_NCHUNK, rows, S5_STATE_LANES), F32),
                        pltpu.VMEM((S5_NCHUNK, rows, S5_STATE_LANES), F32),
                        pltpu.VMEM((rows, SSM_WIDTH), F32)],
        compiler_params=_params("arbitrary"),
        name="s5",
    )(u, *args)


def _outproj_kernel(attn_ref, ssm_ref, x_ref, ga_ref, gs_ref, w_ref, g2_ref, whi_ref, wlo_ref, b_ref,
                    xa_ref, bucket_ref):
    a_n = _rms(attn_ref[...], ga_ref[...]).astype(BF16)
    s_n = _rms(ssm_ref[...].astype(F32), gs_ref[...]).astype(BF16)
    y = _dot(a_n, w_ref[0:ATTN_WIDTH, :]) + _dot(s_n, w_ref[ATTN_WIDTH:, :])
    x = x_ref[...] + y
    rec, bucket = _route(_rms(x, g2_ref[...]), whi_ref[...], wlo_ref[...], b_ref[...])
    xa_ref[:, 0:D_MODEL] = x
    xa_ref[:, D_MODEL:] = rec
    bucket_ref[...] = bucket


def _outproj(attn, ssm, x, ga, gs, w, g2, whi, wlo, bias):
    aw = ATTN_WIDTH
    return pl.pallas_call(
        _outproj_kernel,
        grid=(TOKENS // TOK_TILE,),
        in_specs=[_tok_spec(aw), _tok_spec(SSM_WIDTH), _tok_spec(D_MODEL),
                  _const_spec((1, aw)), _const_spec((1, SSM_WIDTH)),
                  _const_spec((aw + SSM_WIDTH, D_MODEL)), _const_spec((1, D_MODEL)),
                  _const_spec((ROUTER_ROWS, D_MODEL)), _const_spec((ROUTER_ROWS, D_MODEL)),
                  _const_spec((ROUTER_ROWS, 1))],
        out_specs=[_tok_spec(MOE_ROW), pl.BlockSpec((1, TOK_TILE), lambda i: (0, i))],
        out_shape=[jax.ShapeDtypeStruct((TOKENS, MOE_ROW), F32),
                   jax.ShapeDtypeStruct((1, TOKENS), F32)],
        compiler_params=_params("parallel"),
        name="outproj_router",
    )(attn, ssm, x, ga, gs, w, g2, whi, wlo, bias)


ROUTER_ROWS = 32
EXPERT_ROW0 = 8
PAIRS_PER_GROUP = EXPERTS_PER_GROUP * (EXPERTS_PER_GROUP - 1) // 2
N_BUCKETS = MOE_GROUPS * PAIRS_PER_GROUP
REC_BUCKET, REC_W_LO, REC_W_HI = 0, 1, 2
TOKENS = BATCH * SEQ
PLAN_SIDE = 128
MOE_TILE = 256
MOE_NTILES = TOKENS // MOE_TILE + N_BUCKETS
MOE_SLOTS = MOE_NTILES * MOE_TILE
MOE_ROW = D_MODEL + LANES
DUMP_ROWS = 2 * MOE_TILE
X_ROWS = TOKENS + DUMP_ROWS
MOE_BUFS = 3
SPARE_TILE = MOE_NTILES + 3
INV_LEN = (MOE_NTILES + 4) * MOE_TILE
assert PLAN_SIDE * PLAN_SIDE == TOKENS and MOE_NTILES <= PLAN_SIDE
assert INV_LEN % DUMP_ROWS == 0 and SPARE_TILE * MOE_TILE < INV_LEN
assert SPARE_TILE % 2 == 1


def _route(h, whi, wlo, bias):
    h_hi, h_lo = _split_bf16(h)
    nt = (((1,), (1,)), ((), ()))
    logits = (lax.dot_general(whi, h_hi, nt, preferred_element_type=F32)
              + lax.dot_general(wlo, h_hi, nt, preferred_element_type=F32)
              + lax.dot_general(whi, h_lo, nt, preferred_element_type=F32)
              + bias)
    ng, ne = MOE_GROUPS, EXPERTS_PER_GROUP
    gl = [logits[g:g + 1, :] for g in range(ng)]
    best, grp = gl[0], jnp.zeros_like(gl[0], dtype=jnp.int32)
    for g in range(1, ng):
        better = gl[g] > best
        grp = jnp.where(better, g, grp)
        best = jnp.where(better, gl[g], best)
    g1 = 1.0 / sum(jnp.exp(x - best) for x in gl)
    sel = []
    for e in range(ne):
        acc = jnp.zeros_like(best)
        for g in range(ng):
            r = EXPERT_ROW0 + g * ne + e
            acc = jnp.where(grp == g, logits[r:r + 1, :], acc)
        sel.append(acc)

    def first_argmax(vals):
        bv, bi = vals[0], jnp.zeros_like(grp)
        for e in range(1, ne):
            better = vals[e] > bv
            bi = jnp.where(better, e, bi)
            bv = jnp.where(better, vals[e], bv)
        return bv, bi

    v1, i1 = first_argmax(sel)
    v2, i2 = first_argmax([jnp.where(i1 == e, -jnp.inf, sel[e]) for e in range(ne)])
    e2 = jnp.exp(v2 - v1)
    w1 = g1 / (1.0 + e2)
    w2 = g1 * e2 / (1.0 + e2)
    first_is_low = i1 < i2
    lo = jnp.where(first_is_low, i1, i2)
    hi = jnp.where(first_is_low, i2, i1)
    pair = jnp.where(lo == 0, 0, jnp.where(lo == 1, 3, 5)) + hi - lo - 1
    bucket = (grp * PAIRS_PER_GROUP + pair).astype(F32)
    w_lo = jnp.where(first_is_low, w1, w2)
    w_hi = jnp.where(first_is_low, w2, w1)
    tokens = logits.shape[1]
    rowid = lax.broadcasted_iota(jnp.int32, (LANES, tokens), 0)
    table = jnp.where(rowid == REC_BUCKET, bucket,
                      jnp.where(rowid == REC_W_LO, w_lo, jnp.where(rowid == REC_W_HI, w_hi, 0.0)))
    return table.T, bucket


def _plan_kernel(bucket_ref, pos_ref, tile_bucket_ref):
    n = PLAN_SIDE
    bucket = bucket_ref[...]
    r = lax.broadcasted_iota(jnp.int32, (n, n), 0)
    c = lax.broadcasted_iota(jnp.int32, (n, n), 1)
    before_in_row = (r < c).astype(BF16)
    rows_before = (c < r).astype(BF16)
    ones = jnp.ones((n, n), BF16)
    tile_start = (c * MOE_TILE).astype(F32)
    pos = jnp.zeros((n, n), F32)
    base = jnp.zeros((n, n), F32)
    ended = jnp.zeros((n, n), F32)
    for k in range(N_BUCKETS):
        member = bucket == float(k)
        mb = member.astype(BF16)
        in_row = _dot(mb, before_in_row)
        row_count = _dot(mb, ones).astype(BF16)
        rank = _dot(rows_before, row_count) + in_row
        total = _dot(ones, row_count)
        pos = jnp.where(member, base + rank, pos)
        base = base + jnp.ceil(total * (1.0 / MOE_TILE)) * MOE_TILE
        ended = ended + (tile_start >= base).astype(F32)
    pos_ref[...] = pos.astype(jnp.int32)
    tile_bucket_ref[...] = ended.astype(jnp.int32)


def _plan(bucket):
    n = PLAN_SIDE
    pos, tile_bucket = pl.pallas_call(
        _plan_kernel,
        out_shape=[jax.ShapeDtypeStruct((n, n), jnp.int32)] * 2,
        name="moe_plan",
    )(bucket.reshape(n, n))
    return pos.reshape(TOKENS), tile_bucket[0, :MOE_NTILES]


def _invert_kernel(pos_ref, inv_ref):
    def init(blk, carry):
        for r in range(DUMP_ROWS):
            inv_ref[blk * DUMP_ROWS + r] = TOKENS + r
        return carry

    def place(t, carry):
        inv_ref[pos_ref[t]] = t
        return carry

    lax.fori_loop(0, INV_LEN // DUMP_ROWS, init, 0)
    lax.fori_loop(0, TOKENS, place, 0, unroll=16)


def _invert(pos):
    smem = pl.BlockSpec(memory_space=pltpu.SMEM)
    return pl.pallas_call(
        _invert_kernel,
        in_specs=[smem],
        out_specs=smem,
        out_shape=jax.ShapeDtypeStruct((INV_LEN,), jnp.int32),
        name="moe_invert",
    )(pos)


def _row_copy(src, src_row, dst, dst_row, sem):
    return pltpu.make_async_copy(src.at[pl.ds(src_row, 1), :], dst.at[pl.ds(dst_row, 1), :], sem)


def _experts_kernel(inv_ref, e_lo_ref, e_hi_ref, nused_ref, xa_ref, g_ref, fg_ref,
                    wg_lo, wu_lo, wd_lo, wg_hi, wu_hi, wd_hi, xo_ref,
                    xin, yout, gsem, ssem, fsem, *, final_norm):
    i = pl.program_id(0)
    n_used = nused_ref[0]
    s = lax.rem(i, MOE_BUFS)
    s_prev = lax.rem(i + 2, MOE_BUFS)
    s_next = lax.rem(i + 1, MOE_BUFS)

    def start_gather(tile, slot):
        for r in range(MOE_TILE):
            src = jnp.minimum(inv_ref[tile * MOE_TILE + r], TOKENS - 1)
            _row_copy(xa_ref, src, xin.at[slot], r, gsem.at[slot]).start(priority=r % 2)

    def start_scatter(tile, slot):
        for r in range(MOE_TILE):
            _row_copy(yout.at[slot], r, xo_ref, inv_ref[tile * MOE_TILE + r],
                      ssem.at[slot]).start(priority=r % 2)

    def wait_gather(slot):
        pltpu.make_async_copy(xa_ref.at[pl.ds(0, MOE_TILE), :], xin.at[slot], gsem.at[slot]).wait()

    def wait_scatter(slot):
        pltpu.make_async_copy(yout.at[slot], xo_ref.at[pl.ds(0, MOE_TILE), :], ssem.at[slot]).wait()

    @pl.when(i == 0)
    def _():
        yout[MOE_BUFS - 1] = jnp.zeros((MOE_TILE, D_MODEL), F32)
        for half in range(DUMP_ROWS // MOE_TILE):
            fill = pltpu.make_async_copy(
                yout.at[MOE_BUFS - 1], xo_ref.at[pl.ds(TOKENS + half * MOE_TILE, MOE_TILE), :], fsem)
            fill.start()
            fill.wait()
        start_gather(0, 0)
        start_gather(1, 1)

    @pl.when(i <= n_used)
    def _():
        wait_gather(s)

        @pl.when(i >= 2)
        def _():
            wait_scatter(s)

        start_gather(i + 2, s_prev)
        start_scatter(jnp.where(i >= 1, i - 1, SPARE_TILE), s_prev)

        xt = xin[s]
        x_rows = xt[:, 0:D_MODEL]
        rec = xt[:, D_MODEL:]
        h = _rms(x_rows, g_ref[...]).astype(BF16)

        def expert(wg, wu, wd, lane):
            hg = _dot(h, wg[...].astype(BF16))
            hu = _dot(h, wu[...].astype(BF16))
            act = jax.nn.silu(hg) * hu * rec[:, lane:lane + 1]
            return _dot(act.astype(BF16), wd[...].astype(BF16))

        out = x_rows + expert(wg_lo, wu_lo, wd_lo, REC_W_LO) + expert(wg_hi, wu_hi, wd_hi, REC_W_HI)
        yout[s] = _rms(out, fg_ref[...]) if final_norm else out

        @pl.when(i == n_used)
        def _():
            wait_gather(s_next)
            wait_gather(s_prev)
            wait_scatter(s_next)
            wait_scatter(s_prev)


def _experts(layer, inv, e_lo, e_hi, n_used, xa, g, final_g, w_gate, w_up, w_down):
    final_norm = final_g is not None
    if not final_norm:
        final_g = jnp.ones((1, D_MODEL), F32)
    up = lambda pick: pl.BlockSpec((None, None, D_MODEL, EXPERT_FF),
                                   lambda i, inv, lo, hi, nu: (layer, pick(lo, hi)[i], 0, 0))
    down = lambda pick: pl.BlockSpec((None, None, EXPERT_FF, D_MODEL),
                                     lambda i, inv, lo, hi, nu: (layer, pick(lo, hi)[i], 0, 0))
    first, second = (lambda lo, hi: lo), (lambda lo, hi: hi)
    grid_spec = pltpu.PrefetchScalarGridSpec(
        num_scalar_prefetch=4,
        grid=(MOE_NTILES + 1,),
        in_specs=[pl.BlockSpec(memory_space=pl.ANY), _const_spec((1, D_MODEL)), _const_spec((1, D_MODEL)),
                  up(first), up(first), down(first), up(second), up(second), down(second)],
        out_specs=pl.BlockSpec(memory_space=pl.ANY),
        scratch_shapes=[pltpu.VMEM((MOE_BUFS, MOE_TILE, MOE_ROW), F32),
                        pltpu.VMEM((MOE_BUFS, MOE_TILE, D_MODEL), F32),
                        pltpu.SemaphoreType.DMA((MOE_BUFS,)),
                        pltpu.SemaphoreType.DMA((MOE_BUFS,)),
                        pltpu.SemaphoreType.DMA(())])
    return pl.pallas_call(
        functools.partial(_experts_kernel, final_norm=final_norm),
        grid_spec=grid_spec,
        out_shape=jax.ShapeDtypeStruct((X_ROWS, D_MODEL), F32),
        compiler_params=_params("arbitrary"),
        name="moe_experts_final" if final_norm else "moe_experts",
    )(inv, e_lo, e_hi, n_used, xa, g, final_g, w_gate, w_up, w_down, w_gate, w_up, w_down)


def _bucket_experts():
    pairs = [(a, b) for a in range(EXPERTS_PER_GROUP) for b in range(a + 1, EXPERTS_PER_GROUP)]
    lo = [g * EXPERTS_PER_GROUP + a for g in range(MOE_GROUPS) for a, _ in pairs]
    hi = [g * EXPERTS_PER_GROUP + b for g in range(MOE_GROUPS) for _, b in pairs]
    return jnp.array(lo, jnp.int32), jnp.array(hi, jnp.int32)


def _routed_moe(layer, xa, bucket, g, w_gate, w_up, w_down, final_g=None):
    pos, tile_bucket = _plan(bucket)
    n_used = jnp.sum((tile_bucket < N_BUCKETS).astype(jnp.int32)).reshape(1)
    tile_bucket = jnp.minimum(jnp.concatenate([tile_bucket, tile_bucket[-1:]]), N_BUCKETS - 1)
    lo, hi = _bucket_experts()
    return _experts(layer, _invert(pos), lo[tile_bucket], hi[tile_bucket], n_used, xa, g, final_g,
                    w_gate, w_up, w_down)


def kernel(x, ln1_g, w_in, lam_re, lam_im, log_dt, b_re, b_im, c_re, c_im, d_skip, w_glu,
           gn_attn, gn_ssm, w_out, ln2_g, w_router_grp, b_router_grp, w_router_exp,
           b_router_exp, w_gate, w_up, w_down, final_g):
    assert x.shape == (BATCH, SEQ, D_MODEL)
    ng = MOE_GROUPS
    x = x.reshape(TOKENS, D_MODEL)

    for l in range(DEPTH):
        q, k, v, u = _inproj(x, ln1_g[l][None, :], w_in[l].astype(BF16))
        attn = _attention(q, k, v).reshape(TOKENS, ATTN_WIDTH)
        disc = _s5_discretize(lam_re[l], lam_im[l], log_dt[l], b_re[l], b_im[l], c_re[l], c_im[l])
        ssm = _s5(u, disc, d_skip[l].reshape(1, SSM_WIDTH), w_glu[l].astype(BF16))
        ssm = ssm.reshape(TOKENS, SSM_WIDTH)

        w_r = jnp.zeros((ROUTER_ROWS, D_MODEL), F32)
        w_r = w_r.at[0:ng].set(w_router_grp[l].T)
        w_r = w_r.at[EXPERT_ROW0:EXPERT_ROW0 + N_EXPERTS].set(
            jnp.transpose(w_router_exp[l], (0, 2, 1)).reshape(N_EXPERTS, D_MODEL))
        b_r = jnp.zeros((ROUTER_ROWS, 1), F32)
        b_r = b_r.at[0:ng, 0].set(b_router_grp[l])
        b_r = b_r.at[EXPERT_ROW0:EXPERT_ROW0 + N_EXPERTS, 0].set(b_router_exp[l].reshape(N_EXPERTS))
        w_r_hi, w_r_lo = _split_bf16(w_r)
        g2 = ln2_g[l][None, :]
        xa, bucket = _outproj(attn, ssm, x, gn_attn[l][None, :], gn_ssm[l][None, :],
                              w_out[l].astype(BF16), g2, w_r_hi, w_r_lo, b_r)
        x = _routed_moe(l, xa, bucket, g2, w_gate, w_up, w_down,
                        final_g[None, :] if l == DEPTH - 1 else None)
    return x[:TOKENS].reshape(BATCH, SEQ, D_MODEL)
```

```python
import functools
import math

import jax
import jax.numpy as jnp
from jax import lax
from jax.experimental import pallas as pl
from jax.experimental.pallas import tpu as pltpu

D_MODEL = 1024
BATCH = 8
SEQ = 2048
DEPTH = 4
ATTN_WIDTH = 512
HEAD_DIM = 64
ATTN_HEADS = 8
DILATIONS = (1, 4, 16)
SPAN = 128
SSM_WIDTH = 512
SSM_GROUP_CH = 16
SSM_GROUPS = 32
SSM_STATE = 64
MOE_GROUPS = 4
EXPERTS_PER_GROUP = 4
N_EXPERTS = 16
EXPERT_FF = 256
RMS_EPS = 1e-6

LANES = 128
BF16_ROWS = 16
VMEM_LIMIT_BYTES = 56 * 1024 * 1024

TOK_TILE = 1024
PERM_TILE = 256
S5_CHUNK = 128
S5_GROUPS_PER_CHUNK = 8
S5_NCHUNK = SSM_GROUPS // S5_GROUPS_PER_CHUNK
S5_STATE_LANES = S5_GROUPS_PER_CHUNK * SSM_STATE
S5_SUB = BF16_ROWS

F32 = jnp.float32
BF16 = jnp.bfloat16


def _params(*sem):
    return pltpu.CompilerParams(dimension_semantics=sem, vmem_limit_bytes=VMEM_LIMIT_BYTES)


def _rms(x, g):
    return x * lax.rsqrt(jnp.mean(x * x, axis=-1, keepdims=True) + RMS_EPS) * g


def _split_bf16(a):
    hi = a.astype(BF16)
    lo = (a - hi.astype(F32)).astype(BF16)
    return hi, lo


def _dot(a, b):
    return jnp.dot(a, b, preferred_element_type=F32)


def _dot_hilo(p, a):
    hi, lo = _split_bf16(a)
    return _dot(p, hi) + _dot(p, lo)


def _tok_spec(width):
    return pl.BlockSpec((TOK_TILE, width), lambda i: (i, 0))


def _const_spec(shape):
    return pl.BlockSpec(shape, lambda *_: (0,) * len(shape))


def _inproj_kernel(x_ref, g_ref, w_ref, q_ref, k_ref, v_ref, u_ref):
    h = _rms(x_ref[...], g_ref[...]).astype(BF16)
    p = _dot(h, w_ref[...])
    aw = ATTN_WIDTH
    q_ref[...] = (p[:, :aw] * (1.0 / math.sqrt(HEAD_DIM))).astype(BF16)
    k_ref[...] = p[:, aw:2 * aw].astype(BF16)
    v_ref[...] = p[:, 2 * aw:3 * aw].astype(BF16)
    u_ref[...] = p[:, 3 * aw:].astype(BF16)


def _inproj(x, g, w):
    aw = ATTN_WIDTH
    out = jax.ShapeDtypeStruct((TOKENS, aw), BF16)
    outs = pl.pallas_call(
        _inproj_kernel,
        grid=(TOKENS // TOK_TILE,),
        in_specs=[_tok_spec(D_MODEL), _const_spec((1, D_MODEL)),
                  _const_spec((D_MODEL, 3 * aw + SSM_WIDTH))],
        out_specs=[_tok_spec(aw)] * 4,
        out_shape=[out] * 4,
        compiler_params=_params("parallel"),
        name="inproj",
    )(x, g, w)
    return [o.reshape(BATCH, SEQ, aw) for o in outs]


def _attn_block(q, kk, vv, bias, low_half, lane):
    return _attn_blocks([(q, kk, vv)], bias, low_half, lane)[0]


def _attn_blocks(blocks, bias, low_half, lane):
    heads = range(ATTN_HEADS)
    nt = (((1,), (1,)), ((), ()))
    scores = []
    for q, kk, _ in blocks:
        for h in heads:
            sl = slice((h // 2) * LANES, (h // 2 + 1) * LANES)
            keep = low_half if h % 2 == 0 else jnp.logical_not(low_half)
            qm = jnp.where(keep, q[:, sl], jnp.zeros_like(q[:, sl]))
            scores.append(lax.dot_general(qm, kk[:, sl], nt, preferred_element_type=F32) + bias)
    probs, dens, lses = [], [], []
    for s in scores:
        m = jnp.max(s, axis=-1, keepdims=True)
        p = jnp.exp(s - m)
        den = jnp.sum(p, axis=-1, keepdims=True)
        probs.append(p.astype(BF16))
        dens.append(den)
        lses.append(m + jnp.log(den))
    results = []
    for b, (_, _, vv) in enumerate(blocks):
        base = b * ATTN_HEADS
        pv = [_dot(probs[base + h], vv[:, (h // 2) * LANES:(h // 2 + 1) * LANES]) / dens[base + h]
              for h in heads]
        lse_tile = jnp.zeros((SPAN, LANES), F32)
        for h in heads:
            lse_tile = jnp.where(lane == h, lses[base + h], lse_tile)
        outs = [jnp.where(low_half, pv[2 * j], pv[2 * j + 1]) for j in range(ATTN_HEADS // 2)]
        results.append((jnp.concatenate(outs, axis=-1), lse_tile))
    return results


def _attn_kernel(q_ref, k_ref, v_ref, p4_ref, p4t_ref, p16_ref, p16t_ref, e_ref, o_ref,
                 qkv4, qkv16, lse1, o4p, lse4p, o16p, lse16p):
    lane = lax.broadcasted_iota(jnp.int32, (SPAN, LANES), 1)
    low_half = lane < HEAD_DIM
    row0 = lax.broadcasted_iota(jnp.int32, (SPAN, SPAN), 0)
    col0 = lax.broadcasted_iota(jnp.int32, (SPAN, SPAN), 1)
    bias_first = jnp.where(col0 <= row0, 0.0, -jnp.inf).astype(F32)
    row = lax.broadcasted_iota(jnp.int32, (SPAN, 2 * SPAN), 0)
    col = lax.broadcasted_iota(jnp.int32, (SPAN, 2 * SPAN), 1)
    bias_band = jnp.where((col >= row) & (col <= row + SPAN), 0.0, -jnp.inf).astype(F32)
    block = functools.partial(_attn_block, low_half=low_half, lane=lane)
    srcs = (q_ref, k_ref, v_ref)

    def permute(i, carry):
        rows = pl.ds(pl.multiple_of(i * PERM_TILE, PERM_TILE), PERM_TILE)
        for a, src in enumerate(srcs):
            x = src[rows, :]
            y4 = _dot(p4_ref[...], x).astype(BF16)
            y16 = _dot(p16_ref[...], x).astype(BF16)
            n4, n16 = PERM_TILE // 4, PERM_TILE // 16
            for r in range(4):
                qkv4[a, r, pl.ds(pl.multiple_of(i * n4, n4), n4), :] = y4[r * n4:(r + 1) * n4]
            for r in range(16):
                qkv16[a, r, pl.ds(pl.multiple_of(i * n16, n16), n16), :] = y16[r * n16:(r + 1) * n16]
        return carry

    lax.fori_loop(0, SEQ // PERM_TILE, permute, 0)

    def q_rows(n):
        return pl.ds(pl.multiple_of(n * SPAN, SPAN), SPAN)

    def k_rows(n):
        return pl.ds(pl.multiple_of((n - 1) * SPAN, SPAN), 2 * SPAN)

    def band_blocks(items):
        results = _attn_blocks([(get(0, q_rows(n)), get(1, k_rows(n)), get(2, k_rows(n)))
                                for get, _, n in items], bias_band, low_half, lane)
        for (_, put, n), (o, lse) in zip(items, results):
            put(q_rows(n), o, lse)

    def first_blocks(items):
        head = pl.ds(0, SPAN)
        results = _attn_blocks([(get(0, head), get(1, head), get(2, head)) for get, _ in items],
                               bias_first, low_half, lane)
        for (_, put), (o, lse) in zip(items, results):
            put(head, o, lse)

    def get1(a, rows):
        return srcs[a][rows, :]

    def put1(rows, o, lse):
        o_ref[rows, :] = o
        lse1[rows, :] = lse

    n1 = SEQ // SPAN
    first_blocks([(get1, put1)])

    def pair1(j, carry):
        band_blocks([(get1, put1, 1 + 2 * j), (get1, put1, 2 + 2 * j)])
        return carry

    lax.fori_loop(0, (n1 - 1) // 2, pair1, 0)
    if (n1 - 1) % 2:
        band_blocks([(get1, put1, n1 - 1)])

    def access4(r):
        def get(a, rows):
            return qkv4[a, r, rows, :]

        def put(rows, o, lse):
            o4p[r, rows, :] = o.astype(BF16)
            lse4p[r, rows, :] = lse
        return get, put

    def class4_pair(j, carry):
        pair = [access4(2 * j), access4(2 * j + 1)]
        first_blocks(pair)

        def body(n, c):
            band_blocks([(get, put, n) for get, put in pair])
            return c

        lax.fori_loop(1, SEQ // 4 // SPAN, body, 0)
        return carry

    lax.fori_loop(0, 2, class4_pair, 0)

    assert SEQ // 16 == SPAN

    def class16_pair(j, carry):
        classes = (2 * j, 2 * j + 1)
        results = _attn_blocks([tuple(qkv16[a, r] for a in range(3)) for r in classes],
                               bias_first, low_half, lane)
        for r, (o, lse) in zip(classes, results):
            o16p[r] = o.astype(BF16)
            lse16p[r] = lse
        return carry

    lax.fori_loop(0, 8, class16_pair, 0)

    def expand(w):
        hi, lo = _split_bf16(w)
        return _dot(hi, e_ref[...]) + _dot(lo, e_ref[...])

    def combine(i, carry):
        rows = pl.ds(pl.multiple_of(i * PERM_TILE, PERM_TILE), PERM_TILE)
        n4, n16 = PERM_TILE // 4, PERM_TILE // 16
        r4 = pl.ds(pl.multiple_of(i * n4, n4), n4)
        r16 = pl.ds(pl.multiple_of(i * n16, n16), n16)
        o4 = _dot(p4t_ref[...], jnp.concatenate([o4p[r, r4, :] for r in range(4)], axis=0))
        l4 = _dot_hilo(p4t_ref[...], jnp.concatenate([lse4p[r, r4, :] for r in range(4)], axis=0))
        o16 = _dot(p16t_ref[...], jnp.concatenate([o16p[r, r16, :] for r in range(16)], axis=0))
        l16 = _dot_hilo(p16t_ref[...], jnp.concatenate([lse16p[r, r16, :] for r in range(16)], axis=0))
        l1 = lse1[rows, :]
        m = jnp.maximum(jnp.maximum(l1, l4), l16)
        e1, e4, e16 = jnp.exp(l1 - m), jnp.exp(l4 - m), jnp.exp(l16 - m)
        den = e1 + e4 + e16
        w4, w16 = expand(e4 / den), expand(e16 / den)
        o1 = o_ref[rows, :]
        o_ref[rows, :] = o1 + w4 * (o4 - o1) + w16 * (o16 - o1)
        return carry

    lax.fori_loop(0, SEQ // PERM_TILE, combine, 0)


def _perm_matrix(dil):
    n = PERM_TILE // dil
    out_row = jnp.arange(PERM_TILE)
    src = dil * (out_row % n) + out_row // n
    return (src[:, None] == jnp.arange(PERM_TILE)[None, :]).astype(BF16)


def _head_expand_matrix():
    r = jnp.arange(LANES)[:, None]
    c = jnp.arange(ATTN_WIDTH)[None, :] // HEAD_DIM
    return (r == c).astype(BF16)


def _attention(q, k, v):
    aw = ATTN_WIDTH
    seq_spec = pl.BlockSpec((None, SEQ, aw), lambda b: (b, 0, 0))
    p4, p16 = _perm_matrix(4), _perm_matrix(16)
    consts = (p4, p4.T, p16, p16.T, _head_expand_matrix())
    return pl.pallas_call(
        _attn_kernel,
        grid=(BATCH,),
        in_specs=[seq_spec] * 3 + [_const_spec(c.shape) for c in consts],
        out_specs=seq_spec,
        out_shape=jax.ShapeDtypeStruct((BATCH, SEQ, aw), F32),
        scratch_shapes=[pltpu.VMEM((3, 4, SEQ // 4, aw), BF16),
                        pltpu.VMEM((3, 16, SEQ // 16, aw), BF16),
                        pltpu.VMEM((SEQ, LANES), F32),
                        pltpu.VMEM((4, SEQ // 4, aw), BF16),
                        pltpu.VMEM((4, SEQ // 4, LANES), F32),
                        pltpu.VMEM((16, SEQ // 16, aw), BF16),
                        pltpu.VMEM((16, SEQ // 16, LANES), F32)],
        compiler_params=_params("parallel"),
        name="attention",
    )(q, k, v, *consts)


def _s5_kernel(u_ref, pf_ref, pb_ref, bre_ref, bim_ref, lre_ref, lim_ref, cre_ref, cim_ref, d_ref,
               wglu_ref, o_ref, st_re, st_im, u_buf, xr_buf, xi_buf, y_buf):
    rows = S5_CHUNK * BATCH
    sub_rows = S5_SUB * BATCH

    @pl.when(pl.program_id(0) == 0)
    def _():
        st_re[...] = jnp.zeros_like(st_re)
        st_im[...] = jnp.zeros_like(st_im)

    for tb in range(S5_CHUNK // S5_SUB):
        t = slice(tb * S5_SUB, (tb + 1) * S5_SUB)
        piece = jnp.concatenate([u_ref[b, t, :] for b in range(BATCH)], axis=0)
        u_buf[tb * sub_rows:(tb + 1) * sub_rows, :] = _dot(pf_ref[...], piece)

    for c in range(S5_NCHUNK):
        ub = u_buf[:, c * LANES:(c + 1) * LANES].astype(BF16)
        xr_buf[c] = _dot(ub, bre_ref[c])
        xi_buf[c] = _dot(ub, bim_ref[c])

    for c in range(S5_NCHUNK):
        ch = slice(c * LANES, (c + 1) * LANES)
        stl = slice(c * S5_STATE_LANES, (c + 1) * S5_STATE_LANES)
        lr = jnp.broadcast_to(lre_ref[:, stl], (BATCH, S5_STATE_LANES))
        li = jnp.broadcast_to(lim_ref[:, stl], (BATCH, S5_STATE_LANES))
        xr, xi = st_re[:, stl], st_im[:, stl]
        for s in range(S5_CHUNK):
            sl = slice(s * BATCH, (s + 1) * BATCH)
            xr, xi = (lr * xr - li * xi + xr_buf[c, sl, :], lr * xi + li * xr + xi_buf[c, sl, :])
            xr_buf[c, sl, :] = xr
            xi_buf[c, sl, :] = xi
        st_re[:, stl] = xr
        st_im[:, stl] = xi
        yc = (_dot(xr_buf[c].astype(BF16), cre_ref[c]) - _dot(xi_buf[c].astype(BF16), cim_ref[c]))
        y_buf[:, ch] = yc + d_ref[:, ch] * u_buf[:, ch]

    y = jax.nn.gelu(y_buf[...])
    z = _dot(y.astype(BF16), wglu_ref[...])
    y_buf[...] = y * jax.nn.sigmoid(z)
    for tb in range(S5_CHUNK // S5_SUB):
        back = _dot(pb_ref[...], y_buf[tb * sub_rows:(tb + 1) * sub_rows, :].astype(BF16)).astype(BF16)
        for b in range(BATCH):
            o_ref[b, tb * S5_SUB:(tb + 1) * S5_SUB, :] = back[b * S5_SUB:(b + 1) * S5_SUB]


def _s5_discretize(lam_re, lam_im, log_dt, b_re, b_im, c_re, c_im):
    dt = jnp.exp(log_dt)[:, None]
    mag = jnp.exp(lam_re * dt)
    lb_re, lb_im = mag * jnp.cos(lam_im * dt), mag * jnp.sin(lam_im * dt)
    den = lam_re * lam_re + lam_im * lam_im
    nr, ni = lb_re - 1.0, lb_im
    f_re = (nr * lam_re + ni * lam_im) / den
    f_im = (ni * lam_re - nr * lam_im) / den
    bb_re = f_re[..., None] * b_re - f_im[..., None] * b_im
    bb_im = f_re[..., None] * b_im + f_im[..., None] * b_re
    eye = jnp.eye(S5_GROUPS_PER_CHUNK, dtype=F32)
    gpc, nch = S5_GROUPS_PER_CHUNK, S5_NCHUNK

    def b_blockdiag(b):
        b = b.reshape(nch, gpc, SSM_STATE, SSM_GROUP_CH)
        m = jnp.einsum('cgph,gk->cghkp', b, eye)
        return m.reshape(nch, gpc * SSM_GROUP_CH, gpc * SSM_STATE).astype(BF16)

    def c_blockdiag(c):
        c = c.reshape(nch, gpc, SSM_GROUP_CH, SSM_STATE)
        m = jnp.einsum('cghp,gk->cgpkh', c, eye)
        return m.reshape(nch, gpc * SSM_STATE, gpc * SSM_GROUP_CH).astype(BF16)

    return (b_blockdiag(bb_re), b_blockdiag(bb_im),
            lb_re.reshape(1, SSM_GROUPS * SSM_STATE), lb_im.reshape(1, SSM_GROUPS * SSM_STATE),
            c_blockdiag(c_re), c_blockdiag(c_im))


def _s5_reorder_matrices():
    n = S5_SUB * BATCH
    out_row = jnp.arange(n)
    src = (out_row % BATCH) * S5_SUB + out_row // BATCH
    fwd = (src[:, None] == jnp.arange(n)[None, :]).astype(BF16)
    return fwd, fwd.T


def _s5(u, disc, d_skip, w_glu):
    rows = S5_CHUNK * BATCH
    blk = pl.BlockSpec((BATCH, S5_CHUNK, SSM_WIDTH), lambda i: (0, i, 0))
    args = _s5_reorder_matrices() + tuple(disc) + (d_skip, w_glu)
    return pl.pallas_call(
        _s5_kernel,
        grid=(SEQ // S5_CHUNK,),
        in_specs=[blk] + [_const_spec(a.shape) for a in args],
        out_specs=blk,
        out_shape=jax.ShapeDtypeStruct((BATCH, SEQ, SSM_WIDTH), BF16),
        scratch_shapes=[pltpu.VMEM((BATCH, SSM_GROUPS * SSM_STATE), F32),
                        pltpu.VMEM((BATCH, SSM_GROUPS * SSM_STATE), F32),
                        pltpu.VMEM((rows, SSM_WIDTH), F32),
                        pltpu.VMEM((S5_NCHUNK, rows, S5_STATE_LANES), F32),
                        pltpu.VMEM((S5_NCHUNK, rows, S5_STATE_LANES), F32),
                        pltpu.VMEM((rows, SSM_WIDTH), F32)],
        compiler_params=_params("arbitrary"),
        name="s5",
    )(u, *args)


def _outproj_kernel(attn_ref, ssm_ref, x_ref, ga_ref, gs_ref, w_ref, g2_ref, whi_ref, wlo_ref, b_ref,
                    xa_ref, bucket_ref):
    a_n = _rms(attn_ref[...], ga_ref[...]).astype(BF16)
    s_n = _rms(ssm_ref[...].astype(F32), gs_ref[...]).astype(BF16)
    y = _dot(a_n, w_ref[0:ATTN_WIDTH, :]) + _dot(s_n, w_ref[ATTN_WIDTH:, :])
    x = x_ref[...] + y
    rec, bucket = _route(_rms(x, g2_ref[...]), whi_ref[...], wlo_ref[...], b_ref[...])
    xa_ref[:, 0:D_MODEL] = x
    xa_ref[:, D_MODEL:] = rec
    bucket_ref[...] = bucket


def _outproj(attn, ssm, x, ga, gs, w, g2, whi, wlo, bias):
    aw = ATTN_WIDTH
    return pl.pallas_call(
        _outproj_kernel,
        grid=(TOKENS // TOK_TILE,),
        in_specs=[_tok_spec(aw), _tok_spec(SSM_WIDTH), _tok_spec(D_MODEL),
                  _const_spec((1, aw)), _const_spec((1, SSM_WIDTH)),
                  _const_spec((aw + SSM_WIDTH, D_MODEL)), _const_spec((1, D_MODEL)),
                  _const_spec((ROUTER_ROWS, D_MODEL)), _const_spec((ROUTER_ROWS, D_MODEL)),
                  _const_spec((ROUTER_ROWS, 1))],
        out_specs=[_tok_spec(MOE_ROW), pl.BlockSpec((1, TOK_TILE), lambda i: (0, i))],
        out_shape=[jax.ShapeDtypeStruct((TOKENS, MOE_ROW), F32),
                   jax.ShapeDtypeStruct((1, TOKENS), F32)],
        compiler_params=_params("parallel"),
        name="outproj_router",
    )(attn, ssm, x, ga, gs, w, g2, whi, wlo, bias)


ROUTER_ROWS = 32
EXPERT_ROW0 = 8
PAIRS_PER_GROUP = EXPERTS_PER_GROUP * (EXPERTS_PER_GROUP - 1) // 2
N_BUCKETS = MOE_GROUPS * PAIRS_PER_GROUP
REC_BUCKET, REC_W_LO, REC_W_HI = 0, 1, 2
TOKENS = BATCH * SEQ
PLAN_SIDE = 128
MOE_TILE = 256
MOE_NTILES = TOKENS // MOE_TILE + N_BUCKETS
MOE_SLOTS = MOE_NTILES * MOE_TILE
MOE_ROW = D_MODEL + LANES
DUMP_ROWS = 2 * MOE_TILE
X_ROWS = TOKENS + DUMP_ROWS
MOE_BUFS = 3
SPARE_TILE = MOE_NTILES + 3
INV_LEN = (MOE_NTILES + 4) * MOE_TILE
assert PLAN_SIDE * PLAN_SIDE == TOKENS and MOE_NTILES <= PLAN_SIDE
assert INV_LEN % DUMP_ROWS == 0 and SPARE_TILE * MOE_TILE < INV_LEN
assert SPARE_TILE % 2 == 1


def _route(h, whi, wlo, bias):
    h_hi, h_lo = _split_bf16(h)
    nt = (((1,), (1,)), ((), ()))
    logits = (lax.dot_general(whi, h_hi, nt, preferred_element_type=F32)
              + lax.dot_general(wlo, h_hi, nt, preferred_element_type=F32)
              + lax.dot_general(whi, h_lo, nt, preferred_element_type=F32)
              + bias)
    ng, ne = MOE_GROUPS, EXPERTS_PER_GROUP
    gl = [logits[g:g + 1, :] for g in range(ng)]
    best, grp = gl[0], jnp.zeros_like(gl[0], dtype=jnp.int32)
    for g in range(1, ng):
        better = gl[g] > best
        grp = jnp.where(better, g, grp)
        best = jnp.where(better, gl[g], best)
    g1 = 1.0 / sum(jnp.exp(x - best) for x in gl)
    sel = []
    for e in range(ne):
        acc = jnp.zeros_like(best)
        for g in range(ng):
            r = EXPERT_ROW0 + g * ne + e
            acc = jnp.where(grp == g, logits[r:r + 1, :], acc)
        sel.append(acc)

    def first_argmax(vals):
        bv, bi = vals[0], jnp.zeros_like(grp)
        for e in range(1, ne):
            better = vals[e] > bv
            bi = jnp.where(better, e, bi)
            bv = jnp.where(better, vals[e], bv)
        return bv, bi

    v1, i1 = first_argmax(sel)
    v2, i2 = first_argmax([jnp.where(i1 == e, -jnp.inf, sel[e]) for e in range(ne)])
    e2 = jnp.exp(v2 - v1)
    w1 = g1 / (1.0 + e2)
    w2 = g1 * e2 / (1.0 + e2)
    first_is_low = i1 < i2
    lo = jnp.where(first_is_low, i1, i2)
    hi = jnp.where(first_is_low, i2, i1)
    pair = jnp.where(lo == 0, 0, jnp.where(lo == 1, 3, 5)) + hi - lo - 1
    bucket = (grp * PAIRS_PER_GROUP + pair).astype(F32)
    w_lo = jnp.where(first_is_low, w1, w2)
    w_hi = jnp.where(first_is_low, w2, w1)
    tokens = logits.shape[1]
    rowid = lax.broadcasted_iota(jnp.int32, (LANES, tokens), 0)
    table = jnp.where(rowid == REC_BUCKET, bucket,
                      jnp.where(rowid == REC_W_LO, w_lo, jnp.where(rowid == REC_W_HI, w_hi, 0.0)))
    return table.T, bucket


def _plan_kernel(bucket_ref, pos_ref, tile_bucket_ref):
    n = PLAN_SIDE
    bucket = bucket_ref[...]
    r = lax.broadcasted_iota(jnp.int32, (n, n), 0)
    c = lax.broadcasted_iota(jnp.int32, (n, n), 1)
    before_in_row = (r < c).astype(BF16)
    rows_before = (c < r).astype(BF16)
    ones = jnp.ones((n, n), BF16)
    tile_start = (c * MOE_TILE).astype(F32)
    pos = jnp.zeros((n, n), F32)
    base = jnp.zeros((n, n), F32)
    ended = jnp.zeros((n, n), F32)
    for k in range(N_BUCKETS):
        member = bucket == float(k)
        mb = member.astype(BF16)
        in_row = _dot(mb, before_in_row)
        row_count = _dot(mb, ones).astype(BF16)
        rank = _dot(rows_before, row_count) + in_row
        total = _dot(ones, row_count)
        pos = jnp.where(member, base + rank, pos)
        base = base + jnp.ceil(total * (1.0 / MOE_TILE)) * MOE_TILE
        ended = ended + (tile_start >= base).astype(F32)
    pos_ref[...] = pos.astype(jnp.int32)
    tile_bucket_ref[...] = ended.astype(jnp.int32)


def _plan(bucket):
    n = PLAN_SIDE
    pos, tile_bucket = pl.pallas_call(
        _plan_kernel,
        out_shape=[jax.ShapeDtypeStruct((n, n), jnp.int32)] * 2,
        name="moe_plan",
    )(bucket.reshape(n, n))
    return pos.reshape(TOKENS), tile_bucket[0, :MOE_NTILES]


def _invert_kernel(pos_ref, inv_ref):
    def init(blk, carry):
        for r in range(DUMP_ROWS):
            inv_ref[blk * DUMP_ROWS + r] = TOKENS + r
        return carry

    def place(t, carry):
        inv_ref[pos_ref[t]] = t
        return carry

    lax.fori_loop(0, INV_LEN // DUMP_ROWS, init, 0)
    lax.fori_loop(0, TOKENS, place, 0, unroll=16)


def _invert(pos):
    smem = pl.BlockSpec(memory_space=pltpu.SMEM)
    return pl.pallas_call(
        _invert_kernel,
        in_specs=[smem],
        out_specs=smem,
        out_shape=jax.ShapeDtypeStruct((INV_LEN,), jnp.int32),
        name="moe_invert",
    )(pos)


def _row_copy(src, src_row, dst, dst_row, sem):
    return pltpu.make_async_copy(src.at[pl.ds(src_row, 1), :], dst.at[pl.ds(dst_row, 1), :], sem)


def _experts_kernel(inv_ref, e_lo_ref, e_hi_ref, nused_ref, xa_ref, g_ref, fg_ref,
                    wg_lo, wu_lo, wd_lo, wg_hi, wu_hi, wd_hi, xo_ref,
                    xin, yout, gsem, ssem, fsem, *, final_norm):
    i = pl.program_id(0)
    n_used = nused_ref[0]
    s = lax.rem(i, MOE_BUFS)
    s_prev = lax.rem(i + 2, MOE_BUFS)
    s_next = lax.rem(i + 1, MOE_BUFS)

    def start_gather(tile, slot):
        for r in range(MOE_TILE):
            src = jnp.minimum(inv_ref[tile * MOE_TILE + r], TOKENS - 1)
            _row_copy(xa_ref, src, xin.at[slot], r, gsem.at[slot]).start(priority=r % 2)

    def start_scatter(tile, slot):
        for r in range(MOE_TILE):
            _row_copy(yout.at[slot], r, xo_ref, inv_ref[tile * MOE_TILE + r],
                      ssem.at[slot]).start(priority=r % 2)

    def wait_gather(slot):
        pltpu.make_async_copy(xa_ref.at[pl.ds(0, MOE_TILE), :], xin.at[slot], gsem.at[slot]).wait()

    def wait_scatter(slot):
        pltpu.make_async_copy(yout.at[slot], xo_ref.at[pl.ds(0, MOE_TILE), :], ssem.at[slot]).wait()

    @pl.when(i == 0)
    def _():
        yout[MOE_BUFS - 1] = jnp.zeros((MOE_TILE, D_MODEL), F32)
        for half in range(DUMP_ROWS // MOE_TILE):
            fill = pltpu.make_async_copy(
                yout.at[MOE_BUFS - 1], xo_ref.at[pl.ds(TOKENS + half * MOE_TILE, MOE_TILE), :], fsem)
            fill.start()
            fill.wait()
        start_gather(0, 0)
        start_gather(1, 1)

    @pl.when(i <= n_used)
    def _():
        wait_gather(s)

        @pl.when(i >= 2)
        def _():
            wait_scatter(s)

        start_gather(i + 2, s_prev)
        start_scatter(jnp.where(i >= 1, i - 1, SPARE_TILE), s_prev)

        xt = xin[s]
        x_rows = xt[:, 0:D_MODEL]
        rec = xt[:, D_MODEL:]
        h = _rms(x_rows, g_ref[...]).astype(BF16)

        def expert(wg, wu, wd, lane):
            hg = _dot(h, wg[...].astype(BF16))
            hu = _dot(h, wu[...].astype(BF16))
            act = jax.nn.silu(hg) * hu * rec[:, lane:lane + 1]
            return _dot(act.astype(BF16), wd[...].astype(BF16))

        out = x_rows + expert(wg_lo, wu_lo, wd_lo, REC_W_LO) + expert(wg_hi, wu_hi, wd_hi, REC_W_HI)
        yout[s] = _rms(out, fg_ref[...]) if final_norm else out

        @pl.when(i == n_used)
        def _():
            wait_gather(s_next)
            wait_gather(s_prev)
            wait_scatter(s_next)
            wait_scatter(s_prev)


def _experts(layer, inv, e_lo, e_hi, n_used, xa, g, final_g, w_gate, w_up, w_down):
    final_norm = final_g is not None
    if not final_norm:
        final_g = jnp.ones((1, D_MODEL), F32)
    up = lambda pick: pl.BlockSpec((None, None, D_MODEL, EXPERT_FF),
                                   lambda i, inv, lo, hi, nu: (layer, pick(lo, hi)[i], 0, 0))
    down = lambda pick: pl.BlockSpec((None, None, EXPERT_FF, D_MODEL),
                                     lambda i, inv, lo, hi, nu: (layer, pick(lo, hi)[i], 0, 0))
    first, second = (lambda lo, hi: lo), (lambda lo, hi: hi)
    grid_spec = pltpu.PrefetchScalarGridSpec(
        num_scalar_prefetch=4,
        grid=(MOE_NTILES + 1,),
        in_specs=[pl.BlockSpec(memory_space=pl.ANY), _const_spec((1, D_MODEL)), _const_spec((1, D_MODEL)),
                  up(first), up(first), down(first), up(second), up(second), down(second)],
        out_specs=pl.BlockSpec(memory_space=pl.ANY),
        scratch_shapes=[pltpu.VMEM((MOE_BUFS, MOE_TILE, MOE_ROW), F32),
                        pltpu.VMEM((MOE_BUFS, MOE_TILE, D_MODEL), F32),
                        pltpu.SemaphoreType.DMA((MOE_BUFS,)),
                        pltpu.SemaphoreType.DMA((MOE_BUFS,)),
                        pltpu.SemaphoreType.DMA(())])
    return pl.pallas_call(
        functools.partial(_experts_kernel, final_norm=final_norm),
        grid_spec=grid_spec,
        out_shape=jax.ShapeDtypeStruct((X_ROWS, D_MODEL), F32),
        compiler_params=_params("arbitrary"),
        name="moe_experts_final" if final_norm else "moe_experts",
    )(inv, e_lo, e_hi, n_used, xa, g, final_g, w_gate, w_up, w_down, w_gate, w_up, w_down)


def _bucket_experts():
    pairs = [(a, b) for a in range(EXPERTS_PER_GROUP) for b in range(a + 1, EXPERTS_PER_GROUP)]
    lo = [g * EXPERTS_PER_GROUP + a for g in range(MOE_GROUPS) for a, _ in pairs]
    hi = [g * EXPERTS_PER_GROUP + b for g in range(MOE_GROUPS) for _, b in pairs]
    return jnp.array(lo, jnp.int32), jnp.array(hi, jnp.int32)


def _routed_moe(layer, xa, bucket, g, w_gate, w_up, w_down, final_g=None):
    pos, tile_bucket = _plan(bucket)
    n_used = jnp.sum((tile_bucket < N_BUCKETS).astype(jnp.int32)).reshape(1)
    tile_bucket = jnp.minimum(jnp.concatenate([tile_bucket, tile_bucket[-1:]]), N_BUCKETS - 1)
    lo, hi = _bucket_experts()
    return _experts(layer, _invert(pos), lo[tile_bucket], hi[tile_bucket], n_used, xa, g, final_g,
                    w_gate, w_up, w_down)


def kernel(x, ln1_g, w_in, lam_re, lam_im, log_dt, b_re, b_im, c_re, c_im, d_skip, w_glu,
           gn_attn, gn_ssm, w_out, ln2_g, w_router_grp, b_router_grp, w_router_exp,
           b_router_exp, w_gate, w_up, w_down, final_g):
    assert x.shape == (BATCH, SEQ, D_MODEL)
    ng = MOE_GROUPS
    x = x.reshape(TOKENS, D_MODEL)

    for l in range(DEPTH):
        q, k, v, u = _inproj(x, ln1_g[l][None, :], w_in[l].astype(BF16))
        attn = _attention(q, k, v).reshape(TOKENS, ATTN_WIDTH)
        disc = _s5_discretize(lam_re[l], lam_im[l], log_dt[l], b_re[l], b_im[l], c_re[l], c_im[l])
        ssm = _s5(u, disc, d_skip[l].reshape(1, SSM_WIDTH), w_glu[l].astype(BF16))
        ssm = ssm.reshape(TOKENS, SSM_WIDTH)

        w_r = jnp.zeros((ROUTER_ROWS, D_MODEL), F32)
        w_r = w_r.at[0:ng].set(w_router_grp[l].T)
        w_r = w_r.at[EXPERT_ROW0:EXPERT_ROW0 + N_EXPERTS].set(
            jnp.transpose(w_router_exp[l], (0, 2, 1)).reshape(N_EXPERTS, D_MODEL))
        b_r = jnp.zeros((ROUTER_ROWS, 1), F32)
        b_r = b_r.at[0:ng, 0].set(b_router_grp[l])
        b_r = b_r.at[EXPERT_ROW0:EXPERT_ROW0 + N_EXPERTS, 0].set(b_router_exp[l].reshape(N_EXPERTS))
        w_r_hi, w_r_lo = _split_bf16(w_r)
        g2 = ln2_g[l][None, :]
        xa, bucket = _outproj(attn, ssm, x, gn_attn[l][None, :], gn_ssm[l][None, :],
                              w_out[l].astype(BF16), g2, w_r_hi, w_r_lo, b_r)
        x = _routed_moe(l, xa, bucket, g2, w_gate, w_up, w_down,
                        final_g[None, :] if l == DEPTH - 1 else None)
    return x[:TOKENS].reshape(BATCH, SEQ, D_MODEL)
```

```python
import functools
import math

import jax
import jax.numpy as jnp
from jax import lax
from jax.experimental import pallas as pl
from jax.experimental.pallas import tpu as pltpu

D_MODEL = 1024
BATCH = 8
SEQ = 2048
DEPTH = 4
ATTN_WIDTH = 512
HEAD_DIM = 64
ATTN_HEADS = 8
DILATIONS = (1, 4, 16)
SPAN = 128
SSM_WIDTH = 512
SSM_GROUP_CH = 16
SSM_GROUPS = 32
SSM_STATE = 64
MOE_GROUPS = 4
EXPERTS_PER_GROUP = 4
N_EXPERTS = 16
EXPERT_FF = 256
RMS_EPS = 1e-6

LANES = 128
BF16_ROWS = 16
VMEM_LIMIT_BYTES = 56 * 1024 * 1024

TOK_TILE = 1024
PERM_TILE = 256
S5_CHUNK = 128
S5_GROUPS_PER_CHUNK = 8
S5_NCHUNK = SSM_GROUPS // S5_GROUPS_PER_CHUNK
S5_STATE_LANES = S5_GROUPS_PER_CHUNK * SSM_STATE
S5_SUB = BF16_ROWS

F32 = jnp.float32
BF16 = jnp.bfloat16


def _params(*sem):
    return pltpu.CompilerParams(dimension_semantics=sem, vmem_limit_bytes=VMEM_LIMIT_BYTES)


def _rms(x, g):
    return x * lax.rsqrt(jnp.mean(x * x, axis=-1, keepdims=True) + RMS_EPS) * g


def _split_bf16(a):
    hi = a.astype(BF16)
    lo = (a - hi.astype(F32)).astype(BF16)
    return hi, lo


def _dot(a, b):
    return jnp.dot(a, b, preferred_element_type=F32)


def _dot_hilo(p, a):
    hi, lo = _split_bf16(a)
    return _dot(p, hi) + _dot(p, lo)


def _tok_spec(width):
    return pl.BlockSpec((TOK_TILE, width), lambda i: (i, 0))


def _const_spec(shape):
    return pl.BlockSpec(shape, lambda *_: (0,) * len(shape))


def _inproj_kernel(x_ref, g_ref, w_ref, q_ref, k_ref, v_ref, u_ref):
    h = _rms(x_ref[...], g_ref[...]).astype(BF16)
    p = _dot(h, w_ref[...])
    aw = ATTN_WIDTH
    q_ref[...] = (p[:, :aw] * (1.0 / math.sqrt(HEAD_DIM))).astype(BF16)
    k_ref[...] = p[:, aw:2 * aw].astype(BF16)
    v_ref[...] = p[:, 2 * aw:3 * aw].astype(BF16)
    u_ref[...] = p[:, 3 * aw:].astype(BF16)


def _inproj(x, g, w):
    aw = ATTN_WIDTH
    out = jax.ShapeDtypeStruct((TOKENS, aw), BF16)
    outs = pl.pallas_call(
        _inproj_kernel,
        grid=(TOKENS // TOK_TILE,),
        in_specs=[_tok_spec(D_MODEL), _const_spec((1, D_MODEL)),
                  _const_spec((D_MODEL, 3 * aw + SSM_WIDTH))],
        out_specs=[_tok_spec(aw)] * 4,
        out_shape=[out] * 4,
        compiler_params=_params("parallel"),
        name="inproj",
    )(x, g, w)
    return [o.reshape(BATCH, SEQ, aw) for o in outs]


def _attn_block(q, kk, vv, bias, low_half, lane):
    return _attn_blocks([(q, kk, vv)], bias, low_half, lane)[0]


def _attn_blocks(blocks, bias, low_half, lane):
    heads = range(ATTN_HEADS)
    nt = (((1,), (1,)), ((), ()))
    scores = []
    for q, kk, _ in blocks:
        for h in heads:
            sl = slice((h // 2) * LANES, (h // 2 + 1) * LANES)
            keep = low_half if h % 2 == 0 else jnp.logical_not(low_half)
            qm = jnp.where(keep, q[:, sl], jnp.zeros_like(q[:, sl]))
            scores.append(lax.dot_general(qm, kk[:, sl], nt, preferred_element_type=F32) + bias)
    probs, dens, lses = [], [], []
    for s in scores:
        m = jnp.max(s, axis=-1, keepdims=True)
        p = jnp.exp(s - m)
        den = jnp.sum(p, axis=-1, keepdims=True)
        probs.append(p.astype(BF16))
        dens.append(den)
        lses.append(m + jnp.log(den))
    results = []
    for b, (_, _, vv) in enumerate(blocks):
        base = b * ATTN_HEADS
        pv = [_dot(probs[base + h], vv[:, (h // 2) * LANES:(h // 2 + 1) * LANES]) / dens[base + h]
              for h in heads]
        lse_tile = jnp.zeros((SPAN, LANES), F32)
        for h in heads:
            lse_tile = jnp.where(lane == h, lses[base + h], lse_tile)
        outs = [jnp.where(low_half, pv[2 * j], pv[2 * j + 1]) for j in range(ATTN_HEADS // 2)]
        results.append((jnp.concatenate(outs, axis=-1), lse_tile))
    return results


def _attn_kernel(q_ref, k_ref, v_ref, p4_ref, p4t_ref, p16_ref, p16t_ref, e_ref, o_ref,
                 qkv4, qkv16, lse1, o4p, lse4p, o16p, lse16p):
    lane = lax.broadcasted_iota(jnp.int32, (SPAN, LANES), 1)
    low_half = lane < HEAD_DIM
    row0 = lax.broadcasted_iota(jnp.int32, (SPAN, SPAN), 0)
    col0 = lax.broadcasted_iota(jnp.int32, (SPAN, SPAN), 1)
    bias_first = jnp.where(col0 <= row0, 0.0, -jnp.inf).astype(F32)
    row = lax.broadcasted_iota(jnp.int32, (SPAN, 2 * SPAN), 0)
    col = lax.broadcasted_iota(jnp.int32, (SPAN, 2 * SPAN), 1)
    bias_band = jnp.where((col >= row) & (col <= row + SPAN), 0.0, -jnp.inf).astype(F32)
    block = functools.partial(_attn_block, low_half=low_half, lane=lane)
    srcs = (q_ref, k_ref, v_ref)

    def permute(i, carry):
        rows = pl.ds(pl.multiple_of(i * PERM_TILE, PERM_TILE), PERM_TILE)
        for a, src in enumerate(srcs):
            x = src[rows, :]
            y4 = _dot(p4_ref[...], x).astype(BF16)
            y16 = _dot(p16_ref[...], x).astype(BF16)
            n4, n16 = PERM_TILE // 4, PERM_TILE // 16
            for r in range(4):
                qkv4[a, r, pl.ds(pl.multiple_of(i * n4, n4), n4), :] = y4[r * n4:(r + 1) * n4]
            for r in range(16):
                qkv16[a, r, pl.ds(pl.multiple_of(i * n16, n16), n16), :] = y16[r * n16:(r + 1) * n16]
        return carry

    lax.fori_loop(0, SEQ // PERM_TILE, permute, 0)

    def q_rows(n):
        return pl.ds(pl.multiple_of(n * SPAN, SPAN), SPAN)

    def k_rows(n):
        return pl.ds(pl.multiple_of((n - 1) * SPAN, SPAN), 2 * SPAN)

    def band_blocks(items):
        results = _attn_blocks([(get(0, q_rows(n)), get(1, k_rows(n)), get(2, k_rows(n)))
                                for get, _, n in items], bias_band, low_half, lane)
        for (_, put, n), (o, lse) in zip(items, results):
            put(q_rows(n), o, lse)

    def first_blocks(items):
        head = pl.ds(0, SPAN)
        results = _attn_blocks([(get(0, head), get(1, head), get(2, head)) for get, _ in items],
                               bias_first, low_half, lane)
        for (_, put), (o, lse) in zip(items, results):
            put(head, o, lse)

    def get1(a, rows):
        return srcs[a][rows, :]

    def put1(rows, o, lse):
        o_ref[rows, :] = o
        lse1[rows, :] = lse

    n1 = SEQ // SPAN
    first_blocks([(get1, put1)])

    def pair1(j, carry):
        band_blocks([(get1, put1, 1 + 2 * j), (get1, put1, 2 + 2 * j)])
        return carry

    lax.fori_loop(0, (n1 - 1) // 2, pair1, 0)
    if (n1 - 1) % 2:
        band_blocks([(get1, put1, n1 - 1)])

    def access4(r):
        def get(a, rows):
            return qkv4[a, r, rows, :]

        def put(rows, o, lse):
            o4p[r, rows, :] = o.astype(BF16)
            lse4p[r, rows, :] = lse
        return get, put

    def class4_pair(j, carry):
        pair = [access4(2 * j), access4(2 * j + 1)]
        first_blocks(pair)

        def body(n, c):
            band_blocks([(get, put, n) for get, put in pair])
            return c

        lax.fori_loop(1, SEQ // 4 // SPAN, body, 0)
        return carry

    lax.fori_loop(0, 2, class4_pair, 0)

    assert SEQ // 16 == SPAN

    def class16_pair(j, carry):
        classes = (2 * j, 2 * j + 1)
        results = _attn_blocks([tuple(qkv16[a, r] for a in range(3)) for r in classes],
                               bias_first, low_half, lane)
        for r, (o, lse) in zip(classes, results):
            o16p[r] = o.astype(BF16)
            lse16p[r] = lse
        return carry

    lax.fori_loop(0, 8, class16_pair, 0)

    def expand(w):
        hi, lo = _split_bf16(w)
        return _dot(hi, e_ref[...]) + _dot(lo, e_ref[...])

    def combine(i, carry):
        rows = pl.ds(pl.multiple_of(i * PERM_TILE, PERM_TILE), PERM_TILE)
        n4, n16 = PERM_TILE // 4, PERM_TILE // 16
        r4 = pl.ds(pl.multiple_of(i * n4, n4), n4)
        r16 = pl.ds(pl.multiple_of(i * n16, n16), n16)
        o4 = _dot(p4t_ref[...], jnp.concatenate([o4p[r, r4, :] for r in range(4)], axis=0))
        l4 = _dot_hilo(p4t_ref[...], jnp.concatenate([lse4p[r, r4, :] for r in range(4)], axis=0))
        o16 = _dot(p16t_ref[...], jnp.concatenate([o16p[r, r16, :] for r in range(16)], axis=0))
        l16 = _dot_hilo(p16t_ref[...], jnp.concatenate([lse16p[r, r16, :] for r in range(16)], axis=0))
        l1 = lse1[rows, :]
        m = jnp.maximum(jnp.maximum(l1, l4), l16)
        e1, e4, e16 = jnp.exp(l1 - m), jnp.exp(l4 - m), jnp.exp(l16 - m)
        den = e1 + e4 + e16
        w4, w16 = expand(e4 / den), expand(e16 / den)
        o1 = o_ref[rows, :]
        o_ref[rows, :] = o1 + w4 * (o4 - o1) + w16 * (o16 - o1)
        return carry

    lax.fori_loop(0, SEQ // PERM_TILE, combine, 0)


def _perm_matrix(dil):
    n = PERM_TILE // dil
    out_row = jnp.arange(PERM_TILE)
    src = dil * (out_row % n) + out_row // n
    return (src[:, None] == jnp.arange(PERM_TILE)[None, :]).astype(BF16)


def _head_expand_matrix():
    r = jnp.arange(LANES)[:, None]
    c = jnp.arange(ATTN_WIDTH)[None, :] // HEAD_DIM
    return (r == c).astype(BF16)


def _attention(q, k, v):
    aw = ATTN_WIDTH
    seq_spec = pl.BlockSpec((None, SEQ, aw), lambda b: (b, 0, 0))
    p4, p16 = _perm_matrix(4), _perm_matrix(16)
    consts = (p4, p4.T, p16, p16.T, _head_expand_matrix())
    return pl.pallas_call(
        _attn_kernel,
        grid=(BATCH,),
        in_specs=[seq_spec] * 3 + [_const_spec(c.shape) for c in consts],
        out_specs=seq_spec,
        out_shape=jax.ShapeDtypeStruct((BATCH, SEQ, aw), F32),
        scratch_shapes=[pltpu.VMEM((3, 4, SEQ // 4, aw), BF16),
                        pltpu.VMEM((3, 16, SEQ // 16, aw), BF16),
                        pltpu.VMEM((SEQ, LANES), F32),
                        pltpu.VMEM((4, SEQ // 4, aw), BF16),
                        pltpu.VMEM((4, SEQ // 4, LANES), F32),
                        pltpu.VMEM((16, SEQ // 16, aw), BF16),
                        pltpu.VMEM((16, SEQ // 16, LANES), F32)],
        compiler_params=_params("parallel"),
        name="attention",
    )(q, k, v, *consts)


def _s5_kernel(u_ref, pf_ref, pb_ref, bre_ref, bim_ref, lre_ref, lim_ref, cre_ref, cim_ref, d_ref,
               wglu_ref, o_ref, st_re, st_im, u_buf, xr_buf, xi_buf, y_buf):
    rows = S5_CHUNK * BATCH
    sub_rows = S5_SUB * BATCH

    @pl.when(pl.program_id(0) == 0)
    def _():
        st_re[...] = jnp.zeros_like(st_re)
        st_im[...] = jnp.zeros_like(st_im)

    for tb in range(S5_CHUNK // S5_SUB):
        t = slice(tb * S5_SUB, (tb + 1) * S5_SUB)
        piece = jnp.concatenate([u_ref[b, t, :] for b in range(BATCH)], axis=0)
        u_buf[tb * sub_rows:(tb + 1) * sub_rows, :] = _dot(pf_ref[...], piece)

    for c in range(S5_NCHUNK):
        ub = u_buf[:, c * LANES:(c + 1) * LANES].astype(BF16)
        xr_buf[c] = _dot(ub, bre_ref[c])
        xi_buf[c] = _dot(ub, bim_ref[c])

    for c in range(S5_NCHUNK):
        ch = slice(c * LANES, (c + 1) * LANES)
        stl = slice(c * S5_STATE_LANES, (c + 1) * S5_STATE_LANES)
        lr = jnp.broadcast_to(lre_ref[:, stl], (BATCH, S5_STATE_LANES))
        li = jnp.broadcast_to(lim_ref[:, stl], (BATCH, S5_STATE_LANES))
        xr, xi = st_re[:, stl], st_im[:, stl]
        for s in range(S5_CHUNK):
            sl = slice(s * BATCH, (s + 1) * BATCH)
            xr, xi = (lr * xr - li * xi + xr_buf[c, sl, :], lr * xi + li * xr + xi_buf[c, sl, :])
            xr_buf[c, sl, :] = xr
            xi_buf[c, sl, :] = xi
        st_re[:, stl] = xr
        st_im[:, stl] = xi
        yc = (_dot(xr_buf[c].astype(BF16), cre_ref[c]) - _dot(xi_buf[c].astype(BF16), cim_ref[c]))
        y_buf[:, ch] = yc + d_ref[:, ch] * u_buf[:, ch]

    y = jax.nn.gelu(y_buf[...])
    z = _dot(y.astype(BF16), wglu_ref[...])
    y_buf[...] = y * jax.nn.sigmoid(z)
    for tb in range(S5_CHUNK // S5_SUB):
        back = _dot(pb_ref[...], y_buf[tb * sub_rows:(tb + 1) * sub_rows, :].astype(BF16)).astype(BF16)
        for b in range(BATCH):
            o_ref[b, tb * S5_SUB:(tb + 1) * S5_SUB, :] = back[b * S5_SUB:(b + 1) * S5_SUB]


def _s5_discretize(lam_re, lam_im, log_dt, b_re, b_im, c_re, c_im):
    dt = jnp.exp(log_dt)[:, None]
    mag = jnp.exp(lam_re * dt)
    lb_re, lb_im = mag * jnp.cos(lam_im * dt), mag * jnp.sin(lam_im * dt)
    den = lam_re * lam_re + lam_im * lam_im
    nr, ni = lb_re - 1.0, lb_im
    f_re = (nr * lam_re + ni * lam_im) / den
    f_im = (ni * lam_re - nr * lam_im) / den
    bb_re = f_re[..., None] * b_re - f_im[..., None] * b_im
    bb_im = f_re[..., None] * b_im + f_im[..., None] * b_re
    eye = jnp.eye(S5_GROUPS_PER_CHUNK, dtype=F32)
    gpc, nch = S5_GROUPS_PER_CHUNK, S5_NCHUNK

    def b_blockdiag(b):
        b = b.reshape(nch, gpc, SSM_STATE, SSM_GROUP_CH)
        m = jnp.einsum('cgph,gk->cghkp', b, eye)
        return m.reshape(nch, gpc * SSM_GROUP_CH, gpc * SSM_STATE).astype(BF16)

    def c_blockdiag(c):
        c = c.reshape(nch, gpc, SSM_GROUP_CH, SSM_STATE)
        m = jnp.einsum('cghp,gk->cgpkh', c, eye)
        return m.reshape(nch, gpc * SSM_STATE, gpc * SSM_GROUP_CH).astype(BF16)

    return (b_blockdiag(bb_re), b_blockdiag(bb_im),
            lb_re.reshape(1, SSM_GROUPS * SSM_STATE), lb_im.reshape(1, SSM_GROUPS * SSM_STATE),
            c_blockdiag(c_re), c_blockdiag(c_im))


def _s5_reorder_matrices():
    n = S5_SUB * BATCH
    out_row = jnp.arange(n)
    src = (out_row % BATCH) * S5_SUB + out_row // BATCH
    fwd = (src[:, None] == jnp.arange(n)[None, :]).astype(BF16)
    return fwd, fwd.T


def _s5(u, disc, d_skip, w_glu):
    rows = S5_CHUNK * BATCH
    blk = pl.BlockSpec((BATCH, S5_CHUNK, SSM_WIDTH), lambda i: (0, i, 0))
    args = _s5_reorder_matrices() + tuple(disc) + (d_skip, w_glu)
    return pl.pallas_call(
        _s5_kernel,
        grid=(SEQ // S5_CHUNK,),
        in_specs=[blk] + [_const_spec(a.shape) for a in args],
        out_specs=blk,
        out_shape=jax.ShapeDtypeStruct((BATCH, SEQ, SSM_WIDTH), BF16),
        scratch_shapes=[pltpu.VMEM((BATCH, SSM_GROUPS * SSM_STATE), F32),
                        pltpu.VMEM((BATCH, SSM_GROUPS * SSM_STATE), F32),
                        pltpu.VMEM((rows, SSM_WIDTH), F32),
                        pltpu.VMEM((S5_NCHUNK, rows, S5_STATE_LANES), F32),
                        pltpu.VMEM((S5_NCHUNK, rows, S5_STATE_LANES), F32),
                        pltpu.VMEM((rows, SSM_WIDTH), F32)],
        compiler_params=_params("arbitrary"),
        name="s5",
    )(u, *args)


def _outproj_kernel(attn_ref, ssm_ref, x_ref, ga_ref, gs_ref, w_ref, g2_ref, whi_ref, wlo_ref, b_ref,
                    xa_ref, bucket_ref):
    a_n = _rms(attn_ref[...], ga_ref[...]).astype(BF16)
    s_n = _rms(ssm_ref[...].astype(F32), gs_ref[...]).astype(BF16)
    y = _dot(a_n, w_ref[0:ATTN_WIDTH, :]) + _dot(s_n, w_ref[ATTN_WIDTH:, :])
    x = x_ref[...] + y
    rec, bucket = _route(_rms(x, g2_ref[...]), whi_ref[...], wlo_ref[...], b_ref[...])
    xa_ref[:, 0:D_MODEL] = x
    xa_ref[:, D_MODEL:] = rec
    bucket_ref[...] = bucket


def _outproj(attn, ssm, x, ga, gs, w, g2, whi, wlo, bias):
    aw = ATTN_WIDTH
    return pl.pallas_call(
        _outproj_kernel,
        grid=(TOKENS // TOK_TILE,),
        in_specs=[_tok_spec(aw), _tok_spec(SSM_WIDTH), _tok_spec(D_MODEL),
                  _const_spec((1, aw)), _const_spec((1, SSM_WIDTH)),
                  _const_spec((aw + SSM_WIDTH, D_MODEL)), _const_spec((1, D_MODEL)),
                  _const_spec((ROUTER_ROWS, D_MODEL)), _const_spec((ROUTER_ROWS, D_MODEL)),
                  _const_spec((ROUTER_ROWS, 1))],
        out_specs=[_tok_spec(MOE_ROW), pl.BlockSpec((1, TOK_TILE), lambda i: (0, i))],
        out_shape=[jax.ShapeDtypeStruct((TOKENS, MOE_ROW), F32),
                   jax.ShapeDtypeStruct((1, TOKENS), F32)],
        compiler_params=_params("parallel"),
        name="outproj_router",
    )(attn, ssm, x, ga, gs, w, g2, whi, wlo, bias)


ROUTER_ROWS = 32
EXPERT_ROW0 = 8
PAIRS_PER_GROUP = EXPERTS_PER_GROUP * (EXPERTS_PER_GROUP - 1) // 2
N_BUCKETS = MOE_GROUPS * PAIRS_PER_GROUP
REC_BUCKET, REC_W_LO, REC_W_HI = 0, 1, 2
TOKENS = BATCH * SEQ
PLAN_SIDE = 128
MOE_TILE = 256
MOE_NTILES = TOKENS // MOE_TILE + N_BUCKETS
MOE_SLOTS = MOE_NTILES * MOE_TILE
MOE_ROW = D_MODEL + LANES
DUMP_ROWS = 2 * MOE_TILE
X_ROWS = TOKENS + DUMP_ROWS
MOE_BUFS = 3
SPARE_TILE = MOE_NTILES + 3
INV_LEN = (MOE_NTILES + 4) * MOE_TILE
assert PLAN_SIDE * PLAN_SIDE == TOKENS and MOE_NTILES <= PLAN_SIDE
assert INV_LEN % DUMP_ROWS == 0 and SPARE_TILE * MOE_TILE < INV_LEN
assert SPARE_TILE % 2 == 1


def _route(h, whi, wlo, bias):
    h_hi, h_lo = _split_bf16(h)
    nt = (((1,), (1,)), ((), ()))
    logits = (lax.dot_general(whi, h_hi, nt, preferred_element_type=F32)
              + lax.dot_general(wlo, h_hi, nt, preferred_element_type=F32)
              + lax.dot_general(whi, h_lo, nt, preferred_element_type=F32)
              + bias)
    ng, ne = MOE_GROUPS, EXPERTS_PER_GROUP
    gl = [logits[g:g + 1, :] for g in range(ng)]
    best, grp = gl[0], jnp.zeros_like(gl[0], dtype=jnp.int32)
    for g in range(1, ng):
        better = gl[g] > best
        grp = jnp.where(better, g, grp)
        best = jnp.where(better, gl[g], best)
    g1 = 1.0 / sum(jnp.exp(x - best) for x in gl)
    sel = []
    for e in range(ne):
        acc = jnp.zeros_like(best)
        for g in range(ng):
            r = EXPERT_ROW0 + g * ne + e
            acc = jnp.where(grp == g, logits[r:r + 1, :], acc)
        sel.append(acc)

    def first_argmax(vals):
        bv, bi = vals[0], jnp.zeros_like(grp)
        for e in range(1, ne):
            better = vals[e] > bv
            bi = jnp.where(better, e, bi)
            bv = jnp.where(better, vals[e], bv)
        return bv, bi

    v1, i1 = first_argmax(sel)
    v2, i2 = first_argmax([jnp.where(i1 == e, -jnp.inf, sel[e]) for e in range(ne)])
    e2 = jnp.exp(v2 - v1)
    w1 = g1 / (1.0 + e2)
    w2 = g1 * e2 / (1.0 + e2)
    first_is_low = i1 < i2
    lo = jnp.where(first_is_low, i1, i2)
    hi = jnp.where(first_is_low, i2, i1)
    pair = jnp.where(lo == 0, 0, jnp.where(lo == 1, 3, 5)) + hi - lo - 1
    bucket = (grp * PAIRS_PER_GROUP + pair).astype(F32)
    w_lo = jnp.where(first_is_low, w1, w2)
    w_hi = jnp.where(first_is_low, w2, w1)
    tokens = logits.shape[1]
    rowid = lax.broadcasted_iota(jnp.int32, (LANES, tokens), 0)
    table = jnp.where(rowid == REC_BUCKET, bucket,
                      jnp.where(rowid == REC_W_LO, w_lo, jnp.where(rowid == REC_W_HI, w_hi, 0.0)))
    return table.T, bucket


def _plan_kernel(bucket_ref, pos_ref, tile_bucket_ref):
    n = PLAN_SIDE
    bucket = bucket_ref[...]
    r = lax.broadcasted_iota(jnp.int32, (n, n), 0)
    c = lax.broadcasted_iota(jnp.int32, (n, n), 1)
    before_in_row = (r < c).astype(BF16)
    rows_before = (c < r).astype(BF16)
    ones = jnp.ones((n, n), BF16)
    tile_start = (c * MOE_TILE).astype(F32)
    pos = jnp.zeros((n, n), F32)
    base = jnp.zeros((n, n), F32)
    ended = jnp.zeros((n, n), F32)
    for k in range(N_BUCKETS):
        member = bucket == float(k)
        mb = member.astype(BF16)
        in_row = _dot(mb, before_in_row)
        row_count = _dot(mb, ones).astype(BF16)
        rank = _dot(rows_before, row_count) + in_row
        total = _dot(ones, row_count)
        pos = jnp.where(member, base + rank, pos)
        base = base + jnp.ceil(total * (1.0 / MOE_TILE)) * MOE_TILE
        ended = ended + (tile_start >= base).astype(F32)
    pos_ref[...] = pos.astype(jnp.int32)
    tile_bucket_ref[...] = ended.astype(jnp.int32)


def _plan(bucket):
    n = PLAN_SIDE
    pos, tile_bucket = pl.pallas_call(
        _plan_kernel,
        out_shape=[jax.ShapeDtypeStruct((n, n), jnp.int32)] * 2,
        name="moe_plan",
    )(bucket.reshape(n, n))
    return pos.reshape(TOKENS), tile_bucket[0, :MOE_NTILES]


def _invert_kernel(pos_ref, dump_ref, inv_ref):
    pltpu.sync_copy(dump_ref, inv_ref)

    def place(t, carry):
        inv_ref[pos_ref[t]] = t
        return carry

    lax.fori_loop(0, TOKENS, place, 0, unroll=16)


def _invert(pos):
    smem = pl.BlockSpec(memory_space=pltpu.SMEM)
    dump_rows = TOKENS + jnp.arange(INV_LEN, dtype=jnp.int32) % DUMP_ROWS
    return pl.pallas_call(
        _invert_kernel,
        in_specs=[smem, pl.BlockSpec(memory_space=pl.ANY)],
        out_specs=smem,
        out_shape=jax.ShapeDtypeStruct((INV_LEN,), jnp.int32),
        name="moe_invert",
    )(pos, dump_rows)


def _row_copy(src, src_row, dst, dst_row, sem):
    return pltpu.make_async_copy(src.at[pl.ds(src_row, 1), :], dst.at[pl.ds(dst_row, 1), :], sem)


def _experts_kernel(inv_ref, e_lo_ref, e_hi_ref, nused_ref, xa_ref, g_ref, fg_ref,
                    wg_lo, wu_lo, wd_lo, wg_hi, wu_hi, wd_hi, xo_ref,
                    xin, yout, gsem, ssem, fsem, *, final_norm):
    i = pl.program_id(0)
    n_used = nused_ref[0]
    s = lax.rem(i, MOE_BUFS)
    s_prev = lax.rem(i + 2, MOE_BUFS)
    s_next = lax.rem(i + 1, MOE_BUFS)

    def start_gather(tile, slot):
        for r in range(MOE_TILE):
            src = jnp.minimum(inv_ref[tile * MOE_TILE + r], TOKENS - 1)
            _row_copy(xa_ref, src, xin.at[slot], r, gsem.at[slot]).start(priority=r % 2)

    def start_scatter(tile, slot):
        for r in range(MOE_TILE):
            _row_copy(yout.at[slot], r, xo_ref, inv_ref[tile * MOE_TILE + r],
                      ssem.at[slot]).start(priority=r % 2)

    def wait_gather(slot):
        pltpu.make_async_copy(xa_ref.at[pl.ds(0, MOE_TILE), :], xin.at[slot], gsem.at[slot]).wait()

    def wait_scatter(slot):
        pltpu.make_async_copy(yout.at[slot], xo_ref.at[pl.ds(0, MOE_TILE), :], ssem.at[slot]).wait()

    @pl.when(i == 0)
    def _():
        yout[MOE_BUFS - 1] = jnp.zeros((MOE_TILE, D_MODEL), F32)
        for half in range(DUMP_ROWS // MOE_TILE):
            fill = pltpu.make_async_copy(
                yout.at[MOE_BUFS - 1], xo_ref.at[pl.ds(TOKENS + half * MOE_TILE, MOE_TILE), :], fsem)
            fill.start()
            fill.wait()
        start_gather(0, 0)
        start_gather(1, 1)

    @pl.when(i <= n_used)
    def _():
        wait_gather(s)

        @pl.when(i >= 2)
        def _():
            wait_scatter(s)

        start_gather(i + 2, s_prev)
        start_scatter(jnp.where(i >= 1, i - 1, SPARE_TILE), s_prev)

        xt = xin[s]
        x_rows = xt[:, 0:D_MODEL]
        rec = xt[:, D_MODEL:]
        h = _rms(x_rows, g_ref[...]).astype(BF16)

        def expert(wg, wu, wd, lane):
            hg = _dot(h, wg[...].astype(BF16))
            hu = _dot(h, wu[...].astype(BF16))
            act = jax.nn.silu(hg) * hu * rec[:, lane:lane + 1]
            return _dot(act.astype(BF16), wd[...].astype(BF16))

        out = x_rows + expert(wg_lo, wu_lo, wd_lo, REC_W_LO) + expert(wg_hi, wu_hi, wd_hi, REC_W_HI)
        yout[s] = _rms(out, fg_ref[...]) if final_norm else out

        @pl.when(i == n_used)
        def _():
            wait_gather(s_next)
            wait_gather(s_prev)
            wait_scatter(s_next)
            wait_scatter(s_prev)


def _experts(layer, inv, e_lo, e_hi, n_used, xa, g, final_g, w_gate, w_up, w_down):
    final_norm = final_g is not None
    if not final_norm:
        final_g = jnp.ones((1, D_MODEL), F32)
    up = lambda pick: pl.BlockSpec((None, None, D_MODEL, EXPERT_FF),
                                   lambda i, inv, lo, hi, nu: (layer, pick(lo, hi)[i], 0, 0))
    down = lambda pick: pl.BlockSpec((None, None, EXPERT_FF, D_MODEL),
                                     lambda i, inv, lo, hi, nu: (layer, pick(lo, hi)[i], 0, 0))
    first, second = (lambda lo, hi: lo), (lambda lo, hi: hi)
    grid_spec = pltpu.PrefetchScalarGridSpec(
        num_scalar_prefetch=4,
        grid=(MOE_NTILES + 1,),
        in_specs=[pl.BlockSpec(memory_space=pl.ANY), _const_spec((1, D_MODEL)), _const_spec((1, D_MODEL)),
                  up(first), up(first), down(first), up(second), up(second), down(second)],
        out_specs=pl.BlockSpec(memory_space=pl.ANY),
        scratch_shapes=[pltpu.VMEM((MOE_BUFS, MOE_TILE, MOE_ROW), F32),
                        pltpu.VMEM((MOE_BUFS, MOE_TILE, D_MODEL), F32),
                        pltpu.SemaphoreType.DMA((MOE_BUFS,)),
                        pltpu.SemaphoreType.DMA((MOE_BUFS,)),
                        pltpu.SemaphoreType.DMA(())])
    return pl.pallas_call(
        functools.partial(_experts_kernel, final_norm=final_norm),
        grid_spec=grid_spec,
        out_shape=jax.ShapeDtypeStruct((X_ROWS, D_MODEL), F32),
        compiler_params=_params("arbitrary"),
        name="moe_experts_final" if final_norm else "moe_experts",
    )(inv, e_lo, e_hi, n_used, xa, g, final_g, w_gate, w_up, w_down, w_gate, w_up, w_down)


def _bucket_experts():
    pairs = [(a, b) for a in range(EXPERTS_PER_GROUP) for b in range(a + 1, EXPERTS_PER_GROUP)]
    lo = [g * EXPERTS_PER_GROUP + a for g in range(MOE_GROUPS) for a, _ in pairs]
    hi = [g * EXPERTS_PER_GROUP + b for g in range(MOE_GROUPS) for _, b in pairs]
    return jnp.array(lo, jnp.int32), jnp.array(hi, jnp.int32)


def _routed_moe(layer, xa, bucket, g, w_gate, w_up, w_down, final_g=None):
    pos, tile_bucket = _plan(bucket)
    n_used = jnp.sum((tile_bucket < N_BUCKETS).astype(jnp.int32)).reshape(1)
    tile_bucket = jnp.minimum(jnp.concatenate([tile_bucket, tile_bucket[-1:]]), N_BUCKETS - 1)
    lo, hi = _bucket_experts()
    return _experts(layer, _invert(pos), lo[tile_bucket], hi[tile_bucket], n_used, xa, g, final_g,
                    w_gate, w_up, w_down)


def kernel(x, ln1_g, w_in, lam_re, lam_im, log_dt, b_re, b_im, c_re, c_im, d_skip, w_glu,
           gn_attn, gn_ssm, w_out, ln2_g, w_router_grp, b_router_grp, w_router_exp,
           b_router_exp, w_gate, w_up, w_down, final_g):
    assert x.shape == (BATCH, SEQ, D_MODEL)
    ng = MOE_GROUPS
    x = x.reshape(TOKENS, D_MODEL)

    for l in range(DEPTH):
        q, k, v, u = _inproj(x, ln1_g[l][None, :], w_in[l].astype(BF16))
        attn = _attention(q, k, v).reshape(TOKENS, ATTN_WIDTH)
        disc = _s5_discretize(lam_re[l], lam_im[l], log_dt[l], b_re[l], b_im[l], c_re[l], c_im[l])
        ssm = _s5(u, disc, d_skip[l].reshape(1, SSM_WIDTH), w_glu[l].astype(BF16))
        ssm = ssm.reshape(TOKENS, SSM_WIDTH)

        w_r = jnp.zeros((ROUTER_ROWS, D_MODEL), F32)
        w_r = w_r.at[0:ng].set(w_router_grp[l].T)
        w_r = w_r.at[EXPERT_ROW0:EXPERT_ROW0 + N_EXPERTS].set(
            jnp.transpose(w_router_exp[l], (0, 2, 1)).reshape(N_EXPERTS, D_MODEL))
        b_r = jnp.zeros((ROUTER_ROWS, 1), F32)
        b_r = b_r.at[0:ng, 0].set(b_router_grp[l])
        b_r = b_r.at[EXPERT_ROW0:EXPERT_ROW0 + N_EXPERTS, 0].set(b_router_exp[l].reshape(N_EXPERTS))
        w_r_hi, w_r_lo = _split_bf16(w_r)
        g2 = ln2_g[l][None, :]
        xa, bucket = _outproj(attn, ssm, x, gn_attn[l][None, :], gn_ssm[l][None, :],
                              w_out[l].astype(BF16), g2, w_r_hi, w_r_lo, b_r)
        x = _routed_moe(l, xa, bucket, g2, w_gate, w_up, w_down,
                        final_g[None, :] if l == DEPTH - 1 else None)
    return x[:TOKENS].reshape(BATCH, SEQ, D_MODEL)
```
